```python
import math
import jax, jax.numpy as jnp
from jax import lax
import numpy as np

D_MODEL = 1024
BATCH = 4
SEQ = 8192
DEPTH = 4

N_MIXERS = 3
N_LAYERS_A = (DEPTH + 2) // 3
N_LAYERS_B = (DEPTH + 1) // 3
N_LAYERS_C = DEPTH // 3

A_HEADS = 8
A_DK = 128
A_DV = 128
A_QK_W = A_HEADS * A_DK
A_V_W = A_HEADS * A_DV
A_CONV = 4
A_CHUNK = 64
A_IN_W = 2 * A_QK_W + 2 * A_V_W + 2 * A_HEADS

B_DH = 64
B_HEADS = D_MODEL // (2 * B_DH)
B_BLOCK = 128
REL_BUCKETS = 32
REL_MAX_DIST = 128

C_CHUNK = 128
C_HALF = 2 * D_MODEL
C_GROUPS = 8
C_GW = C_HALF // C_GROUPS

N_EXPERTS = 16
N_GROUPS = 4
EXPERTS_PER_GROUP = N_EXPERTS // N_GROUPS
TOP_K = 2
D_EXPERT = 512

DN_ALPHA = (2 * DEPTH) ** 0.25
DN_BETA = (8 * DEPTH) ** -0.25
EPS = 1e-5

kernel_name = "hybrid_deltanet_diffattn_gmlp_groupmoe"


def layer_norm(x, g, b):
    xf = x.astype(jnp.float32)
    mu = jnp.mean(xf, -1, keepdims=True)
    var = jnp.mean(jnp.square(xf - mu), -1, keepdims=True)
    return ((xf - mu) * lax.rsqrt(var + EPS) * g + b).astype(x.dtype)


def rms_norm(x, g):
    xf = x.astype(jnp.float32)
    return (xf * lax.rsqrt(jnp.mean(xf * xf, -1, keepdims=True) + EPS) * g).astype(x.dtype)


def l2norm(x):
    xf = x.astype(jnp.float32)
    return xf * lax.rsqrt(jnp.sum(xf * xf, -1, keepdims=True) + 1e-6)


def causal_short_conv(x, w):
    kw = w.shape[0]
    s = x.shape[1]
    xp = jnp.pad(x, ((0, 0), (kw - 1, 0), (0, 0)))
    y = xp[:, 0:s] * w[0]
    for j in range(1, kw):
        y = y + xp[:, j:j + s] * w[j]
    return y


def gated_delta_rule(q, k, v, g, beta):
    b, s, h, dk = q.shape
    dv = v.shape[-1]
    c = A_CHUNK
    n = s // c
    f32 = jnp.float32

    def to_chunks(t):
        t = t.astype(f32).reshape((b, n, c, h) + t.shape[3:])
        return jnp.moveaxis(t, 3, 1)

    q = to_chunks(q) * dk ** -0.5
    k = to_chunks(k)
    v = to_chunks(v)
    beta = to_chunks(beta)
    g = jnp.cumsum(to_chunks(g), axis=-1)
    idx = jnp.arange(c)
    causal = idx[:, None] >= idx[None, :]
    strict = idx[:, None] > idx[None, :]
    decay = jnp.exp(jnp.where(causal, g[..., :, None] - g[..., None, :], -jnp.inf))
    kb = k * beta[..., None]
    a_mat = jnp.where(strict, jnp.einsum('bhncd,bhnmd->bhncm', kb, k) * decay, 0.0)
    rhs = jnp.concatenate([v * beta[..., None], kb * jnp.exp(g)[..., None]], axis=-1)
    sol = lax.linalg.triangular_solve(a_mat + jnp.eye(c, dtype=f32), rhs,
                                      left_side=True, lower=True, unit_diagonal=True)
    u, w = sol[..., :dv], sol[..., dv:]
    attn = jnp.einsum('bhncd,bhnmd->bhncm', q, k) * decay
    q_dec = q * jnp.exp(g)[..., None]
    g_last = g[..., -1]
    k_dec = k * jnp.exp(g_last[..., None] - g)[..., None]

    def step(state, xs):
        attn_c, qd, kd, u_c, w_c, gl = xs
        v_new = u_c - jnp.einsum('bhck,bhkv->bhcv', w_c, state)
        o = jnp.einsum('bhck,bhkv->bhcv', qd, state) + jnp.einsum('bhcm,bhmv->bhcv', attn_c, v_new)
        state = state * jnp.exp(gl)[..., None, None] + jnp.einsum('bhck,bhcv->bhkv', kd, v_new)
        return state, o

    xs = tuple(jnp.moveaxis(t, 2, 0) for t in (attn, q_dec, k_dec, u, w, g_last))
    s0 = jnp.zeros((b, h, dk, dv), f32)
    _, o = lax.scan(step, s0, xs)
    return jnp.transpose(o, (1, 0, 3, 2, 4)).reshape(b, s, h, dv)


def gated_deltanet_mixer(x, w_in, conv_w, a_log, dt_bias, norm_g, w_out):
    b, s, _ = x.shape
    proj = x @ w_in
    qkv, gate, a_raw, b_raw = jnp.split(
        proj, [2 * A_QK_W + A_V_W, 2 * A_QK_W + 2 * A_V_W, 2 * A_QK_W + 2 * A_V_W + A_HEADS], axis=-1)
    qkv = jax.nn.silu(causal_short_conv(qkv, conv_w))
    q, k, v = jnp.split(qkv, [A_QK_W, 2 * A_QK_W], axis=-1)
    q = l2norm(q.reshape(b, s, A_HEADS, A_DK))
    k = l2norm(k.reshape(b, s, A_HEADS, A_DK))
    v = v.reshape(b, s, A_HEADS, A_DV)
    g = -jnp.exp(a_log.astype(jnp.float32)) * jax.nn.softplus(a_raw.astype(jnp.float32) + dt_bias)
    beta = jax.nn.sigmoid(b_raw.astype(jnp.float32))
    o = gated_delta_rule(q, k, v, g, beta)
    o = rms_norm(o, norm_g) * jax.nn.silu(gate.reshape(b, s, A_HEADS, A_DV).astype(jnp.float32))
    return o.reshape(b, s, A_V_W).astype(x.dtype) @ w_out


def rel_position_bucket(rel):
    n = jnp.maximum(rel, 0)
    max_exact = REL_BUCKETS // 2
    large = max_exact + (jnp.log(jnp.maximum(n, 1).astype(jnp.float32) / max_exact)
                         / math.log(REL_MAX_DIST / max_exact) * (REL_BUCKETS - max_exact)).astype(jnp.int32)
    large = jnp.minimum(large, REL_BUCKETS - 1)
    return jnp.where(n < max_exact, n, large)


def diff_attention_mixer(x, w_in, lam, norm_g, w_out, rel_bias, lambda_init):
    b, s, _ = x.shape
    q, k, v = jnp.split(x @ w_in, 3, axis=-1)
    q = q.reshape(b, s, B_HEADS, 2, B_DH)
    k = k.reshape(b, s, B_HEADS, 2, B_DH)
    v = v.reshape(b, s, B_HEADS, 2 * B_DH)
    lamf = lam.astype(jnp.float32)
    lam_full = jnp.exp(jnp.sum(lamf[0] * lamf[1])) - jnp.exp(jnp.sum(lamf[2] * lamf[3])) + lambda_init
    nblk = s // B_BLOCK
    qb = jnp.moveaxis(q.reshape(b, nblk, B_BLOCK, B_HEADS, 2, B_DH), 1, 0)
    kpos = jnp.arange(s)

    def block(args):
        qi, q_blk = args
        qpos = qi * B_BLOCK + jnp.arange(B_BLOCK)
        rel = qpos[:, None] - kpos[None, :]
        bias = rel_bias.astype(jnp.float32)[rel_position_bucket(rel)]
        logits = (jnp.einsum('bqhmd,bkhmd->bhmqk', q_blk, k).astype(jnp.float32) * B_DH ** -0.5
                  + jnp.transpose(bias, (2, 0, 1))[None, :, None])
        logits = jnp.where(rel >= 0, logits, -jnp.inf)
        p = jax.nn.softmax(logits, axis=-1)
        attn = p[:, :, 0] - lam_full * p[:, :, 1]
        return jnp.einsum('bhqk,bkhd->bqhd', attn.astype(v.dtype), v)

    o = lax.map(block, (jnp.arange(nblk), qb))
    o = jnp.moveaxis(o, 0, 1).reshape(b, s, B_HEADS, 2 * B_DH)
    o = rms_norm(o, norm_g) * (1.0 - lambda_init)
    return o.reshape(b, s, D_MODEL).astype(x.dtype) @ w_out


def chunked_gmlp_mixer(x, w_in, ln_g, ln_b, w_s, b_s, w_out):
    b, s, _ = x.shape
    z = jax.nn.gelu(x @ w_in, approximate=False)
    u, v = jnp.split(z, 2, axis=-1)
    v = layer_norm(v, ln_g, ln_b)
    n = s // C_CHUNK
    v = v.reshape(b, n, C_CHUNK, C_GROUPS, C_GW)
    idx = jnp.arange(C_CHUNK)
    w_causal = jnp.where(idx[:, None] >= idx[None, :], w_s, 0.0)
    mixed = jnp.einsum('gts,bnsgc->bntgc', w_causal, v) + jnp.transpose(b_s)[:, :, None]
    return (u * mixed.reshape(b, s, C_HALF)) @ w_out


def grouped_moe(x, w_router, router_bias, w1, w3, w2):
    b, s, d = x.shape
    t = x.reshape(b * s, d)
    scores = jax.nn.sigmoid((t @ w_router).astype(jnp.float32))
    sel = scores + router_bias.astype(jnp.float32)
    grp_score = jnp.sum(lax.top_k(sel.reshape(-1, N_GROUPS, EXPERTS_PER_GROUP), TOP_K)[0], -1)
    best = jnp.argmax(grp_score, -1)
    in_group = (jnp.arange(N_EXPERTS) // EXPERTS_PER_GROUP)[None, :] == best[:, None]
    _, top_idx = lax.top_k(jnp.where(in_group, sel, -jnp.inf), TOP_K)
    top_w = jnp.take_along_axis(scores, top_idx, -1)
    top_w = top_w / jnp.sum(top_w, -1, keepdims=True)
    gates = jnp.einsum('tk,tke->te', top_w, jax.nn.one_hot(top_idx, N_EXPERTS, dtype=jnp.float32))
    out = jnp.zeros_like(t)
    for e in range(N_EXPERTS):
        h = jax.nn.silu(t @ w1[e]) * (t @ w3[e])
        out = out + gates[:, e:e + 1].astype(t.dtype) * (h @ w2[e])
    return out.reshape(b, s, d)


def setup_inputs(seed: int = 0) -> dict:
    key = jax.random.key(seed)
    ks = iter(jax.random.split(key, 40))
    f32 = jnp.float32

    def nrm(shape, scale):
        return jax.random.normal(next(ks), shape, f32) * scale

    def gain(shape):
        return 1.0 + nrm(shape, 0.02)

    nA, nB, nC = N_LAYERS_A, N_LAYERS_B, N_LAYERS_C
    a_decay = jax.random.uniform(next(ks), (nA, A_HEADS), f32, 1.0, 16.0)
    dt = jnp.exp(jax.random.uniform(next(ks), (nA, A_HEADS), f32, math.log(1e-3), math.log(1e-1)))
    return {
        "x": nrm((BATCH, SEQ, D_MODEL), 1.0),
        "ln_mix_g": gain((DEPTH, D_MODEL)),
        "ln_mix_b": nrm((DEPTH, D_MODEL), 0.02),
        "ln_ffn_g": gain((DEPTH, D_MODEL)),
        "ln_ffn_b": nrm((DEPTH, D_MODEL), 0.02),
        "a_w_in": nrm((nA, D_MODEL, A_IN_W), D_MODEL ** -0.5),
        "a_conv": nrm((nA, A_CONV, 2 * A_QK_W + A_V_W), A_CONV ** -0.5),
        "a_a_log": jnp.log(a_decay),
        "a_dt_bias": dt + jnp.log(-jnp.expm1(-dt)),
        "a_norm_g": gain((nA, A_DV)),
        "a_w_out": nrm((nA, A_V_W, D_MODEL), A_V_W ** -0.5 * DN_BETA),
        "b_w_in": nrm((nB, D_MODEL, 3 * D_MODEL), D_MODEL ** -0.5),
        "b_lambda": nrm((nB, 4, B_DH), 0.1),
        "b_norm_g": gain((nB, 2 * B_DH)),
        "b_w_out": nrm((nB, D_MODEL, D_MODEL), D_MODEL ** -0.5 * DN_BETA),
        "rel_bias": nrm((REL_BUCKETS, B_HEADS), 0.2),
        "c_w_in": nrm((nC, D_MODEL, 2 * C_HALF), D_MODEL ** -0.5),
        "c_ln_g": gain((nC, C_HALF)),
        "c_ln_b": nrm((nC, C_HALF), 0.02),
        "c_w_s": nrm((nC, C_GROUPS, C_CHUNK, C_CHUNK), C_CHUNK ** -0.5),
        "c_b_s": gain((nC, C_GROUPS, C_CHUNK)),
        "c_w_out": nrm((nC, C_HALF, D_MODEL), C_HALF ** -0.5 * DN_BETA),
        "w_router": nrm((D_MODEL, N_EXPERTS), D_MODEL ** -0.5),
        "router_bias": nrm((N_EXPERTS,), 0.01),
        "e_w1": nrm((DEPTH, N_EXPERTS, D_MODEL, D_EXPERT), D_MODEL ** -0.5),
        "e_w3": nrm((DEPTH, N_EXPERTS, D_MODEL, D_EXPERT), D_MODEL ** -0.5),
        "e_w2": nrm((DEPTH, N_EXPERTS, D_EXPERT, D_MODEL), D_EXPERT ** -0.5 * DN_BETA),
    }


def reference(x, ln_mix_g, ln_mix_b, ln_ffn_g, ln_ffn_b,
              a_w_in, a_conv, a_a_log, a_dt_bias, a_norm_g, a_w_out,
              b_w_in, b_lambda, b_norm_g, b_w_out, rel_bias,
              c_w_in, c_ln_g, c_ln_b, c_w_s, c_b_s, c_w_out,
              w_router, router_bias, e_w1, e_w3, e_w2):
    h = x
    for i in range(DEPTH):
        kind, j = i % N_MIXERS, i // N_MIXERS
        if kind == 0:
            y = gated_deltanet_mixer(h, a_w_in[j], a_conv[j], a_a_log[j], a_dt_bias[j],
                                     a_norm_g[j], a_w_out[j])
        elif kind == 1:
            lambda_init = 0.8 - 0.6 * math.exp(-0.3 * i)
            y = diff_attention_mixer(h, b_w_in[j], b_lambda[j], b_norm_g[j], b_w_out[j],
                                     rel_bias, lambda_init)
        else:
            y = chunked_gmlp_mixer(h, c_w_in[j], c_ln_g[j], c_ln_b[j], c_w_s[j], c_b_s[j], c_w_out[j])
        h = layer_norm(DN_ALPHA * h + y, ln_mix_g[i], ln_mix_b[i])
        m = grouped_moe(h, w_router, router_bias, e_w1[i], e_w3[i], e_w2[i])
        h = layer_norm(DN_ALPHA * h + m, ln_ffn_g[i], ln_ffn_b[i])
    return h
```

```python
import functools
import math

import jax
import jax.numpy as jnp
import numpy as np
from jax import lax
from jax.experimental import pallas as pl
from jax.experimental.pallas import tpu as pltpu

F32 = jnp.float32
BF16 = jnp.bfloat16
I32 = jnp.int32

D_MODEL = 1024
DEPTH = 4
N_MIXERS = 3
A_HEADS = 8
A_DK = 128
A_CONV = 4
B_DH = 64
B_HEADS = D_MODEL // (2 * B_DH)
REL_BUCKETS = 32
REL_MAX_DIST = 128
C_CHUNK = 128
C_HALF = 2 * D_MODEL
C_GROUPS = 8
C_GW = C_HALF // C_GROUPS
N_EXPERTS = 16
N_GROUPS = 4
EXPERTS_PER_GROUP = 4
D_EXPERT = 512
DN_ALPHA = (2 * DEPTH) ** 0.25
EPS = 1e-5

LANES = 128
SUBLANES = 8
VMEM_LIMIT_BYTES = 56 * 1024 * 1024

ROW_TILE = 256
DN_CHUNK = 128
ATT_BLOCK = 256
MOE_TILE = 256
N_PAIR_CLASSES = N_GROUPS * 6
META_W = LANES
NEG_BIG = -1e30

_PAIR_LO = np.array([g * 4 + a for g in range(4) for a, b in ((0, 1), (0, 2), (0, 3), (1, 2), (1, 3), (2, 3))], np.int32)
_PAIR_HI = np.array([g * 4 + b for g in range(4) for a, b in ((0, 1), (0, 2), (0, 3), (1, 2), (1, 3), (2, 3))], np.int32)


def _cparams(sem):
    return pltpu.CompilerParams(dimension_semantics=sem, vmem_limit_bytes=VMEM_LIMIT_BYTES)


def _layer_norm(x, g, b):
    mu = jnp.mean(x, -1, keepdims=True)
    xc = x - mu
    var = jnp.mean(xc * xc, -1, keepdims=True)
    return xc * lax.rsqrt(var + EPS) * g + b


def _silu(x):
    return x * jax.nn.sigmoid(x)


def _dot(a, b):
    return jnp.dot(a, b, preferred_element_type=F32)


def _dot_nt(a, b):
    return lax.dot_general(a, b, (((1,), (1,)), ((), ())), preferred_element_type=F32)


def _split3(x):
    hi = x.astype(BF16)
    r = x - hi.astype(F32)
    mid = r.astype(BF16)
    lo = (r - mid.astype(F32)).astype(BF16)
    return hi, mid, lo


def _dn_in_kernel(x_ref, w_ref, wab_ref, conv_ref, alog_ref, dtb_ref,
                  q_ref, k_ref, v_ref, gate_ref, gb_ref, ext_ref):
    ts = x_ref.shape[0]
    qkv_w = 3 * D_MODEL
    s_idx = pl.program_id(1)

    @pl.when(s_idx == 0)
    def _():
        ext_ref[0:SUBLANES, :] = jnp.zeros((SUBLANES, qkv_w), F32)

    @pl.when(s_idx > 0)
    def _():
        ext_ref[0:SUBLANES, :] = ext_ref[ts:ts + SUBLANES, :]

    xb = x_ref[...].astype(BF16)
    ext_ref[SUBLANES:SUBLANES + ts, :] = _dot(xb, w_ref[:, 0:qkv_w])
    gate_ref[...] = _dot(xb, w_ref[:, qkv_w:qkv_w + D_MODEL]).astype(BF16)

    cw = conv_ref[...]
    y = ext_ref[SUBLANES - 3:SUBLANES - 3 + ts, :] * cw[0:1, :]
    for j in range(1, A_CONV):
        y = y + ext_ref[SUBLANES - 3 + j:SUBLANES - 3 + j + ts, :] * cw[j:j + 1, :]
    y = _silu(y)

    for h in range(A_HEADS):
        lo = h * A_DK
        qh = y[:, lo:lo + A_DK]
        kh = y[:, D_MODEL + lo:D_MODEL + lo + A_DK]
        qn = qh * lax.rsqrt(jnp.sum(qh * qh, -1, keepdims=True) + 1e-6)
        kn = kh * lax.rsqrt(jnp.sum(kh * kh, -1, keepdims=True) + 1e-6)
        q_ref[:, lo:lo + A_DK] = (qn * (A_DK ** -0.5)).astype(BF16)
        k_ref[:, lo:lo + A_DK] = kn.astype(BF16)
    v_ref[...] = y[:, 2 * D_MODEL:3 * D_MODEL].astype(BF16)

    ab = _dot(xb, wab_ref[...])
    z = ab + dtb_ref[...]
    softplus = jnp.maximum(z, 0.0) + jnp.log1p(jnp.exp(-jnp.abs(z)))
    g = -jnp.exp(alog_ref[...]) * softplus
    lane = lax.broadcasted_iota(I32, ab.shape, 1)
    gb_ref[...] = jnp.where(lane < A_HEADS, g, jax.nn.sigmoid(ab))


def _dn_in(x2d, batch, seq, w_big, w_ab, conv_w, alog_row, dtb_row):
    ts = ROW_TILE
    ns = seq // ts
    rows = lambda b, s: (b * ns + s, 0)
    const = lambda b, s: (0, 0)
    t = batch * seq
    out_bf = jax.ShapeDtypeStruct((t, D_MODEL), BF16)
    return pl.pallas_call(
        _dn_in_kernel,
        grid=(batch, ns),
        in_specs=[
            pl.BlockSpec((ts, D_MODEL), rows),
            pl.BlockSpec(w_big.shape, const),
            pl.BlockSpec(w_ab.shape, const),
            pl.BlockSpec(conv_w.shape, const),
            pl.BlockSpec(alog_row.shape, const),
            pl.BlockSpec(dtb_row.shape, const),
        ],
        out_specs=[pl.BlockSpec((ts, D_MODEL), rows)] * 4 + [pl.BlockSpec((ts, LANES), rows)],
        out_shape=[out_bf, out_bf, out_bf, out_bf, jax.ShapeDtypeStruct((t, LANES), F32)],
        scratch_shapes=[pltpu.VMEM((ts + 2 * SUBLANES, 3 * D_MODEL), F32)],
        compiler_params=_cparams(("arbitrary", "arbitrary")),
        name="dn_in",
    )(x2d, w_big, w_ab, conv_w, alog_row, dtb_row)


def _dn_scan_kernel(q_ref, k_ref, v_ref, gate_ref, gb_ref, ng_ref, o_ref, state_ref):
    c = q_ref.shape[0]
    n_doublings = int(math.log2(c)) - 1

    @pl.when(pl.program_id(1) == 0)
    def _():
        state_ref[...] = jnp.zeros(state_ref.shape, F32)

    row = lax.broadcasted_iota(I32, (c, c), 0)
    col = lax.broadcasted_iota(I32, (c, c), 1)
    causal = row >= col
    strict = row > col
    eye = (row == col).astype(F32)

    gb = gb_ref[...]
    tril = causal.astype(BF16)
    parts = _split3(gb)
    gc_all = _dot(tril, parts[0]) + _dot(tril, parts[1]) + _dot(tril, parts[2])
    gc_all_t = gc_all.T
    ng = ng_ref[...]

    for h in range(A_HEADS):
        lo = h * A_DK
        qh = q_ref[:, lo:lo + A_DK]
        kh = k_ref[:, lo:lo + A_DK]
        vh = v_ref[:, lo:lo + A_DK].astype(F32)
        khf = kh.astype(F32)
        gc_col = gc_all[:, h:h + 1]
        gc_row = gc_all_t[h:h + 1, :]
        beta = gb[:, A_HEADS + h:A_HEADS + h + 1]
        g_last = gc_all[c - 1:c, h:h + 1]

        decay = jnp.where(causal, jnp.exp(jnp.where(causal, gc_col - gc_row, 0.0)), 0.0)
        kq = _dot_nt(jnp.concatenate([kh, qh], axis=0), kh)
        a_mat = jnp.where(strict, kq[0:c] * beta * decay, 0.0)
        attn = kq[c:2 * c] * decay

        p = -a_mat
        t_inv = eye + p
        for _ in range(n_doublings):
            pb = p.astype(BF16)
            p = _dot(pb, pb)
            t_inv = t_inv + _dot(t_inv.astype(BF16), p.astype(BF16))

        e_gc = jnp.exp(gc_col)
        rhs = jnp.concatenate([vh * beta, khf * (beta * e_gc)], axis=1).astype(BF16)
        sol = _dot(t_inv.astype(BF16), rhs)
        u = sol[:, 0:A_DK]
        w = sol[:, A_DK:2 * A_DK]

        state = state_ref[h]
        sb = state.astype(BF16)
        q_dec = qh.astype(F32) * e_gc
        ws = _dot(jnp.concatenate([w, q_dec], axis=0).astype(BF16), sb)
        v_new = u - ws[0:c]
        o = ws[c:2 * c] + _dot(attn.astype(BF16), v_new.astype(BF16))
        k_dec = khf * jnp.exp(g_last - gc_col)
        state_ref[h] = state * jnp.exp(g_last) + _dot(k_dec.T.astype(BF16), v_new.astype(BF16))

        o = o * lax.rsqrt(jnp.mean(o * o, -1, keepdims=True) + EPS) * ng
        o_ref[:, lo:lo + A_DK] = (o * _silu(gate_ref[:, lo:lo + A_DK].astype(F32))).astype(BF16)


def _dn_scan(q, k, v, gate, gb, norm_g_row, batch, seq):
    c = DN_CHUNK
    nc = seq // c
    rows = lambda b, n: (b * nc + n, 0)
    const = lambda b, n: (0, 0)
    t = batch * seq
    return pl.pallas_call(
        _dn_scan_kernel,
        grid=(batch, nc),
        in_specs=[pl.BlockSpec((c, D_MODEL), rows)] * 4
        + [pl.BlockSpec((c, LANES), rows), pl.BlockSpec(norm_g_row.shape, const)],
        out_specs=pl.BlockSpec((c, D_MODEL), rows),
        out_shape=jax.ShapeDtypeStruct((t, D_MODEL), BF16),
        scratch_shapes=[pltpu.VMEM((A_HEADS, A_DK, A_DK), F32)],
        compiler_params=_cparams(("arbitrary", "arbitrary")),
        name="dn_scan",
    )(q, k, v, gate, gb, norm_g_row)


def _at_in_kernel(x_ref, w_ref, q0_ref, q1_ref, k_ref, v_ref):
    xb = x_ref[...].astype(BF16)
    q = _dot(xb, w_ref[:, 0:D_MODEL]) * (B_DH ** -0.5)
    lane = lax.broadcasted_iota(I32, q.shape, 1)
    first_map = (lane % (2 * B_DH)) < B_DH
    q0_ref[...] = jnp.where(first_map, q, 0.0).astype(BF16)
    q1_ref[...] = jnp.where(first_map, 0.0, q).astype(BF16)
    k_ref[...] = _dot(xb, w_ref[:, D_MODEL:2 * D_MODEL]).astype(BF16)
    v_ref[...] = _dot(xb, w_ref[:, 2 * D_MODEL:3 * D_MODEL]).astype(BF16)


def _at_in(x2d, w_in):
    t = x2d.shape[0]
    tm = ROW_TILE
    rows = lambda i: (i, 0)
    out_bf = jax.ShapeDtypeStruct((t, D_MODEL), BF16)
    return pl.pallas_call(
        _at_in_kernel,
        grid=(t // tm,),
        in_specs=[pl.BlockSpec((tm, D_MODEL), rows), pl.BlockSpec(w_in.shape, lambda i: (0, 0))],
        out_specs=[pl.BlockSpec((tm, D_MODEL), rows)] * 4,
        out_shape=[out_bf] * 4,
        compiler_params=_cparams(("parallel",)),
        name="at_in",
    )(x2d, w_in)


def _attn_kernel(sc_ref, q0_ref, q1_ref, k_ref, v_ref, bias_ref, ng_ref, o_ref,
                 m_ref, l_ref, acc_ref):
    tq = q0_ref.shape[0]
    tk = tq
    i = pl.program_id(2)
    lam = sc_ref[0]
    out_scale = sc_ref[1]

    m_ref[...] = jnp.full(m_ref.shape, NEG_BIG, F32)
    l_ref[...] = jnp.zeros(l_ref.shape, F32)
    acc_ref[...] = jnp.zeros(acc_ref.shape, F32)
    qs = (q0_ref[...], q1_ref[...])

    def block(j, bias):
        start = pl.multiple_of(j * tk, tk)
        kb = k_ref[pl.ds(start, tk), :]
        vb = v_ref[pl.ds(start, tk), :]
        for mp in range(2):
            s = _dot_nt(qs[mp], kb) + bias
            m_old = m_ref[mp]
            m_new = jnp.maximum(m_old, jnp.max(s, -1, keepdims=True))
            alpha = jnp.exp(m_old - m_new)
            p = jnp.exp(s - m_new)
            l_ref[mp] = alpha * l_ref[mp] + jnp.sum(p, -1, keepdims=True)
            acc_ref[mp] = alpha * acc_ref[mp] + _dot(p.astype(BF16), vb)
            m_ref[mp] = m_new

    def far_body(j, carry):
        block(j, bias_ref[0, 2])
        return carry

    lax.fori_loop(0, jnp.maximum(i - 1, 0), far_body, 0)

    @pl.when(i >= 1)
    def _():
        block(i - 1, bias_ref[0, 1])

    block(i, bias_ref[0, 0])

    o = acc_ref[0] / l_ref[0] - lam * (acc_ref[1] / l_ref[1])
    o = o * lax.rsqrt(jnp.mean(o * o, -1, keepdims=True) + EPS) * ng_ref[...] * out_scale
    o_ref[...] = o.astype(BF16)


def _attn(scalars, q0, q1, k, v, bias_tab, norm_g_row, batch, seq):
    tq = ATT_BLOCK
    nq = seq // tq
    t = batch * seq
    qmap = lambda b, h, i: (b * nq + i, h)
    kvmap = lambda b, h, i: (b, h)
    return pl.pallas_call(
        _attn_kernel,
        grid=(batch, B_HEADS, nq),
        in_specs=[
            pl.BlockSpec(memory_space=pltpu.SMEM),
            pl.BlockSpec((tq, LANES), qmap),
            pl.BlockSpec((tq, LANES), qmap),
            pl.BlockSpec((seq, LANES), kvmap),
            pl.BlockSpec((seq, LANES), kvmap),
            pl.BlockSpec((1, 3, tq, tq), lambda b, h, i: (h, 0, 0, 0)),
            pl.BlockSpec(norm_g_row.shape, lambda b, h, i: (0, 0)),
        ],
        out_specs=pl.BlockSpec((tq, LANES), qmap),
        out_shape=jax.ShapeDtypeStruct((t, D_MODEL), BF16),
        scratch_shapes=[pltpu.VMEM((2, tq, 1), F32), pltpu.VMEM((2, tq, 1), F32),
                        pltpu.VMEM((2, tq, LANES), F32)],
        compiler_params=_cparams(("parallel", "parallel", "parallel")),
        name="attn",
    )(scalars, q0, q1, k, v, bias_tab, norm_g_row)


def _rel_bucket(rel):
    n = jnp.maximum(rel, 0)
    max_exact = REL_BUCKETS // 2
    large = max_exact + (jnp.log(jnp.maximum(n, 1).astype(F32) / max_exact)
                         / math.log(REL_MAX_DIST / max_exact) * (REL_BUCKETS - max_exact)).astype(I32)
    large = jnp.minimum(large, REL_BUCKETS - 1)
    return jnp.where(n < max_exact, n, large)


def _attn_bias_table(rel_bias):
    tq = ATT_BLOCK
    assert tq + 1 >= REL_MAX_DIST, "blocks two back must fall in the last bucket"
    r = jnp.arange(tq)[:, None] - jnp.arange(tq)[None, :]
    table = rel_bias.astype(F32)
    diag = jnp.where((r >= 0)[..., None], table[_rel_bucket(r)], NEG_BIG)
    near = table[_rel_bucket(r + tq)]
    far = jnp.broadcast_to(table[REL_BUCKETS - 1], (tq, tq, B_HEADS))
    return jnp.transpose(jnp.stack([diag, near, far], 0), (3, 0, 1, 2))


def _gelu(x):
    return 0.5 * x * (1.0 + lax.erf(x * (2.0 ** -0.5)))


def _gm_kernel(x_ref, w_ref, lng_ref, lnb_ref, ws_ref, bs_ref, o_ref):
    tm = x_ref.shape[0]
    xb = x_ref[...].astype(BF16)
    v = _gelu(_dot(xb, w_ref[:, C_HALF:2 * C_HALF]))
    vn = _layer_norm(v, lng_ref[...], lnb_ref[...]).astype(BF16)
    bs = bs_ref[...]
    for g in range(C_GROUPS):
        lo = g * C_GW
        u = _gelu(_dot(xb, w_ref[:, lo:lo + C_GW]))
        wg = ws_ref[g]
        for ch in range(tm // C_CHUNK):
            r0 = ch * C_CHUNK
            mixed = _dot(wg, vn[r0:r0 + C_CHUNK, lo:lo + C_GW]) + bs[:, g:g + 1]
            o_ref[r0:r0 + C_CHUNK, lo:lo + C_GW] = (u[r0:r0 + C_CHUNK] * mixed).astype(BF16)


def _gm(x2d, w_in, ln_g_row, ln_b_row, ws_causal, bs_t):
    t = x2d.shape[0]
    tm = ROW_TILE
    rows = lambda i: (i, 0)
    const2 = lambda i: (0, 0)
    return pl.pallas_call(
        _gm_kernel,
        grid=(t // tm,),
        in_specs=[
            pl.BlockSpec((tm, D_MODEL), rows),
            pl.BlockSpec(w_in.shape, const2),
            pl.BlockSpec(ln_g_row.shape, const2),
            pl.BlockSpec(ln_b_row.shape, const2),
            pl.BlockSpec(ws_causal.shape, lambda i: (0, 0, 0)),
            pl.BlockSpec(bs_t.shape, const2),
        ],
        out_specs=pl.BlockSpec((tm, C_HALF), rows),
        out_shape=jax.ShapeDtypeStruct((t, C_HALF), BF16),
        compiler_params=_cparams(("parallel",)),
        name="gmlp",
    )(x2d, w_in, ln_g_row, ln_b_row, ws_causal, bs_t)


def _group_member(x, lane, k):
    fwd = pltpu.roll(x, LANES - k, 1)
    back = pltpu.roll(x, EXPERTS_PER_GROUP - k, 1)
    return jnp.where((lane % EXPERTS_PER_GROUP) + k < EXPERTS_PER_GROUP, fwd, back)


def _out_kernel(y_ref, w_ref, h_ref, lng_ref, lnb_ref, wr_ref, rb_ref, hx_ref, cnt_ref, carry_ref):
    tm = y_ref.shape[0]

    @pl.when(pl.program_id(0) == 0)
    def _():
        carry_ref[...] = jnp.zeros(carry_ref.shape, F32)

    hh = DN_ALPHA * h_ref[...] + _dot(y_ref[...], w_ref[...])
    h1 = _layer_norm(hh, lng_ref[...], lnb_ref[...])
    hx_ref[:, 0:D_MODEL] = h1

    h_hi = h1.astype(BF16)
    h_lo = (h1 - h_hi.astype(F32)).astype(BF16)
    both = _dot(h_hi, wr_ref[...])
    logits = both[:, 0:LANES] + both[:, LANES:2 * LANES] + _dot(h_lo, wr_ref[:, 0:LANES])

    lane = lax.broadcasted_iota(I32, (tm, LANES), 1)
    lane_f = lane.astype(F32)
    real = lane < N_EXPERTS
    scores = jax.nn.sigmoid(logits)
    sel = jnp.where(real, scores + rb_ref[...], NEG_BIG)

    member = lane % EXPERTS_PER_GROUP
    others = [_group_member(sel, lane, k) for k in (1, 2, 3)]
    grp = jnp.maximum(jnp.maximum(sel + others[0], sel + others[1]),
                      jnp.maximum(sel + others[2], others[0] + others[1]))
    grp = jnp.maximum(grp, jnp.maximum(others[0] + others[2], others[1] + others[2]))
    grp = jnp.where(real, grp, NEG_BIG)
    gmax = jnp.max(grp, -1, keepdims=True)
    group_id = (lane // EXPERTS_PER_GROUP).astype(F32)
    best = jnp.min(jnp.where(grp == gmax, group_id, 1e9), -1, keepdims=True)
    beaten = jnp.zeros((tm, LANES), F32)
    for k, o in zip((1, 2, 3), others):
        o_member = (member + k) % EXPERTS_PER_GROUP
        ahead = (o > sel) | ((o == sel) & (o_member < member))
        beaten = beaten + ahead.astype(F32)
    chosen = real & (group_id == best) & (beaten < 2.0)

    e_lo = jnp.min(jnp.where(chosen, lane_f, 1e9), -1, keepdims=True)
    e_hi = jnp.max(jnp.where(chosen, lane_f, -1.0), -1, keepdims=True)
    s_lo = jnp.sum(jnp.where(lane_f == e_lo, scores, 0.0), -1, keepdims=True)
    s_hi = jnp.sum(jnp.where(lane_f == e_hi, scores, 0.0), -1, keepdims=True)
    denom = s_lo + s_hi
    a = e_lo - best * EXPERTS_PER_GROUP
    b = e_hi - best * EXPERTS_PER_GROUP
    cls = best * 6.0 + (a * 3.0 - a * (a - 1.0) * 0.5) + (b - a - 1.0)

    onehot = lane_f == cls
    r_i = lax.broadcasted_iota(I32, (tm, tm), 0)
    c_i = lax.broadcasted_iota(I32, (tm, tm), 1)
    before = _dot((r_i > c_i).astype(BF16), onehot.astype(BF16)) + carry_ref[0:1, :]
    rank = jnp.sum(jnp.where(onehot, before, 0.0), -1, keepdims=True)
    carry_ref[0:1, :] = carry_ref[0:1, :] + jnp.sum(onehot.astype(F32), 0, keepdims=True)
    cnt_ref[...] = carry_ref[...]

    meta = jnp.where(lane == 0, s_lo / denom, 0.0)
    meta = jnp.where(lane == 1, s_hi / denom, meta)
    meta = jnp.where(lane == 2, cls, meta)
    meta = jnp.where(lane == 3, rank, meta)
    hx_ref[:, D_MODEL:D_MODEL + META_W] = meta


def _out_proj(y, w_out, h2d, ln_g_row, ln_b_row, wr_split, rbias_row):
    t, kd = y.shape
    tm = ROW_TILE
    rows = lambda i: (i, 0)
    const = lambda i: (0, 0)
    return pl.pallas_call(
        _out_kernel,
        grid=(t // tm,),
        in_specs=[
            pl.BlockSpec((tm, kd), rows),
            pl.BlockSpec(w_out.shape, const),
            pl.BlockSpec((tm, D_MODEL), rows),
            pl.BlockSpec(ln_g_row.shape, const),
            pl.BlockSpec(ln_b_row.shape, const),
            pl.BlockSpec(wr_split.shape, const),
            pl.BlockSpec(rbias_row.shape, const),
        ],
        out_specs=[pl.BlockSpec((tm, D_MODEL + META_W), rows), pl.BlockSpec((SUBLANES, LANES), const)],
        out_shape=[jax.ShapeDtypeStruct((t, D_MODEL + META_W), F32),
                   jax.ShapeDtypeStruct((SUBLANES, LANES), F32)],
        scratch_shapes=[pltpu.VMEM((SUBLANES, LANES), F32)],
        compiler_params=_cparams(("arbitrary",)),
        name="out_proj",
    )(y, w_out, h2d, ln_g_row, ln_b_row, wr_split, rbias_row)


def _row_copies_wait(src_ref, dst_ref, sem):
    pltpu.make_async_copy(src_ref, dst_ref, sem).wait()


def _dispatch_kernel(pos_ref, hx_ref, xs_in_ref, xs_ref, sem):
    del xs_in_ref
    tm = hx_ref.shape[0]

    def issue(r, carry):
        p = pos_ref[0, 0, r]
        pltpu.make_async_copy(hx_ref.at[pl.ds(r, 1), :], xs_ref.at[pl.ds(p, 1), :], sem).start()
        return carry

    lax.fori_loop(0, tm, issue, 0)
    _row_copies_wait(hx_ref, xs_ref.at[pl.ds(0, tm), :], sem)


def _dispatch(pos, hx, n_slots):
    t, w = hx.shape
    tm = ROW_TILE
    xs_init = jnp.zeros((n_slots, w), F32)
    return pl.pallas_call(
        _dispatch_kernel,
        grid=(t // tm,),
        in_specs=[
            pl.BlockSpec((1, 1, tm), lambda i: (i, 0, 0), memory_space=pltpu.SMEM),
            pl.BlockSpec((tm, w), lambda i: (i, 0)),
            pl.BlockSpec(memory_space=pl.ANY),
        ],
        out_specs=pl.BlockSpec(memory_space=pl.ANY),
        out_shape=jax.ShapeDtypeStruct((n_slots, w), F32),
        scratch_shapes=[pltpu.SemaphoreType.DMA(())],
        input_output_aliases={2: 0},
        compiler_params=_cparams(("arbitrary",)),
        name="dispatch",
    )(pos, hx, xs_init)


def _moe_kernel(lo_ref, hi_ref, nv_ref, xs_ref, w1l_ref, w3l_ref, w2l_ref, w1h_ref, w3h_ref, w2h_ref,
                lng_ref, lnb_ref, ys_ref):
    del lo_ref, hi_ref

    @pl.when(pl.program_id(0) < nv_ref[0])
    def _():
        x = xs_ref[:, 0:D_MODEL]
        xb = x.astype(BF16)

        def expert(w1_ref, w3_ref, w2_ref):
            hid = _silu(_dot(xb, w1_ref[0])) * _dot(xb, w3_ref[0])
            return _dot(hid.astype(BF16), w2_ref[0])

        m = xs_ref[:, D_MODEL:D_MODEL + 1] * expert(w1l_ref, w3l_ref, w2l_ref)
        m = m + xs_ref[:, D_MODEL + 1:D_MODEL + 2] * expert(w1h_ref, w3h_ref, w2h_ref)
        ys_ref[...] = _layer_norm(DN_ALPHA * x + m, lng_ref[...], lnb_ref[...])

    @pl.when(pl.program_id(0) >= nv_ref[0])
    def _():
        ys_ref[...] = jnp.zeros(ys_ref.shape, F32)


def _moe(tile_lo, tile_hi, n_valid, xs, w1, w3, w2, ln_g_row, ln_b_row):
    n_slots, w = xs.shape
    tm = MOE_TILE
    rows = lambda i, lo, hi, nv: (jnp.minimum(i, nv[0] - 1), 0)
    w_lo = lambda i, lo, hi, nv: (lo[i], 0, 0)
    w_hi = lambda i, lo, hi, nv: (hi[i], 0, 0)
    const = lambda i, lo, hi, nv: (0, 0)
    up = (1, D_MODEL, D_EXPERT)
    down = (1, D_EXPERT, D_MODEL)
    grid_spec = pltpu.PrefetchScalarGridSpec(
        num_scalar_prefetch=3,
        grid=(n_slots // tm,),
        in_specs=[
            pl.BlockSpec((tm, w), rows),
            pl.BlockSpec(up, w_lo), pl.BlockSpec(up, w_lo), pl.BlockSpec(down, w_lo),
            pl.BlockSpec(up, w_hi), pl.BlockSpec(up, w_hi), pl.BlockSpec(down, w_hi),
            pl.BlockSpec(ln_g_row.shape, const), pl.BlockSpec(ln_b_row.shape, const),
        ],
        out_specs=pl.BlockSpec((tm, D_MODEL), lambda i, lo, hi, nv: (i, 0)),
    )
    return pl.pallas_call(
        _moe_kernel,
        grid_spec=grid_spec,
        out_shape=jax.ShapeDtypeStruct((n_slots, D_MODEL), F32),
        compiler_params=_cparams(("arbitrary",)),
        name="moe",
    )(tile_lo, tile_hi, n_valid, xs, w1, w3, w2, w1, w3, w2, ln_g_row, ln_b_row)


def _unpermute_kernel(pos_ref, ys_ref, o_ref, sem):
    tm = o_ref.shape[0]

    def issue(r, carry):
        p = pos_ref[0, 0, r]
        pltpu.make_async_copy(ys_ref.at[pl.ds(p, 1), :], o_ref.at[pl.ds(r, 1), :], sem).start()
        return carry

    lax.fori_loop(0, tm, issue, 0)
    _row_copies_wait(ys_ref.at[pl.ds(0, tm), :], o_ref, sem)


def _unpermute(pos, ys, t):
    tm = ROW_TILE
    return pl.pallas_call(
        _unpermute_kernel,
        grid=(t // tm,),
        in_specs=[
            pl.BlockSpec((1, 1, tm), lambda i: (i, 0, 0), memory_space=pltpu.SMEM),
            pl.BlockSpec(memory_space=pl.ANY),
        ],
        out_specs=pl.BlockSpec((tm, D_MODEL), lambda i: (i, 0)),
        out_shape=jax.ShapeDtypeStruct((t, D_MODEL), F32),
        scratch_shapes=[pltpu.SemaphoreType.DMA(())],
        compiler_params=_cparams(("arbitrary",)),
        name="unpermute",
    )(pos, ys)


def _routing_plan(hx, counts, n_slots):
    cls = hx[:, D_MODEL + 2].astype(I32)
    rank = hx[:, D_MODEL + 3].astype(I32)
    cnt = counts[0, :N_PAIR_CLASSES].astype(I32)
    padded = ((cnt + MOE_TILE - 1) // MOE_TILE) * MOE_TILE
    end = jnp.cumsum(padded)
    pos = (end - padded)[cls] + rank
    n_tiles = n_slots // MOE_TILE
    tile_cls = jnp.searchsorted(end, jnp.arange(n_tiles, dtype=I32) * MOE_TILE, side="right")
    tile_cls = jnp.minimum(tile_cls, N_PAIR_CLASSES - 1).astype(I32)
    n_valid = (end[-1] // MOE_TILE).astype(I32).reshape(1)
    last_cls = tile_cls[jnp.maximum(n_valid[0] - 1, 0)]
    tile_cls = jnp.where(jnp.arange(n_tiles) < n_valid[0], tile_cls, last_cls)
    return pos, jnp.asarray(_PAIR_LO)[tile_cls], jnp.asarray(_PAIR_HI)[tile_cls], n_valid


def _moe_layer(hx, counts, w1, w3, w2, ln_g_row, ln_b_row):
    t = hx.shape[0]
    n_slots = t + N_PAIR_CLASSES * MOE_TILE
    pos, tile_lo, tile_hi, n_valid = _routing_plan(hx, counts, n_slots)
    pos = pos.reshape(t // ROW_TILE, 1, ROW_TILE)
    xs = _dispatch(pos, hx, n_slots)
    ys = _moe(tile_lo, tile_hi, n_valid, xs, w1, w3, w2, ln_g_row, ln_b_row)
    return _unpermute(pos, ys, t)


def _row(v, width=None):
    v = v.astype(F32).reshape(1, -1)
    if width is not None and v.shape[1] < width:
        v = jnp.pad(v, ((0, 0), (0, width - v.shape[1])))
    return v


def _deltanet(h2d, batch, seq, w_in, conv_w, a_log, dt_bias, norm_g):
    qkvg = 4 * D_MODEL
    w_big = w_in[:, :qkvg].astype(BF16)
    w_ab = jnp.pad(w_in[:, qkvg:], ((0, 0), (0, LANES - 2 * A_HEADS))).astype(BF16)
    q, k, v, gate, gb = _dn_in(h2d, batch, seq, w_big, w_ab, conv_w.astype(F32),
                               _row(a_log, LANES), _row(dt_bias, LANES))
    return _dn_scan(q, k, v, gate, gb, _row(norm_g), batch, seq)


def _diff_attention(h2d, batch, seq, w_in, lam, norm_g, rel_bias, lambda_init):
    lamf = lam.astype(F32)
    lam_full = jnp.exp(jnp.sum(lamf[0] * lamf[1])) - jnp.exp(jnp.sum(lamf[2] * lamf[3])) + lambda_init
    scalars = jnp.stack([lam_full, jnp.asarray(1.0 - lambda_init, F32)]).astype(F32)
    q0, q1, k, v = _at_in(h2d, w_in.astype(BF16))
    return _attn(scalars, q0, q1, k, v, _attn_bias_table(rel_bias), _row(norm_g), batch, seq)


def _gmlp(h2d, w_in, ln_g, ln_b, w_s, b_s):
    ws_causal = jnp.tril(w_s.astype(F32)).astype(BF16)
    bs_t = jnp.pad(jnp.transpose(b_s.astype(F32)), ((0, 0), (0, LANES - C_GROUPS)))
    return _gm(h2d, w_in.astype(BF16), _row(ln_g), _row(ln_b), ws_causal, bs_t)


def kernel(x, ln_mix_g, ln_mix_b, ln_ffn_g, ln_ffn_b, a_w_in, a_conv, a_a_log, a_dt_bias, a_norm_g, a_w_out, b_w_in, b_lambda, b_norm_g, b_w_out, rel_bias, c_w_in, c_ln_g, c_ln_b, c_w_s, c_b_s, c_w_out, w_router, router_bias, e_w1, e_w3, e_w2):
    batch, seq, d = x.shape
    assert d == D_MODEL and seq % max(ROW_TILE, ATT_BLOCK, DN_CHUNK) == 0
    t = batch * seq
    h = x.reshape(t, d).astype(F32)

    wr = jnp.pad(w_router.astype(F32), ((0, 0), (0, LANES - N_EXPERTS)))
    wr_hi = wr.astype(BF16)
    wr_lo = (wr - wr_hi.astype(F32)).astype(BF16)
    wr_split = jnp.concatenate([wr_hi, wr_lo], axis=1)
    rbias_row = _row(router_bias, LANES)

    for i in range(DEPTH):
        kind, j = i % N_MIXERS, i // N_MIXERS
        if kind == 0:
            y = _deltanet(h, batch, seq, a_w_in[j], a_conv[j], a_a_log[j], a_dt_bias[j], a_norm_g[j])
            w_out = a_w_out[j]
        elif kind == 1:
            lambda_init = 0.8 - 0.6 * math.exp(-0.3 * i)
            y = _diff_attention(h, batch, seq, b_w_in[j], b_lambda[j], b_norm_g[j], rel_bias, lambda_init)
            w_out = b_w_out[j]
        else:
            y = _gmlp(h, c_w_in[j], c_ln_g[j], c_ln_b[j], c_w_s[j], c_b_s[j])
            w_out = c_w_out[j]
        hx, counts = _out_proj(y, w_out.astype(BF16), h, _row(ln_mix_g[i]), _row(ln_mix_b[i]),
                               wr_split, rbias_row)
        h = _moe_layer(hx, counts, e_w1[i].astype(BF16), e_w3[i].astype(BF16), e_w2[i].astype(BF16),
                       _row(ln_ffn_g[i]), _row(ln_ffn_b[i]))
    return h.reshape(batch, seq, d).astype(x.dtype)
```

```python
import functools
import math

import jax
import jax.numpy as jnp
import numpy as np
from jax import lax
from jax.experimental import pallas as pl
from jax.experimental.pallas import tpu as pltpu

F32 = jnp.float32
BF16 = jnp.bfloat16
I32 = jnp.int32

D_MODEL = 1024
DEPTH = 4
N_MIXERS = 3
A_HEADS = 8
A_DK = 128
A_CONV = 4
B_DH = 64
B_HEADS = D_MODEL // (2 * B_DH)
REL_BUCKETS = 32
REL_MAX_DIST = 128
C_CHUNK = 128
C_HALF = 2 * D_MODEL
C_GROUPS = 8
C_GW = C_HALF // C_GROUPS
N_EXPERTS = 16
N_GROUPS = 4
EXPERTS_PER_GROUP = 4
D_EXPERT = 512
DN_ALPHA = (2 * DEPTH) ** 0.25
EPS = 1e-5

LANES = 128
SUBLANES = 8
VMEM_LIMIT_BYTES = 56 * 1024 * 1024

ROW_TILE = 256
DN_CHUNK = 128
ATT_BLOCK = 512
MOE_TILE = 256
N_PAIR_CLASSES = N_GROUPS * 6
META_W = LANES
NEG_BIG = -1e30
LOG2E = math.log2(math.e)

_PAIR_LO = np.array([g * 4 + a for g in range(4) for a, b in ((0, 1), (0, 2), (0, 3), (1, 2), (1, 3), (2, 3))], np.int32)
_PAIR_HI = np.array([g * 4 + b for g in range(4) for a, b in ((0, 1), (0, 2), (0, 3), (1, 2), (1, 3), (2, 3))], np.int32)


def _cparams(sem):
    return pltpu.CompilerParams(dimension_semantics=sem, vmem_limit_bytes=VMEM_LIMIT_BYTES)


def _layer_norm(x, g, b):
    mu = jnp.mean(x, -1, keepdims=True)
    xc = x - mu
    var = jnp.mean(xc * xc, -1, keepdims=True)
    return xc * lax.rsqrt(var + EPS) * g + b


def _silu(x):
    return x * jax.nn.sigmoid(x)


def _dot(a, b):
    return jnp.dot(a, b, preferred_element_type=F32)


def _dot_nt(a, b):
    return lax.dot_general(a, b, (((1,), (1,)), ((), ())), preferred_element_type=F32)


def _split3(x):
    hi = x.astype(BF16)
    r = x - hi.astype(F32)
    mid = r.astype(BF16)
    lo = (r - mid.astype(F32)).astype(BF16)
    return hi, mid, lo


def _dn_in_kernel(x_ref, w_ref, wab_ref, conv_ref, alog_ref, dtb_ref,
                  q_ref, k_ref, v_ref, gate_ref, gb_ref, ext_ref):
    ts = x_ref.shape[0]
    qkv_w = 3 * D_MODEL
    s_idx = pl.program_id(1)

    @pl.when(s_idx == 0)
    def _():
        ext_ref[0:SUBLANES, :] = jnp.zeros((SUBLANES, qkv_w), F32)

    @pl.when(s_idx > 0)
    def _():
        ext_ref[0:SUBLANES, :] = ext_ref[ts:ts + SUBLANES, :]

    xb = x_ref[...].astype(BF16)
    ext_ref[SUBLANES:SUBLANES + ts, :] = _dot(xb, w_ref[:, 0:qkv_w])
    gate_ref[...] = _dot(xb, w_ref[:, qkv_w:qkv_w + D_MODEL]).astype(BF16)

    cw = conv_ref[...]
    y = ext_ref[SUBLANES - 3:SUBLANES - 3 + ts, :] * cw[0:1, :]
    for j in range(1, A_CONV):
        y = y + ext_ref[SUBLANES - 3 + j:SUBLANES - 3 + j + ts, :] * cw[j:j + 1, :]
    y = _silu(y)

    for h in range(A_HEADS):
        lo = h * A_DK
        qh = y[:, lo:lo + A_DK]
        kh = y[:, D_MODEL + lo:D_MODEL + lo + A_DK]
        qn = qh * lax.rsqrt(jnp.sum(qh * qh, -1, keepdims=True) + 1e-6)
        kn = kh * lax.rsqrt(jnp.sum(kh * kh, -1, keepdims=True) + 1e-6)
        q_ref[:, lo:lo + A_DK] = (qn * (A_DK ** -0.5)).astype(BF16)
        k_ref[:, lo:lo + A_DK] = kn.astype(BF16)
    v_ref[...] = y[:, 2 * D_MODEL:3 * D_MODEL].astype(BF16)

    ab = _dot(xb, wab_ref[...])
    z = ab + dtb_ref[...]
    softplus = jnp.maximum(z, 0.0) + jnp.log1p(jnp.exp(-jnp.abs(z)))
    g = -jnp.exp(alog_ref[...]) * softplus
    lane = lax.broadcasted_iota(I32, ab.shape, 1)
    gb_ref[...] = jnp.where(lane < A_HEADS, g, jax.nn.sigmoid(ab))


def _dn_in(x2d, batch, seq, w_big, w_ab, conv_w, alog_row, dtb_row):
    ts = ROW_TILE
    ns = seq // ts
    rows = lambda b, s: (b * ns + s, 0)
    const = lambda b, s: (0, 0)
    t = batch * seq
    out_bf = jax.ShapeDtypeStruct((t, D_MODEL), BF16)
    return pl.pallas_call(
        _dn_in_kernel,
        grid=(batch, ns),
        in_specs=[
            pl.BlockSpec((ts, D_MODEL), rows),
            pl.BlockSpec(w_big.shape, const),
            pl.BlockSpec(w_ab.shape, const),
            pl.BlockSpec(conv_w.shape, const),
            pl.BlockSpec(alog_row.shape, const),
            pl.BlockSpec(dtb_row.shape, const),
        ],
        out_specs=[pl.BlockSpec((ts, D_MODEL), rows)] * 4 + [pl.BlockSpec((ts, LANES), rows)],
        out_shape=[out_bf, out_bf, out_bf, out_bf, jax.ShapeDtypeStruct((t, LANES), F32)],
        scratch_shapes=[pltpu.VMEM((ts + 2 * SUBLANES, 3 * D_MODEL), F32)],
        compiler_params=_cparams(("arbitrary", "arbitrary")),
        name="dn_in",
    )(x2d, w_big, w_ab, conv_w, alog_row, dtb_row)


def _dn_scan_kernel(q_ref, k_ref, v_ref, gate_ref, gb_ref, ng_ref, o_ref, state_ref):
    c = q_ref.shape[0]
    n_doublings = int(math.log2(c)) - 1

    @pl.when(pl.program_id(1) == 0)
    def _():
        state_ref[...] = jnp.zeros(state_ref.shape, F32)

    row = lax.broadcasted_iota(I32, (c, c), 0)
    col = lax.broadcasted_iota(I32, (c, c), 1)
    causal = row >= col
    strict = row > col
    eye = (row == col).astype(F32)

    gb = gb_ref[...]
    tril = causal.astype(BF16)
    parts = _split3(gb)
    gc_all = _dot(tril, parts[0]) + _dot(tril, parts[1]) + _dot(tril, parts[2])
    gc_all_t = gc_all.T
    ng = ng_ref[...]

    heads = range(A_HEADS)
    lanes = [slice(h * A_DK, (h + 1) * A_DK) for h in heads]
    qh = [q_ref[:, lanes[h]] for h in heads]
    kh = [k_ref[:, lanes[h]] for h in heads]
    gc_col = [gc_all[:, h:h + 1] for h in heads]
    beta = [gb[:, A_HEADS + h:A_HEADS + h + 1] for h in heads]
    g_last = [gc_all[c - 1:c, h:h + 1] for h in heads]

    kq = [_dot_nt(jnp.concatenate([kh[h], qh[h]], axis=0), kh[h]) for h in heads]
    decay = [jnp.where(causal, jnp.exp(jnp.where(causal, gc_col[h] - gc_all_t[h:h + 1, :], 0.0)), 0.0)
             for h in heads]
    attn = [(kq[h][c:2 * c] * decay[h]).astype(BF16) for h in heads]

    p = [jnp.where(strict, -(kq[h][0:c] * beta[h] * decay[h]), 0.0) for h in heads]
    t_inv = [eye + p[h] for h in heads]
    for _ in range(n_doublings):
        pb = [p[h].astype(BF16) for h in heads]
        p = [_dot(pb[h], pb[h]) for h in heads]
        t_inv = [t_inv[h] + _dot(t_inv[h].astype(BF16), p[h].astype(BF16)) for h in heads]

    e_gc = [jnp.exp(gc_col[h]) for h in heads]
    khf = [kh[h].astype(F32) for h in heads]
    rhs = [jnp.concatenate([v_ref[:, lanes[h]].astype(F32) * beta[h], khf[h] * (beta[h] * e_gc[h])],
                           axis=1).astype(BF16) for h in heads]
    sol = [_dot(t_inv[h].astype(BF16), rhs[h]) for h in heads]

    state = [state_ref[h] for h in heads]
    lhs = [jnp.concatenate([sol[h][:, A_DK:2 * A_DK], qh[h].astype(F32) * e_gc[h]], axis=0).astype(BF16)
           for h in heads]
    ws = [_dot(lhs[h], state[h].astype(BF16)) for h in heads]
    v_new = [(sol[h][:, 0:A_DK] - ws[h][0:c]).astype(BF16) for h in heads]
    k_dec_t = [(khf[h] * jnp.exp(g_last[h] - gc_col[h])).T.astype(BF16) for h in heads]
    o = [ws[h][c:2 * c] + _dot(attn[h], v_new[h]) for h in heads]
    for h in heads:
        state_ref[h] = state[h] * jnp.exp(g_last[h]) + _dot(k_dec_t[h], v_new[h])
    for h in heads:
        on = o[h] * lax.rsqrt(jnp.mean(o[h] * o[h], -1, keepdims=True) + EPS) * ng
        o_ref[:, lanes[h]] = (on * _silu(gate_ref[:, lanes[h]].astype(F32))).astype(BF16)


def _dn_scan(q, k, v, gate, gb, norm_g_row, batch, seq):
    c = DN_CHUNK
    nc = seq // c
    rows = lambda b, n: (b * nc + n, 0)
    const = lambda b, n: (0, 0)
    t = batch * seq
    return pl.pallas_call(
        _dn_scan_kernel,
        grid=(batch, nc),
        in_specs=[pl.BlockSpec((c, D_MODEL), rows)] * 4
        + [pl.BlockSpec((c, LANES), rows), pl.BlockSpec(norm_g_row.shape, const)],
        out_specs=pl.BlockSpec((c, D_MODEL), rows),
        out_shape=jax.ShapeDtypeStruct((t, D_MODEL), BF16),
        scratch_shapes=[pltpu.VMEM((A_HEADS, A_DK, A_DK), F32)],
        compiler_params=_cparams(("arbitrary", "arbitrary")),
        name="dn_scan",
    )(q, k, v, gate, gb, norm_g_row)


def _at_in_kernel(x_ref, w_ref, qt_ref, k_ref, vt_ref):
    xb = x_ref[...].astype(BF16)
    q = _dot(xb, w_ref[:, 0:D_MODEL]) * (B_DH ** -0.5 * LOG2E)
    lane = lax.broadcasted_iota(I32, q.shape, 1)
    first_map = (lane % (2 * B_DH)) < B_DH
    qt_ref[0] = jnp.where(first_map, q, 0.0).T.astype(BF16)
    qt_ref[1] = jnp.where(first_map, 0.0, q).T.astype(BF16)
    k_ref[...] = _dot(xb, w_ref[:, D_MODEL:2 * D_MODEL]).astype(BF16)
    vt_ref[...] = _dot(xb, w_ref[:, 2 * D_MODEL:3 * D_MODEL]).T.astype(BF16)


def _at_in(x2d, w_in):
    t = x2d.shape[0]
    tm = ROW_TILE
    return pl.pallas_call(
        _at_in_kernel,
        grid=(t // tm,),
        in_specs=[pl.BlockSpec((tm, D_MODEL), lambda i: (i, 0)), pl.BlockSpec(w_in.shape, lambda i: (0, 0))],
        out_specs=[pl.BlockSpec((2, D_MODEL, tm), lambda i: (0, 0, i)),
                   pl.BlockSpec((tm, D_MODEL), lambda i: (i, 0)),
                   pl.BlockSpec((D_MODEL, tm), lambda i: (0, i))],
        out_shape=[jax.ShapeDtypeStruct((2, D_MODEL, t), BF16),
                   jax.ShapeDtypeStruct((t, D_MODEL), BF16),
                   jax.ShapeDtypeStruct((D_MODEL, t), BF16)],
        compiler_params=_cparams(("parallel",)),
        name="at_in",
    )(x2d, w_in)


def _attn_kernel(sc_ref, q0_ref, q1_ref, k_ref, vt_ref, bias_ref, ng_ref, o_ref,
                 m_ref, l_ref, acc_ref):
    tq = o_ref.shape[0]
    tk = tq
    i = pl.program_id(2)
    lam = sc_ref[0]
    out_scale = sc_ref[1]

    m_ref[...] = jnp.full(m_ref.shape, NEG_BIG, F32)
    l_ref[...] = jnp.zeros(l_ref.shape, F32)
    acc_ref[...] = jnp.zeros(acc_ref.shape, F32)
    qts = (q0_ref[0], q1_ref[0])

    maps = range(2)

    def block(start, n_keys, bias):
        start = pl.multiple_of(start, tk)
        kb = k_ref[pl.ds(start, n_keys), :]
        vtb = vt_ref[:, pl.ds(start, n_keys)]
        s = [_dot(kb, qts[mp]) for mp in maps]
        if bias is not None:
            s = [x + bias for x in s]
        m_old = [m_ref[mp] for mp in maps]
        m_new = [jnp.maximum(m_old[mp], jnp.max(s[mp], 0, keepdims=True)) for mp in maps]
        alpha = [jnp.exp2(m_old[mp] - m_new[mp]) for mp in maps]
        p = [jnp.exp2(s[mp] - m_new[mp]) for mp in maps]
        pv = [_dot(vtb, p[mp].astype(BF16)) for mp in maps]
        for mp in maps:
            l_ref[mp] = alpha[mp] * l_ref[mp] + jnp.sum(p[mp], 0, keepdims=True)
            acc_ref[mp] = alpha[mp] * acc_ref[mp] + pv[mp]
            m_ref[mp] = m_new[mp]

    n_far = jnp.maximum(i - 1, 0)

    def far_pair(jj, carry):
        block(2 * jj * tk, 2 * tk, None)
        return carry

    lax.fori_loop(0, n_far // 2, far_pair, 0)

    @pl.when(n_far % 2 == 1)
    def _():
        block((n_far - 1) * tk, tk, None)

    @pl.when(i >= 1)
    def _():
        block((i - 1) * tk, tk, bias_ref[0, 1])

    block(i * tk, tk, bias_ref[0, 0])

    o_t = acc_ref[0] / l_ref[0] - lam * (acc_ref[1] / l_ref[1])
    o = o_t.T
    o = o * lax.rsqrt(jnp.mean(o * o, -1, keepdims=True) + EPS) * ng_ref[...] * out_scale
    o_ref[...] = o.astype(BF16)


def _attn(scalars, qt, k, vt, bias_tab, norm_g_row, batch, seq):
    tq = ATT_BLOCK
    nq = seq // tq
    t = batch * seq
    return pl.pallas_call(
        _attn_kernel,
        grid=(batch, B_HEADS, nq),
        in_specs=[
            pl.BlockSpec(memory_space=pltpu.SMEM),
            pl.BlockSpec((1, LANES, tq), lambda b, h, i: (0, h, b * nq + i)),
            pl.BlockSpec((1, LANES, tq), lambda b, h, i: (1, h, b * nq + i)),
            pl.BlockSpec((seq, LANES), lambda b, h, i: (b, h)),
            pl.BlockSpec((LANES, seq), lambda b, h, i: (h, b)),
            pl.BlockSpec((1, 2, tq, tq), lambda b, h, i: (h, 0, 0, 0)),
            pl.BlockSpec(norm_g_row.shape, lambda b, h, i: (0, 0)),
        ],
        out_specs=pl.BlockSpec((tq, LANES), lambda b, h, i: (b * nq + i, h)),
        out_shape=jax.ShapeDtypeStruct((t, D_MODEL), BF16),
        scratch_shapes=[pltpu.VMEM((2, 1, tq), F32), pltpu.VMEM((2, 1, tq), F32),
                        pltpu.VMEM((2, LANES, tq), F32)],
        compiler_params=_cparams(("parallel", "parallel", "parallel")),
        name="attn",
    )(scalars, qt, qt, k, vt, bias_tab, norm_g_row)


def _rel_bucket(rel):
    n = jnp.maximum(rel, 0)
    max_exact = REL_BUCKETS // 2
    large = max_exact + (jnp.log(jnp.maximum(n, 1).astype(F32) / max_exact)
                         / math.log(REL_MAX_DIST / max_exact) * (REL_BUCKETS - max_exact)).astype(I32)
    large = jnp.minimum(large, REL_BUCKETS - 1)
    return jnp.where(n < max_exact, n, large)


def _attn_bias_table(rel_bias):
    tq = ATT_BLOCK
    assert tq + 1 >= REL_MAX_DIST, "blocks two back must fall in the last bucket"
    r = jnp.arange(tq)[None, :] - jnp.arange(tq)[:, None]
    table = (rel_bias.astype(F32) - rel_bias.astype(F32)[REL_BUCKETS - 1]) * LOG2E
    diag = jnp.where((r >= 0)[..., None], table[_rel_bucket(r)], NEG_BIG)
    near = table[_rel_bucket(r + tq)]
    return jnp.transpose(jnp.stack([diag, near], 0), (3, 0, 1, 2))


def _gelu(x):
    return 0.5 * x * (1.0 + lax.erf(x * (2.0 ** -0.5)))


def _gm_kernel(x_ref, w_ref, lng_ref, lnb_ref, ws_ref, bs_ref, o_ref):
    tm = x_ref.shape[0]
    xb = x_ref[...].astype(BF16)
    v = _gelu(_dot(xb, w_ref[:, C_HALF:2 * C_HALF]))
    vn = _layer_norm(v, lng_ref[...], lnb_ref[...]).astype(BF16)
    bs = bs_ref[...]
    for g in range(C_GROUPS):
        lo = g * C_GW
        u = _gelu(_dot(xb, w_ref[:, lo:lo + C_GW]))
        wg = ws_ref[g]
        for ch in range(tm // C_CHUNK):
            r0 = ch * C_CHUNK
            mixed = _dot(wg, vn[r0:r0 + C_CHUNK, lo:lo + C_GW]) + bs[:, g:g + 1]
            o_ref[r0:r0 + C_CHUNK, lo:lo + C_GW] = (u[r0:r0 + C_CHUNK] * mixed).astype(BF16)


def _gm(x2d, w_in, ln_g_row, ln_b_row, ws_causal, bs_t):
    t = x2d.shape[0]
    tm = ROW_TILE
    rows = lambda i: (i, 0)
    const2 = lambda i: (0, 0)
    return pl.pallas_call(
        _gm_kernel,
        grid=(t // tm,),
        in_specs=[
            pl.BlockSpec((tm, D_MODEL), rows),
            pl.BlockSpec(w_in.shape, const2),
            pl.BlockSpec(ln_g_row.shape, const2),
            pl.BlockSpec(ln_b_row.shape, const2),
            pl.BlockSpec(ws_causal.shape, lambda i: (0, 0, 0)),
            pl.BlockSpec(bs_t.shape, const2),
        ],
        out_specs=pl.BlockSpec((tm, C_HALF), rows),
        out_shape=jax.ShapeDtypeStruct((t, C_HALF), BF16),
        compiler_params=_cparams(("parallel",)),
        name="gmlp",
    )(x2d, w_in, ln_g_row, ln_b_row, ws_causal, bs_t)


def _group_member(x, lane, k):
    fwd = pltpu.roll(x, LANES - k, 1)
    back = pltpu.roll(x, EXPERTS_PER_GROUP - k, 1)
    return jnp.where((lane % EXPERTS_PER_GROUP) + k < EXPERTS_PER_GROUP, fwd, back)


def _out_kernel(y_ref, w_ref, h_ref, lng_ref, lnb_ref, wr_ref, rb_ref, hx_ref, cnt_ref, carry_ref):
    tm = y_ref.shape[0]

    @pl.when(pl.program_id(0) == 0)
    def _():
        carry_ref[...] = jnp.zeros(carry_ref.shape, F32)

    hh = DN_ALPHA * h_ref[...] + _dot(y_ref[...], w_ref[...])
    h1 = _layer_norm(hh, lng_ref[...], lnb_ref[...])
    hx_ref[:, 0:D_MODEL] = h1

    h_hi = h1.astype(BF16)
    h_lo = (h1 - h_hi.astype(F32)).astype(BF16)
    both = _dot(h_hi, wr_ref[...])
    logits = both[:, 0:LANES] + both[:, LANES:2 * LANES] + _dot(h_lo, wr_ref[:, 0:LANES])

    lane = lax.broadcasted_iota(I32, (tm, LANES), 1)
    lane_f = lane.astype(F32)
    real = lane < N_EXPERTS
    scores = jax.nn.sigmoid(logits)
    sel = jnp.where(real, scores + rb_ref[...], NEG_BIG)

    member = lane % EXPERTS_PER_GROUP
    others = [_group_member(sel, lane, k) for k in (1, 2, 3)]
    grp = jnp.maximum(jnp.maximum(sel + others[0], sel + others[1]),
                      jnp.maximum(sel + others[2], others[0] + others[1]))
    grp = jnp.maximum(grp, jnp.maximum(others[0] + others[2], others[1] + others[2]))
    grp = jnp.where(real, grp, NEG_BIG)
    gmax = jnp.max(grp, -1, keepdims=True)
    group_id = (lane // EXPERTS_PER_GROUP).astype(F32)
    best = jnp.min(jnp.where(grp == gmax, group_id, 1e9), -1, keepdims=True)
    beaten = jnp.zeros((tm, LANES), F32)
    for k, o in zip((1, 2, 3), others):
        o_member = (member + k) % EXPERTS_PER_GROUP
        ahead = (o > sel) | ((o == sel) & (o_member < member))
        beaten = beaten + ahead.astype(F32)
    chosen = real & (group_id == best) & (beaten < 2.0)

    e_lo = jnp.min(jnp.where(chosen, lane_f, 1e9), -1, keepdims=True)
    e_hi = jnp.max(jnp.where(chosen, lane_f, -1.0), -1, keepdims=True)
    s_lo = jnp.sum(jnp.where(lane_f == e_lo, scores, 0.0), -1, keepdims=True)
    s_hi = jnp.sum(jnp.where(lane_f == e_hi, scores, 0.0), -1, keepdims=True)
    denom = s_lo + s_hi
    a = e_lo - best * EXPERTS_PER_GROUP
    b = e_hi - best * EXPERTS_PER_GROUP
    cls = best * 6.0 + (a * 3.0 - a * (a - 1.0) * 0.5) + (b - a - 1.0)

    onehot = lane_f == cls
    r_i = lax.broadcasted_iota(I32, (tm, tm), 0)
    c_i = lax.broadcasted_iota(I32, (tm, tm), 1)
    before = _dot((r_i > c_i).astype(BF16), onehot.astype(BF16)) + carry_ref[0:1, :]
    rank = jnp.sum(jnp.where(onehot, before, 0.0), -1, keepdims=True)
    carry_ref[0:1, :] = carry_ref[0:1, :] + jnp.sum(onehot.astype(F32), 0, keepdims=True)
    cnt_ref[...] = carry_ref[...]

    meta = jnp.where(lane == 0, s_lo / denom, 0.0)
    meta = jnp.where(lane == 1, s_hi / denom, meta)
    meta = jnp.where(lane == 2, cls, meta)
    meta = jnp.where(lane == 3, rank, meta)
    hx_ref[:, D_MODEL:D_MODEL + META_W] = meta


def _out_proj(y, w_out, h2d, ln_g_row, ln_b_row, wr_split, rbias_row):
    t, kd = y.shape
    tm = ROW_TILE
    rows = lambda i: (i, 0)
    const = lambda i: (0, 0)
    return pl.pallas_call(
        _out_kernel,
        grid=(t // tm,),
        in_specs=[
            pl.BlockSpec((tm, kd), rows),
            pl.BlockSpec(w_out.shape, const),
            pl.BlockSpec((tm, D_MODEL), rows),
            pl.BlockSpec(ln_g_row.shape, const),
            pl.BlockSpec(ln_b_row.shape, const),
            pl.BlockSpec(wr_split.shape, const),
            pl.BlockSpec(rbias_row.shape, const),
        ],
        out_specs=[pl.BlockSpec((tm, D_MODEL + META_W), rows), pl.BlockSpec((SUBLANES, LANES), const)],
        out_shape=[jax.ShapeDtypeStruct((t, D_MODEL + META_W), F32),
                   jax.ShapeDtypeStruct((SUBLANES, LANES), F32)],
        scratch_shapes=[pltpu.VMEM((SUBLANES, LANES), F32)],
        compiler_params=_cparams(("arbitrary",)),
        name="out_proj",
    )(y, w_out, h2d, ln_g_row, ln_b_row, wr_split, rbias_row)


def _row_copies_wait(src_ref, dst_ref, sem):
    pltpu.make_async_copy(src_ref, dst_ref, sem).wait()


def _dispatch_kernel(pos_ref, hx_ref, xs_in_ref, xs_ref, sem):
    del xs_in_ref
    tm = hx_ref.shape[0]

    def issue(r, carry):
        p = pos_ref[0, 0, r]
        pltpu.make_async_copy(hx_ref.at[pl.ds(r, 1), :], xs_ref.at[pl.ds(p, 1), :], sem).start()
        return carry

    lax.fori_loop(0, tm, issue, 0)
    _row_copies_wait(hx_ref, xs_ref.at[pl.ds(0, tm), :], sem)


def _dispatch(pos, hx, n_slots):
    t, w = hx.shape
    tm = ROW_TILE
    xs_init = jnp.zeros((n_slots, w), F32)
    return pl.pallas_call(
        _dispatch_kernel,
        grid=(t // tm,),
        in_specs=[
            pl.BlockSpec((1, 1, tm), lambda i: (i, 0, 0), memory_space=pltpu.SMEM),
            pl.BlockSpec((tm, w), lambda i: (i, 0)),
            pl.BlockSpec(memory_space=pl.ANY),
        ],
        out_specs=pl.BlockSpec(memory_space=pl.ANY),
        out_shape=jax.ShapeDtypeStruct((n_slots, w), F32),
        scratch_shapes=[pltpu.SemaphoreType.DMA(())],
        input_output_aliases={2: 0},
        compiler_params=_cparams(("arbitrary",)),
        name="dispatch",
    )(pos, hx, xs_init)


def _moe_kernel(lo_ref, hi_ref, nv_ref, xs_ref, w1l_ref, w3l_ref, w2l_ref, w1h_ref, w3h_ref, w2h_ref,
                lng_ref, lnb_ref, ys_ref):
    del lo_ref, hi_ref

    @pl.when(pl.program_id(0) < nv_ref[0])
    def _():
        x = xs_ref[:, 0:D_MODEL]
        xb = x.astype(BF16)

        def expert(w1_ref, w3_ref, w2_ref):
            hid = _silu(_dot(xb, w1_ref[0])) * _dot(xb, w3_ref[0])
            return _dot(hid.astype(BF16), w2_ref[0])

        m = xs_ref[:, D_MODEL:D_MODEL + 1] * expert(w1l_ref, w3l_ref, w2l_ref)
        m = m + xs_ref[:, D_MODEL + 1:D_MODEL + 2] * expert(w1h_ref, w3h_ref, w2h_ref)
        ys_ref[...] = _layer_norm(DN_ALPHA * x + m, lng_ref[...], lnb_ref[...])

    @pl.when(pl.program_id(0) >= nv_ref[0])
    def _():
        ys_ref[...] = jnp.zeros(ys_ref.shape, F32)


def _moe(tile_lo, tile_hi, n_valid, xs, w1, w3, w2, ln_g_row, ln_b_row):
    n_slots, w = xs.shape
    tm = MOE_TILE
    rows = lambda i, lo, hi, nv: (jnp.minimum(i, nv[0] - 1), 0)
    w_lo = lambda i, lo, hi, nv: (lo[i], 0, 0)
    w_hi = lambda i, lo, hi, nv: (hi[i], 0, 0)
    const = lambda i, lo, hi, nv: (0, 0)
    up = (1, D_MODEL, D_EXPERT)
    down = (1, D_EXPERT, D_MODEL)
    grid_spec = pltpu.PrefetchScalarGridSpec(
        num_scalar_prefetch=3,
        grid=(n_slots // tm,),
        in_specs=[
            pl.BlockSpec((tm, w), rows),
            pl.BlockSpec(up, w_lo), pl.BlockSpec(up, w_lo), pl.BlockSpec(down, w_lo),
            pl.BlockSpec(up, w_hi), pl.BlockSpec(up, w_hi), pl.BlockSpec(down, w_hi),
            pl.BlockSpec(ln_g_row.shape, const), pl.BlockSpec(ln_b_row.shape, const),
        ],
        out_specs=pl.BlockSpec((tm, D_MODEL), lambda i, lo, hi, nv: (i, 0)),
    )
    return pl.pallas_call(
        _moe_kernel,
        grid_spec=grid_spec,
        out_shape=jax.ShapeDtypeStruct((n_slots, D_MODEL), F32),
        compiler_params=_cparams(("arbitrary",)),
        name="moe",
    )(tile_lo, tile_hi, n_valid, xs, w1, w3, w2, w1, w3, w2, ln_g_row, ln_b_row)


def _unpermute_kernel(pos_ref, ys_ref, o_ref, sem):
    tm = o_ref.shape[0]

    def issue(r, carry):
        p = pos_ref[0, 0, r]
        pltpu.make_async_copy(ys_ref.at[pl.ds(p, 1), :], o_ref.at[pl.ds(r, 1), :], sem).start()
        return carry

    lax.fori_loop(0, tm, issue, 0)
    _row_copies_wait(ys_ref.at[pl.ds(0, tm), :], o_ref, sem)


def _unpermute(pos, ys, t):
    tm = ROW_TILE
    return pl.pallas_call(
        _unpermute_kernel,
        grid=(t // tm,),
        in_specs=[
            pl.BlockSpec((1, 1, tm), lambda i: (i, 0, 0), memory_space=pltpu.SMEM),
            pl.BlockSpec(memory_space=pl.ANY),
        ],
        out_specs=pl.BlockSpec((tm, D_MODEL), lambda i: (i, 0)),
        out_shape=jax.ShapeDtypeStruct((t, D_MODEL), F32),
        scratch_shapes=[pltpu.SemaphoreType.DMA(())],
        compiler_params=_cparams(("arbitrary",)),
        name="unpermute",
    )(pos, ys)


def _routing_plan(hx, counts, n_slots):
    cls = hx[:, D_MODEL + 2].astype(I32)
    rank = hx[:, D_MODEL + 3].astype(I32)
    cnt = counts[0, :N_PAIR_CLASSES].astype(I32)
    padded = ((cnt + MOE_TILE - 1) // MOE_TILE) * MOE_TILE
    end = jnp.cumsum(padded)
    pos = (end - padded)[cls] + rank
    n_tiles = n_slots // MOE_TILE
    tile_cls = jnp.searchsorted(end, jnp.arange(n_tiles, dtype=I32) * MOE_TILE, side="right")
    tile_cls = jnp.minimum(tile_cls, N_PAIR_CLASSES - 1).astype(I32)
    n_valid = (end[-1] // MOE_TILE).astype(I32).reshape(1)
    last_cls = tile_cls[jnp.maximum(n_valid[0] - 1, 0)]
    tile_cls = jnp.where(jnp.arange(n_tiles) < n_valid[0], tile_cls, last_cls)
    return pos, jnp.asarray(_PAIR_LO)[tile_cls], jnp.asarray(_PAIR_HI)[tile_cls], n_valid


def _moe_layer(hx, counts, w1, w3, w2, ln_g_row, ln_b_row):
    t = hx.shape[0]
    n_slots = t + N_PAIR_CLASSES * MOE_TILE
    pos, tile_lo, tile_hi, n_valid = _routing_plan(hx, counts, n_slots)
    pos = pos.reshape(t // ROW_TILE, 1, ROW_TILE)
    xs = _dispatch(pos, hx, n_slots)
    ys = _moe(tile_lo, tile_hi, n_valid, xs, w1, w3, w2, ln_g_row, ln_b_row)
    return _unpermute(pos, ys, t)


def _row(v, width=None):
    v = v.astype(F32).reshape(1, -1)
    if width is not None and v.shape[1] < width:
        v = jnp.pad(v, ((0, 0), (0, width - v.shape[1])))
    return v


def _deltanet(h2d, batch, seq, w_in, conv_w, a_log, dt_bias, norm_g):
    qkvg = 4 * D_MODEL
    w_big = w_in[:, :qkvg].astype(BF16)
    w_ab = jnp.pad(w_in[:, qkvg:], ((0, 0), (0, LANES - 2 * A_HEADS))).astype(BF16)
    q, k, v, gate, gb = _dn_in(h2d, batch, seq, w_big, w_ab, conv_w.astype(F32),
                               _row(a_log, LANES), _row(dt_bias, LANES))
    return _dn_scan(q, k, v, gate, gb, _row(norm_g), batch, seq)


def _diff_attention(h2d, batch, seq, w_in, lam, norm_g, rel_bias, lambda_init):
    lamf = lam.astype(F32)
    lam_full = jnp.exp(jnp.sum(lamf[0] * lamf[1])) - jnp.exp(jnp.sum(lamf[2] * lamf[3])) + lambda_init
    scalars = jnp.stack([lam_full, jnp.asarray(1.0 - lambda_init, F32)]).astype(F32)
    qt, k, vt = _at_in(h2d, w_in.astype(BF16))
    return _attn(scalars, qt, k, vt, _attn_bias_table(rel_bias), _row(norm_g), batch, seq)


def _gmlp(h2d, w_in, ln_g, ln_b, w_s, b_s):
    ws_causal = jnp.tril(w_s.astype(F32)).astype(BF16)
    bs_t = jnp.pad(jnp.transpose(b_s.astype(F32)), ((0, 0), (0, LANES - C_GROUPS)))
    return _gm(h2d, w_in.astype(BF16), _row(ln_g), _row(ln_b), ws_causal, bs_t)


def kernel(x, ln_mix_g, ln_mix_b, ln_ffn_g, ln_ffn_b, a_w_in, a_conv, a_a_log, a_dt_bias, a_norm_g, a_w_out, b_w_in, b_lambda, b_norm_g, b_w_out, rel_bias, c_w_in, c_ln_g, c_ln_b, c_w_s, c_b_s, c_w_out, w_router, router_bias, e_w1, e_w3, e_w2):
    batch, seq, d = x.shape
    assert d == D_MODEL and seq % max(ROW_TILE, ATT_BLOCK, DN_CHUNK) == 0
    t = batch * seq
    h = x.reshape(t, d).astype(F32)

    wr = jnp.pad(w_router.astype(F32), ((0, 0), (0, LANES - N_EXPERTS)))
    wr_hi = wr.astype(BF16)
    wr_lo = (wr - wr_hi.astype(F32)).astype(BF16)
    wr_split = jnp.concatenate([wr_hi, wr_lo], axis=1)
    rbias_row = _row(router_bias, LANES)

    for i in range(DEPTH):
        kind, j = i % N_MIXERS, i // N_MIXERS
        if kind == 0:
            y = _deltanet(h, batch, seq, a_w_in[j], a_conv[j], a_a_log[j], a_dt_bias[j], a_norm_g[j])
            w_out = a_w_out[j]
        elif kind == 1:
            lambda_init = 0.8 - 0.6 * math.exp(-0.3 * i)
            y = _diff_attention(h, batch, seq, b_w_in[j], b_lambda[j], b_norm_g[j], rel_bias, lambda_init)
            w_out = b_w_out[j]
        else:
            y = _gmlp(h, c_w_in[j], c_ln_g[j], c_ln_b[j], c_w_s[j], c_b_s[j])
            w_out = c_w_out[j]
        hx, counts = _out_proj(y, w_out.astype(BF16), h, _row(ln_mix_g[i]), _row(ln_mix_b[i]),
                               wr_split, rbias_row)
        h = _moe_layer(hx, counts, e_w1[i].astype(BF16), e_w3[i].astype(BF16), e_w2[i].astype(BF16),
                       _row(ln_ffn_g[i]), _row(ln_ffn_b[i]))
    return h.reshape(batch, seq, d).astype(x.dtype)
```

```python
import functools
import math

import jax
import jax.numpy as jnp
import numpy as np
from jax import lax
from jax.experimental import pallas as pl
from jax.experimental.pallas import tpu as pltpu

F32 = jnp.float32
BF16 = jnp.bfloat16
I32 = jnp.int32

D_MODEL = 1024
DEPTH = 4
N_MIXERS = 3
A_HEADS = 8
A_DK = 128
A_CONV = 4
B_DH = 64
B_HEADS = D_MODEL // (2 * B_DH)
REL_BUCKETS = 32
REL_MAX_DIST = 128
C_CHUNK = 128
C_HALF = 2 * D_MODEL
C_GROUPS = 8
C_GW = C_HALF // C_GROUPS
N_EXPERTS = 16
N_GROUPS = 4
EXPERTS_PER_GROUP = 4
D_EXPERT = 512
DN_ALPHA = (2 * DEPTH) ** 0.25
EPS = 1e-5

LANES = 128
SUBLANES = 8
VMEM_LIMIT_BYTES = 56 * 1024 * 1024

ROW_TILE = 256
DN_CHUNK = 128
ATT_BLOCK = 512
MOE_TILE = 256
PERMUTE_TOKENS = 1024
N_PAIR_CLASSES = N_GROUPS * 6
NEG_BIG = -1e30
LOG2E = math.log2(math.e)

_PAIR_LO = np.array([g * 4 + a for g in range(4) for a, b in ((0, 1), (0, 2), (0, 3), (1, 2), (1, 3), (2, 3))], np.int32)
_PAIR_HI = np.array([g * 4 + b for g in range(4) for a, b in ((0, 1), (0, 2), (0, 3), (1, 2), (1, 3), (2, 3))], np.int32)


def _cparams(sem):
    return pltpu.CompilerParams(dimension_semantics=sem, vmem_limit_bytes=VMEM_LIMIT_BYTES)


def _layer_norm(x, g, b):
    mu = jnp.mean(x, -1, keepdims=True)
    xc = x - mu
    var = jnp.mean(xc * xc, -1, keepdims=True)
    return xc * lax.rsqrt(var + EPS) * g + b


def _silu(x):
    return x * jax.nn.sigmoid(x)


def _dot(a, b):
    return jnp.dot(a, b, preferred_element_type=F32)


def _dot_nt(a, b):
    return lax.dot_general(a, b, (((1,), (1,)), ((), ())), preferred_element_type=F32)


SLAB = D_MODEL // LANES


def _load_tokens(ref, n_tok):
    return jnp.concatenate([ref[pl.ds(s, n_tok, stride=SLAB), :] for s in range(SLAB)], axis=1)


def _store_tokens(ref, val, n_tok):
    for s in range(SLAB):
        ref[pl.ds(s, n_tok, stride=SLAB), :] = val[:, s * LANES:(s + 1) * LANES]


def _split3(x):
    hi = x.astype(BF16)
    r = x - hi.astype(F32)
    mid = r.astype(BF16)
    lo = (r - mid.astype(F32)).astype(BF16)
    return hi, mid, lo


def _dn_in_kernel(x_ref, w_ref, wab_ref, conv_ref, alog_ref, dtb_ref,
                  q_ref, k_ref, v_ref, gate_ref, gb_ref, ext_ref):
    ts = x_ref.shape[0] // SLAB
    qkv_w = 3 * D_MODEL
    n_slabs = qkv_w // LANES
    s_idx = pl.program_id(1)

    @pl.when(s_idx == 0)
    def _():
        ext_ref[:, 0:SUBLANES, :] = jnp.zeros((n_slabs, SUBLANES, LANES), F32)

    @pl.when(s_idx > 0)
    def _():
        ext_ref[:, 0:SUBLANES, :] = ext_ref[:, ts:ts + SUBLANES, :]

    xb = _load_tokens(x_ref, ts).astype(BF16)
    gate_ref[...] = _dot(xb, w_ref[:, qkv_w:qkv_w + D_MODEL]).astype(BF16)
    cw = conv_ref[...]
    for part, out_ref in enumerate((q_ref, k_ref, v_ref)):
        pre = _dot(xb, w_ref[:, part * D_MODEL:(part + 1) * D_MODEL])
        for h in range(A_HEADS):
            slab = part * A_HEADS + h
            lo = h * A_DK
            ext_ref[slab, SUBLANES:SUBLANES + ts, :] = pre[:, lo:lo + A_DK]
            col = slice(slab * LANES, (slab + 1) * LANES)
            y = ext_ref[slab, pl.ds(SUBLANES - 3, ts), :] * cw[0:1, col]
            for j in range(1, A_CONV):
                y = y + ext_ref[slab, pl.ds(SUBLANES - 3 + j, ts), :] * cw[j:j + 1, col]
            y = _silu(y)
            if part == 0:
                y = y * (lax.rsqrt(jnp.sum(y * y, -1, keepdims=True) + 1e-6) * (A_DK ** -0.5))
            elif part == 1:
                y = y * lax.rsqrt(jnp.sum(y * y, -1, keepdims=True) + 1e-6)
            out_ref[:, lo:lo + A_DK] = y.astype(BF16)

    ab = _dot(xb, wab_ref[...])
    z = ab + dtb_ref[...]
    softplus = jnp.maximum(z, 0.0) + jnp.log1p(jnp.exp(-jnp.abs(z)))
    g = -jnp.exp(alog_ref[...]) * softplus
    lane = lax.broadcasted_iota(I32, ab.shape, 1)
    gb_ref[...] = jnp.where(lane < A_HEADS, g, jax.nn.sigmoid(ab))


def _dn_in(x2d, batch, seq, w_big, w_ab, conv_w, alog_row, dtb_row):
    ts = ROW_TILE
    ns = seq // ts
    rows = lambda b, s: (b * ns + s, 0)
    const = lambda b, s: (0, 0)
    t = batch * seq
    out_bf = jax.ShapeDtypeStruct((t, D_MODEL), BF16)
    return pl.pallas_call(
        _dn_in_kernel,
        grid=(batch, ns),
        in_specs=[
            pl.BlockSpec((ts * SLAB, LANES), rows),
            pl.BlockSpec(w_big.shape, const),
            pl.BlockSpec(w_ab.shape, const),
            pl.BlockSpec(conv_w.shape, const),
            pl.BlockSpec(alog_row.shape, const),
            pl.BlockSpec(dtb_row.shape, const),
        ],
        out_specs=[pl.BlockSpec((ts, D_MODEL), rows)] * 4 + [pl.BlockSpec((ts, LANES), rows)],
        out_shape=[out_bf, out_bf, out_bf, out_bf, jax.ShapeDtypeStruct((t, LANES), F32)],
        scratch_shapes=[pltpu.VMEM((3 * D_MODEL // LANES, ts + 2 * SUBLANES, LANES), F32)],
        compiler_params=_cparams(("arbitrary", "arbitrary")),
        name="dn_in",
    )(x2d, w_big, w_ab, conv_w, alog_row, dtb_row)


def _dn_scan_kernel(q_ref, k_ref, v_ref, gate_ref, gb_ref, ng_ref, o_ref, state_ref):
    c = q_ref.shape[0]
    n_doublings = int(math.log2(c)) - 1

    @pl.when(pl.program_id(1) == 0)
    def _():
        state_ref[...] = jnp.zeros(state_ref.shape, F32)

    row = lax.broadcasted_iota(I32, (c, c), 0)
    col = lax.broadcasted_iota(I32, (c, c), 1)
    causal = row >= col
    strict = row > col
    eye = (row == col).astype(F32)

    gb = gb_ref[...]
    tril = causal.astype(BF16)
    parts = _split3(gb)
    gc_all = _dot(tril, parts[0]) + _dot(tril, parts[1]) + _dot(tril, parts[2])
    gc_all_t = gc_all.T
    ng = ng_ref[...]

    heads = range(A_HEADS)
    lanes = [slice(h * A_DK, (h + 1) * A_DK) for h in heads]
    qh = [q_ref[:, lanes[h]] for h in heads]
    kh = [k_ref[:, lanes[h]] for h in heads]
    gc_col = [gc_all[:, h:h + 1] for h in heads]
    beta = [gb[:, A_HEADS + h:A_HEADS + h + 1] for h in heads]
    g_last = [gc_all[c - 1:c, h:h + 1] for h in heads]

    kq = [_dot_nt(jnp.concatenate([kh[h], qh[h]], axis=0), kh[h]) for h in heads]
    decay = [jnp.where(causal, jnp.exp(jnp.where(causal, gc_col[h] - gc_all_t[h:h + 1, :], 0.0)), 0.0)
             for h in heads]
    attn = [(kq[h][c:2 * c] * decay[h]).astype(BF16) for h in heads]

    p = [jnp.where(strict, -(kq[h][0:c] * beta[h] * decay[h]), 0.0) for h in heads]
    t_inv = [eye + p[h] for h in heads]
    for _ in range(n_doublings):
        pb = [p[h].astype(BF16) for h in heads]
        p = [_dot(pb[h], pb[h]) for h in heads]
        t_inv = [t_inv[h] + _dot(t_inv[h].astype(BF16), p[h].astype(BF16)) for h in heads]

    e_gc = [jnp.exp(gc_col[h]) for h in heads]
    khf = [kh[h].astype(F32) for h in heads]
    rhs = [jnp.concatenate([v_ref[:, lanes[h]].astype(F32) * beta[h], khf[h] * (beta[h] * e_gc[h])],
                           axis=1).astype(BF16) for h in heads]
    sol = [_dot(t_inv[h].astype(BF16), rhs[h]) for h in heads]

    state = [state_ref[h] for h in heads]
    lhs = [jnp.concatenate([sol[h][:, A_DK:2 * A_DK], qh[h].astype(F32) * e_gc[h]], axis=0).astype(BF16)
           for h in heads]
    ws = [_dot(lhs[h], state[h].astype(BF16)) for h in heads]
    v_new = [(sol[h][:, 0:A_DK] - ws[h][0:c]).astype(BF16) for h in heads]
    k_dec_t = [(khf[h] * jnp.exp(g_last[h] - gc_col[h])).T.astype(BF16) for h in heads]
    o = [ws[h][c:2 * c] + _dot(attn[h], v_new[h]) for h in heads]
    for h in heads:
        state_ref[h] = state[h] * jnp.exp(g_last[h]) + _dot(k_dec_t[h], v_new[h])
    for h in heads:
        on = o[h] * lax.rsqrt(jnp.mean(o[h] * o[h], -1, keepdims=True) + EPS) * ng
        o_ref[:, lanes[h]] = (on * _silu(gate_ref[:, lanes[h]].astype(F32))).astype(BF16)


def _dn_scan(q, k, v, gate, gb, norm_g_row, batch, seq):
    c = DN_CHUNK
    nc = seq // c
    rows = lambda b, n: (b * nc + n, 0)
    const = lambda b, n: (0, 0)
    t = batch * seq
    return pl.pallas_call(
        _dn_scan_kernel,
        grid=(batch, nc),
        in_specs=[pl.BlockSpec((c, D_MODEL), rows)] * 4
        + [pl.BlockSpec((c, LANES), rows), pl.BlockSpec(norm_g_row.shape, const)],
        out_specs=pl.BlockSpec((c, D_MODEL), rows),
        out_shape=jax.ShapeDtypeStruct((t, D_MODEL), BF16),
        scratch_shapes=[pltpu.VMEM((A_HEADS, A_DK, A_DK), F32)],
        compiler_params=_cparams(("arbitrary", "arbitrary")),
        name="dn_scan",
    )(q, k, v, gate, gb, norm_g_row)


def _at_in_kernel(x_ref, w_ref, qt_ref, k_ref, vt_ref):
    xb = _load_tokens(x_ref, k_ref.shape[0]).astype(BF16)
    q = _dot(xb, w_ref[:, 0:D_MODEL]) * (B_DH ** -0.5 * LOG2E)
    lane = lax.broadcasted_iota(I32, q.shape, 1)
    first_map = (lane % (2 * B_DH)) < B_DH
    qt_ref[0] = jnp.where(first_map, q, 0.0).T.astype(BF16)
    qt_ref[1] = jnp.where(first_map, 0.0, q).T.astype(BF16)
    k_ref[...] = _dot(xb, w_ref[:, D_MODEL:2 * D_MODEL]).astype(BF16)
    vt_ref[...] = _dot(xb, w_ref[:, 2 * D_MODEL:3 * D_MODEL]).T.astype(BF16)


def _at_in(x2d, w_in):
    t = x2d.shape[0] // SLAB
    tm = ROW_TILE
    return pl.pallas_call(
        _at_in_kernel,
        grid=(t // tm,),
        in_specs=[pl.BlockSpec((tm * SLAB, LANES), lambda i: (i, 0)), pl.BlockSpec(w_in.shape, lambda i: (0, 0))],
        out_specs=[pl.BlockSpec((2, D_MODEL, tm), lambda i: (0, 0, i)),
                   pl.BlockSpec((tm, D_MODEL), lambda i: (i, 0)),
                   pl.BlockSpec((D_MODEL, tm), lambda i: (0, i))],
        out_shape=[jax.ShapeDtypeStruct((2, D_MODEL, t), BF16),
                   jax.ShapeDtypeStruct((t, D_MODEL), BF16),
                   jax.ShapeDtypeStruct((D_MODEL, t), BF16)],
        compiler_params=_cparams(("parallel",)),
        name="at_in",
    )(x2d, w_in)


def _attn_kernel(sc_ref, q0_ref, q1_ref, k_ref, vt_ref, bias_ref, ng_ref, o_ref,
                 m_ref, l_ref, acc_ref):
    tq = o_ref.shape[0]
    tk = tq
    i = pl.program_id(2)
    lam = sc_ref[0]
    out_scale = sc_ref[1]

    m_ref[...] = jnp.full(m_ref.shape, NEG_BIG, F32)
    l_ref[...] = jnp.zeros(l_ref.shape, F32)
    acc_ref[...] = jnp.zeros(acc_ref.shape, F32)
    qts = (q0_ref[0], q1_ref[0])

    maps = range(2)

    def block(start, n_keys, bias):
        start = pl.multiple_of(start, tk)
        kb = k_ref[pl.ds(start, n_keys), :]
        vtb = vt_ref[:, pl.ds(start, n_keys)]
        s = [_dot(kb, qts[mp]) for mp in maps]
        if bias is not None:
            s = [x + bias for x in s]
        m_old = [m_ref[mp] for mp in maps]
        m_new = [jnp.maximum(m_old[mp], jnp.max(s[mp], 0, keepdims=True)) for mp in maps]
        alpha = [jnp.exp2(m_old[mp] - m_new[mp]) for mp in maps]
        p = [jnp.exp2(s[mp] - m_new[mp]) for mp in maps]
        pv = [_dot(vtb, p[mp].astype(BF16)) for mp in maps]
        for mp in maps:
            l_ref[mp] = alpha[mp] * l_ref[mp] + jnp.sum(p[mp], 0, keepdims=True)
            acc_ref[mp] = alpha[mp] * acc_ref[mp] + pv[mp]
            m_ref[mp] = m_new[mp]

    n_far = jnp.maximum(i - 1, 0)

    def far_pair(jj, carry):
        block(2 * jj * tk, 2 * tk, None)
        return carry

    lax.fori_loop(0, n_far // 2, far_pair, 0)

    @pl.when(n_far % 2 == 1)
    def _():
        block((n_far - 1) * tk, tk, None)

    @pl.when(i >= 1)
    def _():
        block((i - 1) * tk, tk, bias_ref[0, 1])

    block(i * tk, tk, bias_ref[0, 0])

    o_t = acc_ref[0] / l_ref[0] - lam * (acc_ref[1] / l_ref[1])
    o = o_t.T
    o = o * lax.rsqrt(jnp.mean(o * o, -1, keepdims=True) + EPS) * ng_ref[...] * out_scale
    o_ref[...] = o.astype(BF16)


def _attn(scalars, qt, k, vt, bias_tab, norm_g_row, batch, seq):
    tq = ATT_BLOCK
    nq = seq // tq
    t = batch * seq
    return pl.pallas_call(
        _attn_kernel,
        grid=(batch, B_HEADS, nq),
        in_specs=[
            pl.BlockSpec(memory_space=pltpu.SMEM),
            pl.BlockSpec((1, LANES, tq), lambda b, h, i: (0, h, b * nq + i)),
            pl.BlockSpec((1, LANES, tq), lambda b, h, i: (1, h, b * nq + i)),
            pl.BlockSpec((seq, LANES), lambda b, h, i: (b, h)),
            pl.BlockSpec((LANES, seq), lambda b, h, i: (h, b)),
            pl.BlockSpec((1, 2, tq, tq), lambda b, h, i: (h, 0, 0, 0)),
            pl.BlockSpec(norm_g_row.shape, lambda b, h, i: (0, 0)),
        ],
        out_specs=pl.BlockSpec((tq, LANES), lambda b, h, i: (b * nq + i, h)),
        out_shape=jax.ShapeDtypeStruct((t, D_MODEL), BF16),
        scratch_shapes=[pltpu.VMEM((2, 1, tq), F32), pltpu.VMEM((2, 1, tq), F32),
                        pltpu.VMEM((2, LANES, tq), F32)],
        compiler_params=_cparams(("parallel", "parallel", "parallel")),
        name="attn",
    )(scalars, qt, qt, k, vt, bias_tab, norm_g_row)


def _rel_bucket(rel):
    n = jnp.maximum(rel, 0)
    max_exact = REL_BUCKETS // 2
    large = max_exact + (jnp.log(jnp.maximum(n, 1).astype(F32) / max_exact)
                         / math.log(REL_MAX_DIST / max_exact) * (REL_BUCKETS - max_exact)).astype(I32)
    large = jnp.minimum(large, REL_BUCKETS - 1)
    return jnp.where(n < max_exact, n, large)


def _attn_bias_table(rel_bias):
    tq = ATT_BLOCK
    assert tq + 1 >= REL_MAX_DIST, "blocks two back must fall in the last bucket"
    table = (rel_bias.astype(F32) - rel_bias.astype(F32)[REL_BUCKETS - 1]) * LOG2E
    period = 2 * tq
    idx = jnp.arange(period)
    diag_vec = jnp.where((idx < tq)[:, None], table[_rel_bucket(jnp.minimum(idx, tq - 1))], NEG_BIG)
    near_vec = table[_rel_bucket(jnp.where(idx < tq, idx + tq, idx - tq))]
    vecs = jnp.transpose(jnp.stack([diag_vec, near_vec], 0), (2, 0, 1))
    flat = jnp.tile(vecs, (1, 1, tq))[:, :, :tq * (period - 1)]
    return flat.reshape(B_HEADS, 2, tq, period - 1)[:, :, :, :tq]


def _gelu(x):
    return 0.5 * x * (1.0 + lax.erf(x * (2.0 ** -0.5)))


def _gm_kernel(x_ref, w_ref, lng_ref, lnb_ref, ws_ref, bs_ref, o_ref):
    tm = o_ref.shape[0]
    xb = _load_tokens(x_ref, tm).astype(BF16)
    v = _gelu(_dot(xb, w_ref[:, C_HALF:2 * C_HALF]))
    vn = _layer_norm(v, lng_ref[...], lnb_ref[...]).astype(BF16)
    bs = bs_ref[...]
    for g in range(C_GROUPS):
        lo = g * C_GW
        u = _gelu(_dot(xb, w_ref[:, lo:lo + C_GW]))
        wg = ws_ref[g]
        for ch in range(tm // C_CHUNK):
            r0 = ch * C_CHUNK
            mixed = _dot(wg, vn[r0:r0 + C_CHUNK, lo:lo + C_GW]) + bs[:, g:g + 1]
            o_ref[r0:r0 + C_CHUNK, lo:lo + C_GW] = (u[r0:r0 + C_CHUNK] * mixed).astype(BF16)


def _gm(x2d, w_in, ln_g_row, ln_b_row, ws_causal, bs_t):
    t = x2d.shape[0] // SLAB
    tm = ROW_TILE
    rows = lambda i: (i, 0)
    const2 = lambda i: (0, 0)
    return pl.pallas_call(
        _gm_kernel,
        grid=(t // tm,),
        in_specs=[
            pl.BlockSpec((tm * SLAB, LANES), rows),
            pl.BlockSpec(w_in.shape, const2),
            pl.BlockSpec(ln_g_row.shape, const2),
            pl.BlockSpec(ln_b_row.shape, const2),
            pl.BlockSpec(ws_causal.shape, lambda i: (0, 0, 0)),
            pl.BlockSpec(bs_t.shape, const2),
        ],
        out_specs=pl.BlockSpec((tm, C_HALF), rows),
        out_shape=jax.ShapeDtypeStruct((t, C_HALF), BF16),
        compiler_params=_cparams(("parallel",)),
        name="gmlp",
    )(x2d, w_in, ln_g_row, ln_b_row, ws_causal, bs_t)


def _group_member(x, lane, k):
    fwd = pltpu.roll(x, LANES - k, 1)
    back = pltpu.roll(x, EXPERTS_PER_GROUP - k, 1)
    return jnp.where((lane % EXPERTS_PER_GROUP) + k < EXPERTS_PER_GROUP, fwd, back)


def _out_kernel(y_ref, w_ref, h_ref, lng_ref, lnb_ref, wr_ref, rb_ref, hs_ref, meta_ref, cnt_ref,
                carry_ref):
    tm = y_ref.shape[0]

    @pl.when(pl.program_id(0) == 0)
    def _():
        carry_ref[...] = jnp.zeros(carry_ref.shape, F32)

    hh = DN_ALPHA * _load_tokens(h_ref, tm) + _dot(y_ref[...], w_ref[...])
    h1 = _layer_norm(hh, lng_ref[...], lnb_ref[...])
    _store_tokens(hs_ref, h1, tm)

    h_hi = h1.astype(BF16)
    h_lo = (h1 - h_hi.astype(F32)).astype(BF16)
    both = _dot(h_hi, wr_ref[...])
    logits = both[:, 0:LANES] + both[:, LANES:2 * LANES] + _dot(h_lo, wr_ref[:, 0:LANES])

    lane = lax.broadcasted_iota(I32, (tm, LANES), 1)
    lane_f = lane.astype(F32)
    real = lane < N_EXPERTS
    scores = jax.nn.sigmoid(logits)
    sel = jnp.where(real, scores + rb_ref[...], NEG_BIG)

    member = lane % EXPERTS_PER_GROUP
    others = [_group_member(sel, lane, k) for k in (1, 2, 3)]
    grp = jnp.maximum(jnp.maximum(sel + others[0], sel + others[1]),
                      jnp.maximum(sel + others[2], others[0] + others[1]))
    grp = jnp.maximum(grp, jnp.maximum(others[0] + others[2], others[1] + others[2]))
    grp = jnp.where(real, grp, NEG_BIG)
    gmax = jnp.max(grp, -1, keepdims=True)
    group_id = (lane // EXPERTS_PER_GROUP).astype(F32)
    best = jnp.min(jnp.where(grp == gmax, group_id, 1e9), -1, keepdims=True)
    beaten = jnp.zeros((tm, LANES), F32)
    for k, o in zip((1, 2, 3), others):
        o_member = (member + k) % EXPERTS_PER_GROUP
        ahead = (o > sel) | ((o == sel) & (o_member < member))
        beaten = beaten + ahead.astype(F32)
    chosen = real & (group_id == best) & (beaten < 2.0)

    e_lo = jnp.min(jnp.where(chosen, lane_f, 1e9), -1, keepdims=True)
    e_hi = jnp.max(jnp.where(chosen, lane_f, -1.0), -1, keepdims=True)
    a = e_lo - best * EXPERTS_PER_GROUP
    b = e_hi - best * EXPERTS_PER_GROUP
    cls = best * 6.0 + (a * 3.0 - a * (a - 1.0) * 0.5) + (b - a - 1.0)

    onehot = lane_f == cls
    r_i = lax.broadcasted_iota(I32, (tm, tm), 0)
    c_i = lax.broadcasted_iota(I32, (tm, tm), 1)
    before = _dot((r_i > c_i).astype(BF16), onehot.astype(BF16)) + carry_ref[0:1, :]
    rank = jnp.sum(jnp.where(onehot, before, 0.0), -1, keepdims=True)
    carry_ref[0:1, :] = carry_ref[0:1, :] + jnp.sum(onehot.astype(F32), 0, keepdims=True)
    cnt_ref[...] = carry_ref[...]

    meta_ref[...] = jnp.where(lane == 0, cls, jnp.where(lane == 1, rank, 0.0))


def _out_proj(y, w_out, hs, ln_g_row, ln_b_row, wr_split, rbias_row):
    t, kd = y.shape
    tm = ROW_TILE
    rows = lambda i: (i, 0)
    const = lambda i: (0, 0)
    return pl.pallas_call(
        _out_kernel,
        grid=(t // tm,),
        in_specs=[
            pl.BlockSpec((tm, kd), rows),
            pl.BlockSpec(w_out.shape, const),
            pl.BlockSpec((tm * SLAB, LANES), rows),
            pl.BlockSpec(ln_g_row.shape, const),
            pl.BlockSpec(ln_b_row.shape, const),
            pl.BlockSpec(wr_split.shape, const),
            pl.BlockSpec(rbias_row.shape, const),
        ],
        out_specs=[pl.BlockSpec((tm * SLAB, LANES), rows), pl.BlockSpec((tm, LANES), rows),
                   pl.BlockSpec((SUBLANES, LANES), const)],
        out_shape=[jax.ShapeDtypeStruct((t * SLAB, LANES), F32),
                   jax.ShapeDtypeStruct((t, LANES), F32),
                   jax.ShapeDtypeStruct((SUBLANES, LANES), F32)],
        scratch_shapes=[pltpu.VMEM((SUBLANES, LANES), F32)],
        compiler_params=_cparams(("arbitrary",)),
        name="out_proj",
    )(y, w_out, hs, ln_g_row, ln_b_row, wr_split, rbias_row)


def _permute_kernel(pos_ref, src_ref, *refs, scatter):
    dst_ref, sem = refs[-2:]
    n_tok = pos_ref.shape[2]
    base = pl.program_id(0) * n_tok

    def issue(r, carry):
        here = pl.multiple_of((base + r) * SLAB, SLAB)
        there = pl.multiple_of(pos_ref[0, 0, r] * SLAB, SLAB)
        src_row, dst_row = (here, there) if scatter else (there, here)
        pltpu.make_async_copy(src_ref.at[pl.ds(src_row, SLAB), :],
                              dst_ref.at[pl.ds(dst_row, SLAB), :], sem).start()
        return carry

    lax.fori_loop(0, n_tok, issue, 0, unroll=8)
    whole = pl.ds(0, n_tok * SLAB)
    pltpu.make_async_copy(src_ref.at[whole, :], dst_ref.at[whole, :], sem).wait()


def _permute(pos, src, n_dst_rows, scatter):
    any_spec = pl.BlockSpec(memory_space=pl.ANY)
    extra = [jnp.zeros((n_dst_rows, LANES), F32)] if scatter else []
    return pl.pallas_call(
        functools.partial(_permute_kernel, scatter=scatter),
        grid=(pos.shape[0],),
        in_specs=[pl.BlockSpec((1, 1, pos.shape[2]), lambda i: (i, 0, 0), memory_space=pltpu.SMEM),
                  any_spec] + [any_spec] * len(extra),
        out_specs=any_spec,
        out_shape=jax.ShapeDtypeStruct((n_dst_rows, LANES), F32),
        scratch_shapes=[pltpu.SemaphoreType.DMA(())],
        input_output_aliases={2: 0} if scatter else {},
        compiler_params=_cparams(("arbitrary",)),
        name="dispatch" if scatter else "unpermute",
    )(pos, src, *extra)


def _moe_kernel(lo_ref, hi_ref, nv_ref, xs_ref, w1l_ref, w3l_ref, w2l_ref, w1h_ref, w3h_ref, w2h_ref,
                wrl_ref, wrh_ref, lng_ref, lnb_ref, ys_ref):
    del lo_ref, hi_ref
    tm = xs_ref.shape[0] // SLAB

    @pl.when(pl.program_id(0) < nv_ref[0])
    def _():
        x = _load_tokens(xs_ref, tm)
        xb = x.astype(BF16)

        def expert(w1_ref, w3_ref, w2_ref):
            hid = _silu(_dot(xb, w1_ref[0])) * _dot(xb, w3_ref[0])
            return _dot(hid.astype(BF16), w2_ref[0])

        s_lo = jax.nn.sigmoid(jnp.sum(x * wrl_ref[0], -1, keepdims=True))
        s_hi = jax.nn.sigmoid(jnp.sum(x * wrh_ref[0], -1, keepdims=True))
        denom = s_lo + s_hi
        m = (s_lo / denom) * expert(w1l_ref, w3l_ref, w2l_ref)
        m = m + (s_hi / denom) * expert(w1h_ref, w3h_ref, w2h_ref)
        _store_tokens(ys_ref, _layer_norm(DN_ALPHA * x + m, lng_ref[...], lnb_ref[...]), tm)

    @pl.when(pl.program_id(0) >= nv_ref[0])
    def _():
        ys_ref[...] = jnp.zeros(ys_ref.shape, F32)


def _moe(tile_lo, tile_hi, n_valid, xs, w1, w3, w2, wr_rows, ln_g_row, ln_b_row):
    tm = MOE_TILE
    n_tiles = xs.shape[0] // (tm * SLAB)
    rows = lambda i, lo, hi, nv: (jnp.minimum(i, nv[0] - 1), 0)
    w_lo = lambda i, lo, hi, nv: (lo[i], 0, 0)
    w_hi = lambda i, lo, hi, nv: (hi[i], 0, 0)
    const = lambda i, lo, hi, nv: (0, 0)
    up = (1, D_MODEL, D_EXPERT)
    down = (1, D_EXPERT, D_MODEL)
    router_row = (1, 1, D_MODEL)
    grid_spec = pltpu.PrefetchScalarGridSpec(
        num_scalar_prefetch=3,
        grid=(n_tiles,),
        in_specs=[
            pl.BlockSpec((tm * SLAB, LANES), rows),
            pl.BlockSpec(up, w_lo), pl.BlockSpec(up, w_lo), pl.BlockSpec(down, w_lo),
            pl.BlockSpec(up, w_hi), pl.BlockSpec(up, w_hi), pl.BlockSpec(down, w_hi),
            pl.BlockSpec(router_row, w_lo), pl.BlockSpec(router_row, w_hi),
            pl.BlockSpec(ln_g_row.shape, const), pl.BlockSpec(ln_b_row.shape, const),
        ],
        out_specs=pl.BlockSpec((tm * SLAB, LANES), lambda i, lo, hi, nv: (i, 0)),
    )
    return pl.pallas_call(
        _moe_kernel,
        grid_spec=grid_spec,
        out_shape=jax.ShapeDtypeStruct(xs.shape, F32),
        compiler_params=_cparams(("arbitrary",)),
        name="moe",
    )(tile_lo, tile_hi, n_valid, xs, w1, w3, w2, w1, w3, w2, wr_rows, wr_rows, ln_g_row, ln_b_row)


def _routing_plan(meta, counts, n_slots):
    cls = meta[:, 0].astype(I32)
    rank = meta[:, 1].astype(I32)
    cnt = counts[0, :N_PAIR_CLASSES].astype(I32)
    padded = ((cnt + MOE_TILE - 1) // MOE_TILE) * MOE_TILE
    end = jnp.cumsum(padded)
    start = end - padded
    classes = jnp.arange(N_PAIR_CLASSES, dtype=I32)
    pos = jnp.sum(jnp.where(cls[:, None] == classes[None, :], start[None, :], 0), -1) + rank
    n_tiles = n_slots // MOE_TILE
    n_valid = (end[-1] // MOE_TILE).astype(I32).reshape(1)
    tile_start = jnp.minimum(jnp.arange(n_tiles, dtype=I32), n_valid[0] - 1) * MOE_TILE
    tile_cls = jnp.sum((tile_start[:, None] >= end[None, :]).astype(I32), -1)
    onehot = (tile_cls[:, None] == classes[None, :]).astype(I32)
    tile_lo = jnp.sum(onehot * jnp.asarray(_PAIR_LO)[None, :], -1)
    tile_hi = jnp.sum(onehot * jnp.asarray(_PAIR_HI)[None, :], -1)
    return pos, tile_lo, tile_hi, n_valid


def _moe_layer(hs, meta, counts, w1, w3, w2, wr_rows, ln_g_row, ln_b_row):
    t = meta.shape[0]
    n_slots = t + N_PAIR_CLASSES * MOE_TILE
    pos, tile_lo, tile_hi, n_valid = _routing_plan(meta, counts, n_slots)
    pos = pos.reshape(t // PERMUTE_TOKENS, 1, PERMUTE_TOKENS)
    xs = _permute(pos, hs, n_slots * SLAB, scatter=True)
    ys = _moe(tile_lo, tile_hi, n_valid, xs, w1, w3, w2, wr_rows, ln_g_row, ln_b_row)
    return _permute(pos, ys, t * SLAB, scatter=False)


def _row(v, width=None):
    v = v.astype(F32).reshape(1, -1)
    if width is not None and v.shape[1] < width:
        v = jnp.pad(v, ((0, 0), (0, width - v.shape[1])))
    return v


def _deltanet(h2d, batch, seq, w_in, conv_w, a_log, dt_bias, norm_g):
    qkvg = 4 * D_MODEL
    w_big = w_in[:, :qkvg].astype(BF16)
    w_ab = jnp.pad(w_in[:, qkvg:], ((0, 0), (0, LANES - 2 * A_HEADS))).astype(BF16)
    q, k, v, gate, gb = _dn_in(h2d, batch, seq, w_big, w_ab, conv_w.astype(F32),
                               _row(a_log, LANES), _row(dt_bias, LANES))
    return _dn_scan(q, k, v, gate, gb, _row(norm_g), batch, seq)


def _diff_attention(h2d, batch, seq, w_in, lam, norm_g, rel_bias, lambda_init):
    lamf = lam.astype(F32)
    lam_full = jnp.exp(jnp.sum(lamf[0] * lamf[1])) - jnp.exp(jnp.sum(lamf[2] * lamf[3])) + lambda_init
    scalars = jnp.stack([lam_full, jnp.asarray(1.0 - lambda_init, F32)]).astype(F32)
    qt, k, vt = _at_in(h2d, w_in.astype(BF16))
    return _attn(scalars, qt, k, vt, _attn_bias_table(rel_bias), _row(norm_g), batch, seq)


def _gmlp(h2d, w_in, ln_g, ln_b, w_s, b_s):
    ws_causal = jnp.tril(w_s.astype(F32)).astype(BF16)
    bs_t = jnp.pad(jnp.transpose(b_s.astype(F32)), ((0, 0), (0, LANES - C_GROUPS)))
    return _gm(h2d, w_in.astype(BF16), _row(ln_g), _row(ln_b), ws_causal, bs_t)


def kernel(x, ln_mix_g, ln_mix_b, ln_ffn_g, ln_ffn_b, a_w_in, a_conv, a_a_log, a_dt_bias, a_norm_g, a_w_out, b_w_in, b_lambda, b_norm_g, b_w_out, rel_bias, c_w_in, c_ln_g, c_ln_b, c_w_s, c_b_s, c_w_out, w_router, router_bias, e_w1, e_w3, e_w2):
    batch, seq, d = x.shape
    assert d == D_MODEL and seq % max(ROW_TILE, ATT_BLOCK, DN_CHUNK) == 0
    t = batch * seq
    h = x.astype(F32).reshape(t * SLAB, LANES)

    wr_rows = jnp.transpose(w_router.astype(F32)).reshape(N_EXPERTS, 1, D_MODEL)
    wr = jnp.pad(w_router.astype(F32), ((0, 0), (0, LANES - N_EXPERTS)))
    wr_hi = wr.astype(BF16)
    wr_lo = (wr - wr_hi.astype(F32)).astype(BF16)
    wr_split = jnp.concatenate([wr_hi, wr_lo], axis=1)
    rbias_row = _row(router_bias, LANES)

    for i in range(DEPTH):
        kind, j = i % N_MIXERS, i // N_MIXERS
        if kind == 0:
            y = _deltanet(h, batch, seq, a_w_in[j], a_conv[j], a_a_log[j], a_dt_bias[j], a_norm_g[j])
            w_out = a_w_out[j]
        elif kind == 1:
            lambda_init = 0.8 - 0.6 * math.exp(-0.3 * i)
            y = _diff_attention(h, batch, seq, b_w_in[j], b_lambda[j], b_norm_g[j], rel_bias, lambda_init)
            w_out = b_w_out[j]
        else:
            y = _gmlp(h, c_w_in[j], c_ln_g[j], c_ln_b[j], c_w_s[j], c_b_s[j])
            w_out = c_w_out[j]
        hs, meta, counts = _out_proj(y, w_out.astype(BF16), h, _row(ln_mix_g[i]), _row(ln_mix_b[i]),
                                     wr_split, rbias_row)
        h = _moe_layer(hs, meta, counts, e_w1[i].astype(BF16), e_w3[i].astype(BF16),
                       e_w2[i].astype(BF16), wr_rows, _row(ln_ffn_g[i]), _row(ln_ffn_b[i]))
    return h.reshape(batch, seq, d).astype(x.dtype)
```

```python
import functools
import math

import jax
import jax.numpy as jnp
import numpy as np
from jax import lax
from jax.experimental import pallas as pl
from jax.experimental.pallas import tpu as pltpu

F32 = jnp.float32
BF16 = jnp.bfloat16
I32 = jnp.int32

D_MODEL = 1024
DEPTH = 4
N_MIXERS = 3
A_HEADS = 8
A_DK = 128
A_CONV = 4
B_DH = 64
B_HEADS = D_MODEL // (2 * B_DH)
REL_BUCKETS = 32
REL_MAX_DIST = 128
C_CHUNK = 128
C_HALF = 2 * D_MODEL
C_GROUPS = 8
C_GW = C_HALF // C_GROUPS
N_EXPERTS = 16
N_GROUPS = 4
EXPERTS_PER_GROUP = 4
D_EXPERT = 512
DN_ALPHA = (2 * DEPTH) ** 0.25
EPS = 1e-5

LANES = 128
SUBLANES = 8
VMEM_LIMIT_BYTES = 56 * 1024 * 1024

ROW_TILE = 256
DN_CHUNK = 128
ATT_BLOCK = 512
MOE_TILE = 256
PERMUTE_TOKENS = 1024
N_PAIR_CLASSES = N_GROUPS * 6
NEG_BIG = -1e30
LOG2E = math.log2(math.e)

_PAIR_LO = np.array([g * 4 + a for g in range(4) for a, b in ((0, 1), (0, 2), (0, 3), (1, 2), (1, 3), (2, 3))], np.int32)
_PAIR_HI = np.array([g * 4 + b for g in range(4) for a, b in ((0, 1), (0, 2), (0, 3), (1, 2), (1, 3), (2, 3))], np.int32)


def _cparams(sem):
    return pltpu.CompilerParams(dimension_semantics=sem, vmem_limit_bytes=VMEM_LIMIT_BYTES)


def _layer_norm(x, g, b):
    mu = jnp.mean(x, -1, keepdims=True)
    xc = x - mu
    var = jnp.mean(xc * xc, -1, keepdims=True)
    return xc * lax.rsqrt(var + EPS) * g + b


def _silu(x):
    return x * jax.nn.sigmoid(x)


def _dot(a, b):
    return jnp.dot(a, b, preferred_element_type=F32)


def _dot_nt(a, b):
    return lax.dot_general(a, b, (((1,), (1,)), ((), ())), preferred_element_type=F32)


SLAB = D_MODEL // LANES


def _load_tokens(ref, n_tok):
    return jnp.concatenate([ref[pl.ds(s, n_tok, stride=SLAB), :] for s in range(SLAB)], axis=1)


def _store_tokens(ref, val, n_tok):
    for s in range(SLAB):
        ref[pl.ds(s, n_tok, stride=SLAB), :] = val[:, s * LANES:(s + 1) * LANES]


def _split3(x):
    hi = x.astype(BF16)
    r = x - hi.astype(F32)
    mid = r.astype(BF16)
    lo = (r - mid.astype(F32)).astype(BF16)
    return hi, mid, lo


def _dn_in_kernel(x_ref, w_ref, wab_ref, conv_ref, alog_ref, dtb_ref,
                  q_ref, k_ref, v_ref, gate_ref, gb_ref, ext_ref):
    ts = x_ref.shape[0] // SLAB
    qkv_w = 3 * D_MODEL
    n_slabs = qkv_w // LANES
    s_idx = pl.program_id(1)

    @pl.when(s_idx == 0)
    def _():
        ext_ref[:, 0:SUBLANES, :] = jnp.zeros((n_slabs, SUBLANES, LANES), F32)

    @pl.when(s_idx > 0)
    def _():
        ext_ref[:, 0:SUBLANES, :] = ext_ref[:, ts:ts + SUBLANES, :]

    xb = _load_tokens(x_ref, ts).astype(BF16)
    gate_ref[...] = _dot(xb, w_ref[:, qkv_w:qkv_w + D_MODEL]).astype(BF16)
    cw = conv_ref[...]
    for part, out_ref in enumerate((q_ref, k_ref, v_ref)):
        pre = _dot(xb, w_ref[:, part * D_MODEL:(part + 1) * D_MODEL])
        for h in range(A_HEADS):
            slab = part * A_HEADS + h
            lo = h * A_DK
            ext_ref[slab, SUBLANES:SUBLANES + ts, :] = pre[:, lo:lo + A_DK]
            col = slice(slab * LANES, (slab + 1) * LANES)
            y = ext_ref[slab, pl.ds(SUBLANES - 3, ts), :] * cw[0:1, col]
            for j in range(1, A_CONV):
                y = y + ext_ref[slab, pl.ds(SUBLANES - 3 + j, ts), :] * cw[j:j + 1, col]
            y = _silu(y)
            if part == 0:
                y = y * (lax.rsqrt(jnp.sum(y * y, -1, keepdims=True) + 1e-6) * (A_DK ** -0.5))
            elif part == 1:
                y = y * lax.rsqrt(jnp.sum(y * y, -1, keepdims=True) + 1e-6)
            out_ref[:, lo:lo + A_DK] = y.astype(BF16)

    ab = _dot(xb, wab_ref[...])
    z = ab + dtb_ref[...]
    softplus = jnp.maximum(z, 0.0) + jnp.log1p(jnp.exp(-jnp.abs(z)))
    g = -jnp.exp(alog_ref[...]) * softplus
    lane = lax.broadcasted_iota(I32, ab.shape, 1)
    gb_ref[...] = jnp.where(lane < A_HEADS, g, jax.nn.sigmoid(ab))


def _dn_in(x2d, batch, seq, w_big, w_ab, conv_w, alog_row, dtb_row):
    ts = ROW_TILE
    ns = seq // ts
    rows = lambda b, s: (b * ns + s, 0)
    const = lambda b, s: (0, 0)
    t = batch * seq
    out_bf = jax.ShapeDtypeStruct((t, D_MODEL), BF16)
    return pl.pallas_call(
        _dn_in_kernel,
        grid=(batch, ns),
        in_specs=[
            pl.BlockSpec((ts * SLAB, LANES), rows),
            pl.BlockSpec(w_big.shape, const),
            pl.BlockSpec(w_ab.shape, const),
            pl.BlockSpec(conv_w.shape, const),
            pl.BlockSpec(alog_row.shape, const),
            pl.BlockSpec(dtb_row.shape, const),
        ],
        out_specs=[pl.BlockSpec((ts, D_MODEL), rows)] * 4 + [pl.BlockSpec((ts, LANES), rows)],
        out_shape=[out_bf, out_bf, out_bf, out_bf, jax.ShapeDtypeStruct((t, LANES), F32)],
        scratch_shapes=[pltpu.VMEM((3 * D_MODEL // LANES, ts + 2 * SUBLANES, LANES), F32)],
        compiler_params=_cparams(("arbitrary", "arbitrary")),
        name="dn_in",
    )(x2d, w_big, w_ab, conv_w, alog_row, dtb_row)


def _dn_scan_kernel(q_ref, k_ref, v_ref, gate_ref, gb_ref, ng_ref, o_ref, state_ref):
    c = q_ref.shape[0]
    n_doublings = int(math.log2(c)) - 1

    @pl.when(pl.program_id(1) == 0)
    def _():
        state_ref[...] = jnp.zeros(state_ref.shape, F32)

    row = lax.broadcasted_iota(I32, (c, c), 0)
    col = lax.broadcasted_iota(I32, (c, c), 1)
    causal = row >= col
    strict = row > col
    eye = (row == col).astype(F32)

    gb = gb_ref[...]
    tril = causal.astype(BF16)
    parts = _split3(gb)
    gc_all = _dot(tril, parts[0]) + _dot(tril, parts[1]) + _dot(tril, parts[2])
    gc_all_t = gc_all.T
    ng = ng_ref[...]

    heads = range(A_HEADS)
    lanes = [slice(h * A_DK, (h + 1) * A_DK) for h in heads]
    qh = [q_ref[:, lanes[h]] for h in heads]
    kh = [k_ref[:, lanes[h]] for h in heads]
    gc_col = [gc_all[:, h:h + 1] for h in heads]
    beta = [gb[:, A_HEADS + h:A_HEADS + h + 1] for h in heads]
    g_last = [gc_all[c - 1:c, h:h + 1] for h in heads]

    kq = [_dot_nt(jnp.concatenate([kh[h], qh[h]], axis=0), kh[h]) for h in heads]
    decay = [jnp.where(causal, jnp.exp(jnp.where(causal, gc_col[h] - gc_all_t[h:h + 1, :], 0.0)), 0.0)
             for h in heads]
    attn = [(kq[h][c:2 * c] * decay[h]).astype(BF16) for h in heads]

    p = [jnp.where(strict, -(kq[h][0:c] * beta[h] * decay[h]), 0.0) for h in heads]
    t_inv = [eye + p[h] for h in heads]
    for _ in range(n_doublings):
        pb = [p[h].astype(BF16) for h in heads]
        p = [_dot(pb[h], pb[h]) for h in heads]
        t_inv = [t_inv[h] + _dot(t_inv[h].astype(BF16), p[h].astype(BF16)) for h in heads]

    e_gc = [jnp.exp(gc_col[h]) for h in heads]
    khf = [kh[h].astype(F32) for h in heads]
    rhs = [jnp.concatenate([v_ref[:, lanes[h]].astype(F32) * beta[h], khf[h] * (beta[h] * e_gc[h])],
                           axis=1).astype(BF16) for h in heads]
    sol = [_dot(t_inv[h].astype(BF16), rhs[h]) for h in heads]

    state = [state_ref[h] for h in heads]
    lhs = [jnp.concatenate([sol[h][:, A_DK:2 * A_DK], qh[h].astype(F32) * e_gc[h]], axis=0).astype(BF16)
           for h in heads]
    ws = [_dot(lhs[h], state[h].astype(BF16)) for h in heads]
    v_new = [(sol[h][:, 0:A_DK] - ws[h][0:c]).astype(BF16) for h in heads]
    k_dec_t = [(khf[h] * jnp.exp(g_last[h] - gc_col[h])).T.astype(BF16) for h in heads]
    o = [ws[h][c:2 * c] + _dot(attn[h], v_new[h]) for h in heads]
    for h in heads:
        state_ref[h] = state[h] * jnp.exp(g_last[h]) + _dot(k_dec_t[h], v_new[h])
    for h in heads:
        on = o[h] * lax.rsqrt(jnp.mean(o[h] * o[h], -1, keepdims=True) + EPS) * ng
        o_ref[:, lanes[h]] = (on * _silu(gate_ref[:, lanes[h]].astype(F32))).astype(BF16)


def _dn_scan(q, k, v, gate, gb, norm_g_row, batch, seq):
    c = DN_CHUNK
    nc = seq // c
    rows = lambda b, n: (b * nc + n, 0)
    const = lambda b, n: (0, 0)
    t = batch * seq
    return pl.pallas_call(
        _dn_scan_kernel,
        grid=(batch, nc),
        in_specs=[pl.BlockSpec((c, D_MODEL), rows)] * 4
        + [pl.BlockSpec((c, LANES), rows), pl.BlockSpec(norm_g_row.shape, const)],
        out_specs=pl.BlockSpec((c, D_MODEL), rows),
        out_shape=jax.ShapeDtypeStruct((t, D_MODEL), BF16),
        scratch_shapes=[pltpu.VMEM((A_HEADS, A_DK, A_DK), F32)],
        compiler_params=_cparams(("arbitrary", "arbitrary")),
        name="dn_scan",
    )(q, k, v, gate, gb, norm_g_row)


def _at_in_kernel(x_ref, w_ref, qt_ref, k_ref, vt_ref):
    xb = _load_tokens(x_ref, k_ref.shape[0]).astype(BF16)
    q = _dot(xb, w_ref[:, 0:D_MODEL]) * (B_DH ** -0.5 * LOG2E)
    lane = lax.broadcasted_iota(I32, q.shape, 1)
    first_map = (lane % (2 * B_DH)) < B_DH
    qt_ref[0] = jnp.where(first_map, q, 0.0).T.astype(BF16)
    qt_ref[1] = jnp.where(first_map, 0.0, q).T.astype(BF16)
    k_ref[...] = _dot(xb, w_ref[:, D_MODEL:2 * D_MODEL]).astype(BF16)
    vt_ref[...] = _dot(xb, w_ref[:, 2 * D_MODEL:3 * D_MODEL]).T.astype(BF16)


def _at_in(x2d, w_in):
    t = x2d.shape[0] // SLAB
    tm = ROW_TILE
    return pl.pallas_call(
        _at_in_kernel,
        grid=(t // tm,),
        in_specs=[pl.BlockSpec((tm * SLAB, LANES), lambda i: (i, 0)), pl.BlockSpec(w_in.shape, lambda i: (0, 0))],
        out_specs=[pl.BlockSpec((2, D_MODEL, tm), lambda i: (0, 0, i)),
                   pl.BlockSpec((tm, D_MODEL), lambda i: (i, 0)),
                   pl.BlockSpec((D_MODEL, tm), lambda i: (0, i))],
        out_shape=[jax.ShapeDtypeStruct((2, D_MODEL, t), BF16),
                   jax.ShapeDtypeStruct((t, D_MODEL), BF16),
                   jax.ShapeDtypeStruct((D_MODEL, t), BF16)],
        compiler_params=_cparams(("parallel",)),
        name="at_in",
    )(x2d, w_in)


def _attn_kernel(sc_ref, q0_ref, q1_ref, k_ref, vt_ref, bias_ref, ng_ref, o_ref,
                 m_ref, l_ref, acc_ref):
    tq = o_ref.shape[0]
    tk = tq
    i = pl.program_id(2)
    lam = sc_ref[0]
    out_scale = sc_ref[1]

    m_ref[...] = jnp.full(m_ref.shape, NEG_BIG, F32)
    l_ref[...] = jnp.zeros(l_ref.shape, F32)
    acc_ref[...] = jnp.zeros(acc_ref.shape, F32)
    qts = (q0_ref[0], q1_ref[0])

    maps = range(2)

    def block(start, n_keys, bias):
        start = pl.multiple_of(start, tk)
        kb = k_ref[pl.ds(start, n_keys), :]
        vtb = vt_ref[:, pl.ds(start, n_keys)]
        s = [_dot(kb, qts[mp]) for mp in maps]
        if bias is not None:
            s = [x + bias for x in s]
        m_old = [m_ref[mp] for mp in maps]
        m_new = [jnp.maximum(m_old[mp], jnp.max(s[mp], 0, keepdims=True)) for mp in maps]
        alpha = [jnp.exp2(m_old[mp] - m_new[mp]) for mp in maps]
        p = [jnp.exp2(s[mp] - m_new[mp]) for mp in maps]
        pv = [_dot(vtb, p[mp].astype(BF16)) for mp in maps]
        for mp in maps:
            l_ref[mp] = alpha[mp] * l_ref[mp] + jnp.sum(p[mp], 0, keepdims=True)
            acc_ref[mp] = alpha[mp] * acc_ref[mp] + pv[mp]
            m_ref[mp] = m_new[mp]

    n_far = jnp.maximum(i - 1, 0)

    def far_quad(jj, carry):
        block(4 * jj * tk, 4 * tk, None)
        return carry

    lax.fori_loop(0, n_far // 4, far_quad, 0)

    @pl.when(n_far % 4 >= 2)
    def _():
        block((n_far // 4) * 4 * tk, 2 * tk, None)

    @pl.when(n_far % 2 == 1)
    def _():
        block((n_far - 1) * tk, tk, None)

    @pl.when(i >= 1)
    def _():
        block((i - 1) * tk, tk, bias_ref[0, 1])

    block(i * tk, tk, bias_ref[0, 0])

    o_t = acc_ref[0] / l_ref[0] - lam * (acc_ref[1] / l_ref[1])
    o = o_t.T
    o = o * lax.rsqrt(jnp.mean(o * o, -1, keepdims=True) + EPS) * ng_ref[...] * out_scale
    o_ref[...] = o.astype(BF16)


def _attn(scalars, qt, k, vt, bias_tab, norm_g_row, batch, seq):
    tq = ATT_BLOCK
    nq = seq // tq
    t = batch * seq
    return pl.pallas_call(
        _attn_kernel,
        grid=(batch, B_HEADS, nq),
        in_specs=[
            pl.BlockSpec(memory_space=pltpu.SMEM),
            pl.BlockSpec((1, LANES, tq), lambda b, h, i: (0, h, b * nq + i)),
            pl.BlockSpec((1, LANES, tq), lambda b, h, i: (1, h, b * nq + i)),
            pl.BlockSpec((seq, LANES), lambda b, h, i: (b, h)),
            pl.BlockSpec((LANES, seq), lambda b, h, i: (h, b)),
            pl.BlockSpec((1, 2, tq, tq), lambda b, h, i: (h, 0, 0, 0)),
            pl.BlockSpec(norm_g_row.shape, lambda b, h, i: (0, 0)),
        ],
        out_specs=pl.BlockSpec((tq, LANES), lambda b, h, i: (b * nq + i, h)),
        out_shape=jax.ShapeDtypeStruct((t, D_MODEL), BF16),
        scratch_shapes=[pltpu.VMEM((2, 1, tq), F32), pltpu.VMEM((2, 1, tq), F32),
                        pltpu.VMEM((2, LANES, tq), F32)],
        compiler_params=_cparams(("parallel", "parallel", "parallel")),
        name="attn",
    )(scalars, qt, qt, k, vt, bias_tab, norm_g_row)


def _rel_bucket(rel):
    n = jnp.maximum(rel, 0)
    max_exact = REL_BUCKETS // 2
    large = max_exact + (jnp.log(jnp.maximum(n, 1).astype(F32) / max_exact)
                         / math.log(REL_MAX_DIST / max_exact) * (REL_BUCKETS - max_exact)).astype(I32)
    large = jnp.minimum(large, REL_BUCKETS - 1)
    return jnp.where(n < max_exact, n, large)


def _attn_bias_table(rel_bias):
    tq = ATT_BLOCK
    assert tq + 1 >= REL_MAX_DIST, "blocks two back must fall in the last bucket"
    table = (rel_bias.astype(F32) - rel_bias.astype(F32)[REL_BUCKETS - 1]) * LOG2E
    period = 2 * tq
    idx = jnp.arange(period)
    diag_vec = jnp.where((idx < tq)[:, None], table[_rel_bucket(jnp.minimum(idx, tq - 1))], NEG_BIG)
    near_vec = table[_rel_bucket(jnp.where(idx < tq, idx + tq, idx - tq))]
    vecs = jnp.transpose(jnp.stack([diag_vec, near_vec], 0), (2, 0, 1))
    flat = jnp.tile(vecs, (1, 1, tq))[:, :, :tq * (period - 1)]
    return flat.reshape(B_HEADS, 2, tq, period - 1)[:, :, :, :tq]


def _gelu(x):
    return 0.5 * x * (1.0 + lax.erf(x * (2.0 ** -0.5)))


def _gm_kernel(x_ref, w_ref, lng_ref, lnb_ref, ws_ref, bs_ref, o_ref):
    tm = o_ref.shape[0]
    xb = _load_tokens(x_ref, tm).astype(BF16)
    v = _gelu(_dot(xb, w_ref[:, C_HALF:2 * C_HALF]))
    vn = _layer_norm(v, lng_ref[...], lnb_ref[...]).astype(BF16)
    bs = bs_ref[...]
    for g in range(C_GROUPS):
        lo = g * C_GW
        u = _gelu(_dot(xb, w_ref[:, lo:lo + C_GW]))
        wg = ws_ref[g]
        for ch in range(tm // C_CHUNK):
            r0 = ch * C_CHUNK
            mixed = _dot(wg, vn[r0:r0 + C_CHUNK, lo:lo + C_GW]) + bs[:, g:g + 1]
            o_ref[r0:r0 + C_CHUNK, lo:lo + C_GW] = (u[r0:r0 + C_CHUNK] * mixed).astype(BF16)


def _gm(x2d, w_in, ln_g_row, ln_b_row, ws_causal, bs_t):
    t = x2d.shape[0] // SLAB
    tm = ROW_TILE
    rows = lambda i: (i, 0)
    const2 = lambda i: (0, 0)
    return pl.pallas_call(
        _gm_kernel,
        grid=(t // tm,),
        in_specs=[
            pl.BlockSpec((tm * SLAB, LANES), rows),
            pl.BlockSpec(w_in.shape, const2),
            pl.BlockSpec(ln_g_row.shape, const2),
            pl.BlockSpec(ln_b_row.shape, const2),
            pl.BlockSpec(ws_causal.shape, lambda i: (0, 0, 0)),
            pl.BlockSpec(bs_t.shape, const2),
        ],
        out_specs=pl.BlockSpec((tm, C_HALF), rows),
        out_shape=jax.ShapeDtypeStruct((t, C_HALF), BF16),
        compiler_params=_cparams(("parallel",)),
        name="gmlp",
    )(x2d, w_in, ln_g_row, ln_b_row, ws_causal, bs_t)


def _group_member(x, lane, k):
    fwd = pltpu.roll(x, LANES - k, 1)
    back = pltpu.roll(x, EXPERTS_PER_GROUP - k, 1)
    return jnp.where((lane % EXPERTS_PER_GROUP) + k < EXPERTS_PER_GROUP, fwd, back)


def _out_kernel(y_ref, w_ref, h_ref, lng_ref, lnb_ref, wr_ref, rb_ref, hs_ref, meta_ref, cnt_ref,
                carry_ref):
    tm = y_ref.shape[0]

    @pl.when(pl.program_id(0) == 0)
    def _():
        carry_ref[...] = jnp.zeros(carry_ref.shape, F32)

    hh = DN_ALPHA * _load_tokens(h_ref, tm) + _dot(y_ref[...], w_ref[...])
    h1 = _layer_norm(hh, lng_ref[...], lnb_ref[...])
    _store_tokens(hs_ref, h1, tm)

    h_hi = h1.astype(BF16)
    h_lo = (h1 - h_hi.astype(F32)).astype(BF16)
    both = _dot(h_hi, wr_ref[...])
    logits = both[:, 0:LANES] + both[:, LANES:2 * LANES] + _dot(h_lo, wr_ref[:, 0:LANES])

    lane = lax.broadcasted_iota(I32, (tm, LANES), 1)
    lane_f = lane.astype(F32)
    real = lane < N_EXPERTS
    scores = jax.nn.sigmoid(logits)
    sel = jnp.where(real, scores + rb_ref[...], NEG_BIG)

    member = lane % EXPERTS_PER_GROUP
    others = [_group_member(sel, lane, k) for k in (1, 2, 3)]
    grp = jnp.maximum(jnp.maximum(sel + others[0], sel + others[1]),
                      jnp.maximum(sel + others[2], others[0] + others[1]))
    grp = jnp.maximum(grp, jnp.maximum(others[0] + others[2], others[1] + others[2]))
    grp = jnp.where(real, grp, NEG_BIG)
    gmax = jnp.max(grp, -1, keepdims=True)
    group_id = (lane // EXPERTS_PER_GROUP).astype(F32)
    best = jnp.min(jnp.where(grp == gmax, group_id, 1e9), -1, keepdims=True)
    beaten = jnp.zeros((tm, LANES), F32)
    for k, o in zip((1, 2, 3), others):
        o_member = (member + k) % EXPERTS_PER_GROUP
        ahead = (o > sel) | ((o == sel) & (o_member < member))
        beaten = beaten + ahead.astype(F32)
    chosen = real & (group_id == best) & (beaten < 2.0)

    e_lo = jnp.min(jnp.where(chosen, lane_f, 1e9), -1, keepdims=True)
    e_hi = jnp.max(jnp.where(chosen, lane_f, -1.0), -1, keepdims=True)
    a = e_lo - best * EXPERTS_PER_GROUP
    b = e_hi - best * EXPERTS_PER_GROUP
    cls = best * 6.0 + (a * 3.0 - a * (a - 1.0) * 0.5) + (b - a - 1.0)

    onehot = lane_f == cls
    r_i = lax.broadcasted_iota(I32, (tm, tm), 0)
    c_i = lax.broadcasted_iota(I32, (tm, tm), 1)
    before = _dot((r_i > c_i).astype(BF16), onehot.astype(BF16)) + carry_ref[0:1, :]
    rank = jnp.sum(jnp.where(onehot, before, 0.0), -1, keepdims=True)
    carry_ref[0:1, :] = carry_ref[0:1, :] + jnp.sum(onehot.astype(F32), 0, keepdims=True)
    cnt_ref[...] = carry_ref[...]

    meta_ref[...] = jnp.where(lane == 0, cls, jnp.where(lane == 1, rank, 0.0))


def _out_proj(y, w_out, hs, ln_g_row, ln_b_row, wr_split, rbias_row):
    t, kd = y.shape
    tm = ROW_TILE
    rows = lambda i: (i, 0)
    const = lambda i: (0, 0)
    return pl.pallas_call(
        _out_kernel,
        grid=(t // tm,),
        in_specs=[
            pl.BlockSpec((tm, kd), rows),
            pl.BlockSpec(w_out.shape, const),
            pl.BlockSpec((tm * SLAB, LANES), rows),
            pl.BlockSpec(ln_g_row.shape, const),
            pl.BlockSpec(ln_b_row.shape, const),
            pl.BlockSpec(wr_split.shape, const),
            pl.BlockSpec(rbias_row.shape, const),
        ],
        out_specs=[pl.BlockSpec((tm * SLAB, LANES), rows), pl.BlockSpec((tm, LANES), rows),
                   pl.BlockSpec((SUBLANES, LANES), const)],
        out_shape=[jax.ShapeDtypeStruct((t * SLAB, LANES), F32),
                   jax.ShapeDtypeStruct((t, LANES), F32),
                   jax.ShapeDtypeStruct((SUBLANES, LANES), F32)],
        scratch_shapes=[pltpu.VMEM((SUBLANES, LANES), F32)],
        compiler_params=_cparams(("arbitrary",)),
        name="out_proj",
    )(y, w_out, hs, ln_g_row, ln_b_row, wr_split, rbias_row)


def _permute_kernel(pos_ref, src_ref, *refs, scatter):
    dst_ref, sem = refs[-2:]
    n_tok = pos_ref.shape[2]

    def issue(r, carry):
        here = pl.ds(pl.multiple_of(r * SLAB, SLAB), SLAB)
        there = pl.ds(pl.multiple_of(pos_ref[0, 0, r] * SLAB, SLAB), SLAB)
        src_rows, dst_rows = (here, there) if scatter else (there, here)
        pltpu.make_async_copy(src_ref.at[src_rows, :], dst_ref.at[dst_rows, :], sem).start()
        return carry

    lax.fori_loop(0, n_tok, issue, 0, unroll=8)
    whole = pl.ds(0, n_tok * SLAB)
    pltpu.make_async_copy(src_ref.at[whole, :], dst_ref.at[whole, :], sem).wait()


def _permute(pos, src, n_dst_rows, scatter):
    n_tok = pos.shape[2]
    any_spec = pl.BlockSpec(memory_space=pl.ANY)
    tile_spec = pl.BlockSpec((n_tok * SLAB, LANES), lambda i: (i, 0))
    extra = [jnp.zeros((n_dst_rows, LANES), F32)] if scatter else []
    return pl.pallas_call(
        functools.partial(_permute_kernel, scatter=scatter),
        grid=(pos.shape[0],),
        in_specs=[pl.BlockSpec((1, 1, n_tok), lambda i: (i, 0, 0), memory_space=pltpu.SMEM),
                  tile_spec if scatter else any_spec] + [any_spec] * len(extra),
        out_specs=any_spec if scatter else tile_spec,
        out_shape=jax.ShapeDtypeStruct((n_dst_rows, LANES), F32),
        scratch_shapes=[pltpu.SemaphoreType.DMA(())],
        input_output_aliases={2: 0} if scatter else {},
        compiler_params=_cparams(("arbitrary",)),
        name="dispatch" if scatter else "unpermute",
    )(pos, src, *extra)


def _moe_kernel(lo_ref, hi_ref, nv_ref, xs_ref, w1l_ref, w3l_ref, w2l_ref, w1h_ref, w3h_ref, w2h_ref,
                wrl_ref, wrh_ref, lng_ref, lnb_ref, ys_ref):
    del lo_ref, hi_ref
    tm = xs_ref.shape[0] // SLAB

    @pl.when(pl.program_id(0) < nv_ref[0])
    def _():
        x = _load_tokens(xs_ref, tm)
        xb = x.astype(BF16)

        def expert(w1_ref, w3_ref, w2_ref):
            hid = _silu(_dot(xb, w1_ref[0])) * _dot(xb, w3_ref[0])
            return _dot(hid.astype(BF16), w2_ref[0])

        s_lo = jax.nn.sigmoid(jnp.sum(x * wrl_ref[0], -1, keepdims=True))
        s_hi = jax.nn.sigmoid(jnp.sum(x * wrh_ref[0], -1, keepdims=True))
        denom = s_lo + s_hi
        m = (s_lo / denom) * expert(w1l_ref, w3l_ref, w2l_ref)
        m = m + (s_hi / denom) * expert(w1h_ref, w3h_ref, w2h_ref)
        _store_tokens(ys_ref, _layer_norm(DN_ALPHA * x + m, lng_ref[...], lnb_ref[...]), tm)

    @pl.when(pl.program_id(0) >= nv_ref[0])
    def _():
        ys_ref[...] = jnp.zeros(ys_ref.shape, F32)


def _moe(tile_lo, tile_hi, n_valid, xs, w1, w3, w2, wr_rows, ln_g_row, ln_b_row):
    tm = MOE_TILE
    n_tiles = xs.shape[0] // (tm * SLAB)
    rows = lambda i, lo, hi, nv: (jnp.minimum(i, nv[0] - 1), 0)
    w_lo = lambda i, lo, hi, nv: (lo[i], 0, 0)
    w_hi = lambda i, lo, hi, nv: (hi[i], 0, 0)
    const = lambda i, lo, hi, nv: (0, 0)
    up = (1, D_MODEL, D_EXPERT)
    down = (1, D_EXPERT, D_MODEL)
    router_row = (1, 1, D_MODEL)
    grid_spec = pltpu.PrefetchScalarGridSpec(
        num_scalar_prefetch=3,
        grid=(n_tiles,),
        in_specs=[
            pl.BlockSpec((tm * SLAB, LANES), rows),
            pl.BlockSpec(up, w_lo), pl.BlockSpec(up, w_lo), pl.BlockSpec(down, w_lo),
            pl.BlockSpec(up, w_hi), pl.BlockSpec(up, w_hi), pl.BlockSpec(down, w_hi),
            pl.BlockSpec(router_row, w_lo), pl.BlockSpec(router_row, w_hi),
            pl.BlockSpec(ln_g_row.shape, const), pl.BlockSpec(ln_b_row.shape, const),
        ],
        out_specs=pl.BlockSpec((tm * SLAB, LANES), lambda i, lo, hi, nv: (i, 0)),
    )
    return pl.pallas_call(
        _moe_kernel,
        grid_spec=grid_spec,
        out_shape=jax.ShapeDtypeStruct(xs.shape, F32),
        compiler_params=_cparams(("arbitrary",)),
        name="moe",
    )(tile_lo, tile_hi, n_valid, xs, w1, w3, w2, w1, w3, w2, wr_rows, wr_rows, ln_g_row, ln_b_row)


def _routing_plan(meta, counts, n_slots):
    cls = meta[:, 0].astype(I32)
    rank = meta[:, 1].astype(I32)
    cnt = counts[0, :N_PAIR_CLASSES].astype(I32)
    padded = ((cnt + MOE_TILE - 1) // MOE_TILE) * MOE_TILE
    end = jnp.cumsum(padded)
    start = end - padded
    classes = jnp.arange(N_PAIR_CLASSES, dtype=I32)
    pos = jnp.sum(jnp.where(cls[:, None] == classes[None, :], start[None, :], 0), -1) + rank
    n_tiles = n_slots // MOE_TILE
    n_valid = (end[-1] // MOE_TILE).astype(I32).reshape(1)
    tile_start = jnp.minimum(jnp.arange(n_tiles, dtype=I32), n_valid[0] - 1) * MOE_TILE
    tile_cls = jnp.sum((tile_start[:, None] >= end[None, :]).astype(I32), -1)
    onehot = (tile_cls[:, None] == classes[None, :]).astype(I32)
    tile_lo = jnp.sum(onehot * jnp.asarray(_PAIR_LO)[None, :], -1)
    tile_hi = jnp.sum(onehot * jnp.asarray(_PAIR_HI)[None, :], -1)
    return pos, tile_lo, tile_hi, n_valid


def _moe_layer(hs, meta, counts, w1, w3, w2, wr_rows, ln_g_row, ln_b_row):
    t = meta.shape[0]
    n_slots = t + N_PAIR_CLASSES * MOE_TILE
    pos, tile_lo, tile_hi, n_valid = _routing_plan(meta, counts, n_slots)
    pos = pos.reshape(t // PERMUTE_TOKENS, 1, PERMUTE_TOKENS)
    xs = _permute(pos, hs, n_slots * SLAB, scatter=True)
    ys = _moe(tile_lo, tile_hi, n_valid, xs, w1, w3, w2, wr_rows, ln_g_row, ln_b_row)
    return _permute(pos, ys, t * SLAB, scatter=False)


def _row(v, width=None):
    v = v.astype(F32).reshape(1, -1)
    if width is not None and v.shape[1] < width:
        v = jnp.pad(v, ((0, 0), (0, width - v.shape[1])))
    return v


def _deltanet(h2d, batch, seq, w_in, conv_w, a_log, dt_bias, norm_g):
    qkvg = 4 * D_MODEL
    w_big = w_in[:, :qkvg].astype(BF16)
    w_ab = jnp.pad(w_in[:, qkvg:], ((0, 0), (0, LANES - 2 * A_HEADS))).astype(BF16)
    q, k, v, gate, gb = _dn_in(h2d, batch, seq, w_big, w_ab, conv_w.astype(F32),
                               _row(a_log, LANES), _row(dt_bias, LANES))
    return _dn_scan(q, k, v, gate, gb, _row(norm_g), batch, seq)


def _diff_attention(h2d, batch, seq, w_in, lam, norm_g, rel_bias, lambda_init):
    lamf = lam.astype(F32)
    lam_full = jnp.exp(jnp.sum(lamf[0] * lamf[1])) - jnp.exp(jnp.sum(lamf[2] * lamf[3])) + lambda_init
    scalars = jnp.stack([lam_full, jnp.asarray(1.0 - lambda_init, F32)]).astype(F32)
    qt, k, vt = _at_in(h2d, w_in.astype(BF16))
    return _attn(scalars, qt, k, vt, _attn_bias_table(rel_bias), _row(norm_g), batch, seq)


def _gmlp(h2d, w_in, ln_g, ln_b, w_s, b_s):
    ws_causal = jnp.tril(w_s.astype(F32)).astype(BF16)
    bs_t = jnp.pad(jnp.transpose(b_s.astype(F32)), ((0, 0), (0, LANES - C_GROUPS)))
    return _gm(h2d, w_in.astype(BF16), _row(ln_g), _row(ln_b), ws_causal, bs_t)


def kernel(x, ln_mix_g, ln_mix_b, ln_ffn_g, ln_ffn_b, a_w_in, a_conv, a_a_log, a_dt_bias, a_norm_g, a_w_out, b_w_in, b_lambda, b_norm_g, b_w_out, rel_bias, c_w_in, c_ln_g, c_ln_b, c_w_s, c_b_s, c_w_out, w_router, router_bias, e_w1, e_w3, e_w2):
    batch, seq, d = x.shape
    assert d == D_MODEL and seq % max(ROW_TILE, ATT_BLOCK, DN_CHUNK) == 0
    t = batch * seq
    h = x.astype(F32).reshape(t * SLAB, LANES)

    wr_rows = jnp.transpose(w_router.astype(F32)).reshape(N_EXPERTS, 1, D_MODEL)
    wr = jnp.pad(w_router.astype(F32), ((0, 0), (0, LANES - N_EXPERTS)))
    wr_hi = wr.astype(BF16)
    wr_lo = (wr - wr_hi.astype(F32)).astype(BF16)
    wr_split = jnp.concatenate([wr_hi, wr_lo], axis=1)
    rbias_row = _row(router_bias, LANES)

    for i in range(DEPTH):
        kind, j = i % N_MIXERS, i // N_MIXERS
        if kind == 0:
            y = _deltanet(h, batch, seq, a_w_in[j], a_conv[j], a_a_log[j], a_dt_bias[j], a_norm_g[j])
            w_out = a_w_out[j]
        elif kind == 1:
            lambda_init = 0.8 - 0.6 * math.exp(-0.3 * i)
            y = _diff_attention(h, batch, seq, b_w_in[j], b_lambda[j], b_norm_g[j], rel_bias, lambda_init)
            w_out = b_w_out[j]
        else:
            y = _gmlp(h, c_w_in[j], c_ln_g[j], c_ln_b[j], c_w_s[j], c_b_s[j])
            w_out = c_w_out[j]
        hs, meta, counts = _out_proj(y, w_out.astype(BF16), h, _row(ln_mix_g[i]), _row(ln_mix_b[i]),
                                     wr_split, rbias_row)
        h = _moe_layer(hs, meta, counts, e_w1[i].astype(BF16), e_w3[i].astype(BF16),
                       e_w2[i].astype(BF16), wr_rows, _row(ln_ffn_g[i]), _row(ln_ffn_b[i]))
    return h.reshape(batch, seq, d).astype(x.dtype)
```

```python
import functools
import math

import jax
import jax.numpy as jnp
import numpy as np
from jax import lax
from jax.experimental import pallas as pl
from jax.experimental.pallas import tpu as pltpu

F32 = jnp.float32
BF16 = jnp.bfloat16
I32 = jnp.int32

D_MODEL = 1024
DEPTH = 4
N_MIXERS = 3
A_HEADS = 8
A_DK = 128
A_CONV = 4
B_DH = 64
B_HEADS = D_MODEL // (2 * B_DH)
REL_BUCKETS = 32
REL_MAX_DIST = 128
C_CHUNK = 128
C_HALF = 2 * D_MODEL
C_GROUPS = 8
C_GW = C_HALF // C_GROUPS
N_EXPERTS = 16
N_GROUPS = 4
EXPERTS_PER_GROUP = 4
D_EXPERT = 512
DN_ALPHA = (2 * DEPTH) ** 0.25
EPS = 1e-5

LANES = 128
SUBLANES = 8
VMEM_LIMIT_BYTES = 56 * 1024 * 1024

ROW_TILE = 256
OUT_TILE = 512
DN_CHUNK = 128
ATT_BLOCK = 512
MOE_TILE = 512
PERMUTE_TOKENS = 1024
PERMUTE_UNROLL = 8
N_PAIR_CLASSES = N_GROUPS * 6
CLASS_ROWS = 32
NEG_BIG = -1e30
LOG2E = math.log2(math.e)

_PAIR_LO = np.array([g * 4 + a for g in range(4) for a, b in ((0, 1), (0, 2), (0, 3), (1, 2), (1, 3), (2, 3))], np.int32)
_PAIR_HI = np.array([g * 4 + b for g in range(4) for a, b in ((0, 1), (0, 2), (0, 3), (1, 2), (1, 3), (2, 3))], np.int32)


def _cparams(sem):
    return pltpu.CompilerParams(dimension_semantics=sem, vmem_limit_bytes=VMEM_LIMIT_BYTES)


def _layer_norm(x, g, b):
    mu = jnp.mean(x, -1, keepdims=True)
    xc = x - mu
    var = jnp.mean(xc * xc, -1, keepdims=True)
    return xc * lax.rsqrt(var + EPS) * g + b


def _silu(x):
    return x * jax.nn.sigmoid(x)


def _dot(a, b):
    return jnp.dot(a, b, preferred_element_type=F32)


def _dot_nt(a, b):
    return lax.dot_general(a, b, (((1,), (1,)), ((), ())), preferred_element_type=F32)


SLAB = D_MODEL // LANES


def _load_tokens(ref, n_tok):
    return jnp.concatenate([ref[pl.ds(s, n_tok, stride=SLAB), :] for s in range(SLAB)], axis=1)


def _store_tokens(ref, val, n_tok):
    for s in range(SLAB):
        ref[pl.ds(s, n_tok, stride=SLAB), :] = val[:, s * LANES:(s + 1) * LANES]


def _split3(x):
    hi = x.astype(BF16)
    r = x - hi.astype(F32)
    mid = r.astype(BF16)
    lo = (r - mid.astype(F32)).astype(BF16)
    return hi, mid, lo


def _dn_in_kernel(x_ref, w_ref, wab_ref, conv_ref, alog_ref, dtb_ref,
                  q_ref, k_ref, v_ref, gate_ref, gb_ref, ext_ref):
    ts = x_ref.shape[0] // SLAB
    qkv_w = 3 * D_MODEL
    n_slabs = qkv_w // LANES
    s_idx = pl.program_id(1)

    @pl.when(s_idx == 0)
    def _():
        ext_ref[:, 0:SUBLANES, :] = jnp.zeros((n_slabs, SUBLANES, LANES), F32)

    @pl.when(s_idx > 0)
    def _():
        ext_ref[:, 0:SUBLANES, :] = ext_ref[:, ts:ts + SUBLANES, :]

    xb = _load_tokens(x_ref, ts).astype(BF16)
    gate_ref[...] = _dot(xb, w_ref[:, qkv_w:qkv_w + D_MODEL]).astype(BF16)
    cw = conv_ref[...]
    for part, out_ref in enumerate((q_ref, k_ref, v_ref)):
        pre = _dot(xb, w_ref[:, part * D_MODEL:(part + 1) * D_MODEL])
        for h in range(A_HEADS):
            slab = part * A_HEADS + h
            lo = h * A_DK
            ext_ref[slab, SUBLANES:SUBLANES + ts, :] = pre[:, lo:lo + A_DK]
            col = slice(slab * LANES, (slab + 1) * LANES)
            y = ext_ref[slab, pl.ds(SUBLANES - 3, ts), :] * cw[0:1, col]
            for j in range(1, A_CONV):
                y = y + ext_ref[slab, pl.ds(SUBLANES - 3 + j, ts), :] * cw[j:j + 1, col]
            y = _silu(y)
            if part == 0:
                y = y * (lax.rsqrt(jnp.sum(y * y, -1, keepdims=True) + 1e-6) * (A_DK ** -0.5))
            elif part == 1:
                y = y * lax.rsqrt(jnp.sum(y * y, -1, keepdims=True) + 1e-6)
            out_ref[:, lo:lo + A_DK] = y.astype(BF16)

    ab = _dot(xb, wab_ref[...])
    z = ab + dtb_ref[...]
    softplus = jnp.maximum(z, 0.0) + jnp.log1p(jnp.exp(-jnp.abs(z)))
    g = -jnp.exp(alog_ref[...]) * softplus
    lane = lax.broadcasted_iota(I32, ab.shape, 1)
    gb_ref[...] = jnp.where(lane < A_HEADS, g, jax.nn.sigmoid(ab))


def _dn_in(x2d, batch, seq, w_big, w_ab, conv_w, alog_row, dtb_row):
    ts = ROW_TILE
    ns = seq // ts
    rows = lambda b, s: (b * ns + s, 0)
    const = lambda b, s: (0, 0)
    t = batch * seq
    out_bf = jax.ShapeDtypeStruct((t, D_MODEL), BF16)
    return pl.pallas_call(
        _dn_in_kernel,
        grid=(batch, ns),
        in_specs=[
            pl.BlockSpec((ts * SLAB, LANES), rows),
            pl.BlockSpec(w_big.shape, const),
            pl.BlockSpec(w_ab.shape, const),
            pl.BlockSpec(conv_w.shape, const),
            pl.BlockSpec(alog_row.shape, const),
            pl.BlockSpec(dtb_row.shape, const),
        ],
        out_specs=[pl.BlockSpec((ts, D_MODEL), rows)] * 4 + [pl.BlockSpec((ts, LANES), rows)],
        out_shape=[out_bf, out_bf, out_bf, out_bf, jax.ShapeDtypeStruct((t, LANES), F32)],
        scratch_shapes=[pltpu.VMEM((3 * D_MODEL // LANES, ts + 2 * SUBLANES, LANES), F32)],
        compiler_params=_cparams(("arbitrary", "arbitrary")),
        name="dn_in",
    )(x2d, w_big, w_ab, conv_w, alog_row, dtb_row)


def _dn_scan_kernel(q_ref, k_ref, v_ref, gate_ref, gb_ref, ng_ref, o_ref, state_ref):
    c = q_ref.shape[0]
    n_doublings = int(math.log2(c)) - 1

    @pl.when(pl.program_id(1) == 0)
    def _():
        state_ref[...] = jnp.zeros(state_ref.shape, F32)

    row = lax.broadcasted_iota(I32, (c, c), 0)
    col = lax.broadcasted_iota(I32, (c, c), 1)
    causal = row >= col
    strict = row > col
    eye = (row == col).astype(F32)

    gb = gb_ref[...]
    tril = causal.astype(BF16)
    parts = _split3(gb)
    gc_all = _dot(tril, parts[0]) + _dot(tril, parts[1]) + _dot(tril, parts[2])
    gc_all_t = gc_all.T
    ng = ng_ref[...]

    heads = range(A_HEADS)
    lanes = [slice(h * A_DK, (h + 1) * A_DK) for h in heads]
    qh = [q_ref[:, lanes[h]] for h in heads]
    kh = [k_ref[:, lanes[h]] for h in heads]
    gc_col = [gc_all[:, h:h + 1] for h in heads]
    beta = [gb[:, A_HEADS + h:A_HEADS + h + 1] for h in heads]
    g_last = [gc_all[c - 1:c, h:h + 1] for h in heads]

    kq = [_dot_nt(jnp.concatenate([kh[h], qh[h]], axis=0), kh[h]) for h in heads]
    decay = [jnp.where(causal, jnp.exp(jnp.where(causal, gc_col[h] - gc_all_t[h:h + 1, :], 0.0)), 0.0)
             for h in heads]
    attn = [(kq[h][c:2 * c] * decay[h]).astype(BF16) for h in heads]

    p = [jnp.where(strict, -(kq[h][0:c] * beta[h] * decay[h]), 0.0) for h in heads]
    t_inv = [eye + p[h] for h in heads]
    for _ in range(n_doublings):
        pb = [p[h].astype(BF16) for h in heads]
        p = [_dot(pb[h], pb[h]) for h in heads]
        t_inv = [t_inv[h] + _dot(t_inv[h].astype(BF16), p[h].astype(BF16)) for h in heads]

    e_gc = [jnp.exp(gc_col[h]) for h in heads]
    khf = [kh[h].astype(F32) for h in heads]
    rhs = [jnp.concatenate([v_ref[:, lanes[h]].astype(F32) * beta[h], khf[h] * (beta[h] * e_gc[h])],
                           axis=1).astype(BF16) for h in heads]
    sol = [_dot(t_inv[h].astype(BF16), rhs[h]) for h in heads]

    state = [state_ref[h] for h in heads]
    lhs = [jnp.concatenate([sol[h][:, A_DK:2 * A_DK], qh[h].astype(F32) * e_gc[h]], axis=0).astype(BF16)
           for h in heads]
    ws = [_dot(lhs[h], state[h].astype(BF16)) for h in heads]
    v_new = [(sol[h][:, 0:A_DK] - ws[h][0:c]).astype(BF16) for h in heads]
    k_dec_t = [(khf[h] * jnp.exp(g_last[h] - gc_col[h])).T.astype(BF16) for h in heads]
    o = [ws[h][c:2 * c] + _dot(attn[h], v_new[h]) for h in heads]
    for h in heads:
        state_ref[h] = state[h] * jnp.exp(g_last[h]) + _dot(k_dec_t[h], v_new[h])
    for h in heads:
        on = o[h] * lax.rsqrt(jnp.mean(o[h] * o[h], -1, keepdims=True) + EPS) * ng
        o_ref[:, lanes[h]] = (on * _silu(gate_ref[:, lanes[h]].astype(F32))).astype(BF16)


def _dn_scan(q, k, v, gate, gb, norm_g_row, batch, seq):
    c = DN_CHUNK
    nc = seq // c
    rows = lambda b, n: (b * nc + n, 0)
    const = lambda b, n: (0, 0)
    t = batch * seq
    return pl.pallas_call(
        _dn_scan_kernel,
        grid=(batch, nc),
        in_specs=[pl.BlockSpec((c, D_MODEL), rows)] * 4
        + [pl.BlockSpec((c, LANES), rows), pl.BlockSpec(norm_g_row.shape, const)],
        out_specs=pl.BlockSpec((c, D_MODEL), rows),
        out_shape=jax.ShapeDtypeStruct((t, D_MODEL), BF16),
        scratch_shapes=[pltpu.VMEM((A_HEADS, A_DK, A_DK), F32)],
        compiler_params=_cparams(("arbitrary", "arbitrary")),
        name="dn_scan",
    )(q, k, v, gate, gb, norm_g_row)


def _at_in_kernel(x_ref, w_ref, qt_ref, k_ref, vt_ref):
    xb = _load_tokens(x_ref, k_ref.shape[0]).astype(BF16)
    q = _dot(xb, w_ref[:, 0:D_MODEL]) * (B_DH ** -0.5 * LOG2E)
    lane = lax.broadcasted_iota(I32, q.shape, 1)
    first_map = (lane % (2 * B_DH)) < B_DH
    qt_ref[0] = jnp.where(first_map, q, 0.0).T.astype(BF16)
    qt_ref[1] = jnp.where(first_map, 0.0, q).T.astype(BF16)
    k_ref[...] = _dot(xb, w_ref[:, D_MODEL:2 * D_MODEL]).astype(BF16)
    vt_ref[...] = _dot(xb, w_ref[:, 2 * D_MODEL:3 * D_MODEL]).T.astype(BF16)


def _at_in(x2d, w_in):
    t = x2d.shape[0] // SLAB
    tm = ROW_TILE
    return pl.pallas_call(
        _at_in_kernel,
        grid=(t // tm,),
        in_specs=[pl.BlockSpec((tm * SLAB, LANES), lambda i: (i, 0)), pl.BlockSpec(w_in.shape, lambda i: (0, 0))],
        out_specs=[pl.BlockSpec((2, D_MODEL, tm), lambda i: (0, 0, i)),
                   pl.BlockSpec((tm, D_MODEL), lambda i: (i, 0)),
                   pl.BlockSpec((D_MODEL, tm), lambda i: (0, i))],
        out_shape=[jax.ShapeDtypeStruct((2, D_MODEL, t), BF16),
                   jax.ShapeDtypeStruct((t, D_MODEL), BF16),
                   jax.ShapeDtypeStruct((D_MODEL, t), BF16)],
        compiler_params=_cparams(("parallel",)),
        name="at_in",
    )(x2d, w_in)


def _attn_kernel(sc_ref, q0_ref, q1_ref, k_ref, vt_ref, bias_ref, ng_ref, o_ref,
                 m_ref, l_ref, acc_ref, sa_ref, sb_ref):
    tq = o_ref.shape[0]
    tk = tq
    i = pl.program_id(2)
    lam = sc_ref[0]
    out_scale = sc_ref[1]

    m_ref[...] = jnp.full(m_ref.shape, NEG_BIG, F32)
    l_ref[...] = jnp.zeros(l_ref.shape, F32)
    acc_ref[...] = jnp.zeros(acc_ref.shape, F32)
    qts = (q0_ref[0], q1_ref[0])

    maps = range(2)

    def scores(j, s_ref):
        kb = k_ref[pl.ds(pl.multiple_of(j * tk, tk), tk), :]
        for mp in maps:
            s_ref[mp] = _dot(kb, qts[mp])

    def absorb(j, s_ref, bias):
        vtb = vt_ref[:, pl.ds(pl.multiple_of(j * tk, tk), tk)]
        s = [s_ref[mp] for mp in maps]
        if bias is not None:
            s = [x + bias for x in s]
        m_old = [m_ref[mp] for mp in maps]
        m_new = [jnp.maximum(m_old[mp], jnp.max(s[mp], 0, keepdims=True)) for mp in maps]
        alpha = [jnp.exp2(m_old[mp] - m_new[mp]) for mp in maps]
        p = [jnp.exp2(s[mp] - m_new[mp]) for mp in maps]
        pv = [_dot(vtb, p[mp].astype(BF16)) for mp in maps]
        for mp in maps:
            l_ref[mp] = alpha[mp] * l_ref[mp] + jnp.sum(p[mp], 0, keepdims=True)
            acc_ref[mp] = alpha[mp] * acc_ref[mp] + pv[mp]
            m_ref[mp] = m_new[mp]

    @pl.when(i == 0)
    def _():
        scores(0, sa_ref)

    @pl.when(i % 2 == 1)
    def _():
        scores(0, sb_ref)

    @pl.when((i % 2 == 0) & (i >= 2))
    def _():
        scores(0, sa_ref)
        scores(1, sb_ref)
        absorb(0, sa_ref, None)

    first = jnp.where(i % 2 == 1, 0, 1)

    def far_pair(jj, carry):
        j = first + 2 * jj
        scores(j + 1, sa_ref)
        absorb(j, sb_ref, None)
        scores(j + 2, sb_ref)
        absorb(j + 1, sa_ref, None)
        return carry

    lax.fori_loop(0, jnp.maximum(i - 1, 0) // 2, far_pair, 0)

    @pl.when(i >= 1)
    def _():
        scores(i, sa_ref)
        absorb(i - 1, sb_ref, bias_ref[0, 1])

    absorb(i, sa_ref, bias_ref[0, 0])

    o_t = acc_ref[0] / l_ref[0] - lam * (acc_ref[1] / l_ref[1])
    o = o_t.T
    o = o * lax.rsqrt(jnp.mean(o * o, -1, keepdims=True) + EPS) * ng_ref[...] * out_scale
    o_ref[...] = o.astype(BF16)


def _attn(scalars, qt, k, vt, bias_tab, norm_g_row, batch, seq):
    tq = ATT_BLOCK
    nq = seq // tq
    t = batch * seq
    return pl.pallas_call(
        _attn_kernel,
        grid=(batch, B_HEADS, nq),
        in_specs=[
            pl.BlockSpec(memory_space=pltpu.SMEM),
            pl.BlockSpec((1, LANES, tq), lambda b, h, i: (0, h, b * nq + i)),
            pl.BlockSpec((1, LANES, tq), lambda b, h, i: (1, h, b * nq + i)),
            pl.BlockSpec((seq, LANES), lambda b, h, i: (b, h)),
            pl.BlockSpec((LANES, seq), lambda b, h, i: (h, b)),
            pl.BlockSpec((1, 2, tq, tq), lambda b, h, i: (h, 0, 0, 0)),
            pl.BlockSpec(norm_g_row.shape, lambda b, h, i: (0, 0)),
        ],
        out_specs=pl.BlockSpec((tq, LANES), lambda b, h, i: (b * nq + i, h)),
        out_shape=jax.ShapeDtypeStruct((t, D_MODEL), BF16),
        scratch_shapes=[pltpu.VMEM((2, 1, tq), F32), pltpu.VMEM((2, 1, tq), F32),
                        pltpu.VMEM((2, LANES, tq), F32),
                        pltpu.VMEM((2, tq, tq), F32), pltpu.VMEM((2, tq, tq), F32)],
        compiler_params=_cparams(("parallel", "parallel", "parallel")),
        name="attn",
    )(scalars, qt, qt, k, vt, bias_tab, norm_g_row)


def _rel_bucket(rel):
    n = jnp.maximum(rel, 0)
    max_exact = REL_BUCKETS // 2
    large = max_exact + (jnp.log(jnp.maximum(n, 1).astype(F32) / max_exact)
                         / math.log(REL_MAX_DIST / max_exact) * (REL_BUCKETS - max_exact)).astype(I32)
    large = jnp.minimum(large, REL_BUCKETS - 1)
    return jnp.where(n < max_exact, n, large)


def _attn_bias_table(rel_bias):
    tq = ATT_BLOCK
    assert tq + 1 >= REL_MAX_DIST, "blocks two back must fall in the last bucket"
    table = (rel_bias.astype(F32) - rel_bias.astype(F32)[REL_BUCKETS - 1]) * LOG2E
    period = 2 * tq
    idx = jnp.arange(period)
    diag_vec = jnp.where((idx < tq)[:, None], table[_rel_bucket(jnp.minimum(idx, tq - 1))], NEG_BIG)
    near_vec = table[_rel_bucket(jnp.where(idx < tq, idx + tq, idx - tq))]
    vecs = jnp.transpose(jnp.stack([diag_vec, near_vec], 0), (2, 0, 1))
    flat = jnp.tile(vecs, (1, 1, tq))[:, :, :tq * (period - 1)]
    return flat.reshape(B_HEADS, 2, tq, period - 1)[:, :, :, :tq]


def _gelu(x):
    return 0.5 * x * (1.0 + lax.erf(x * (2.0 ** -0.5)))


def _gm_kernel(x_ref, w_ref, lng_ref, lnb_ref, ws_ref, bs_ref, o_ref):
    tm = o_ref.shape[0]
    xb = _load_tokens(x_ref, tm).astype(BF16)
    v = _gelu(_dot(xb, w_ref[:, C_HALF:2 * C_HALF]))
    vn = _layer_norm(v, lng_ref[...], lnb_ref[...]).astype(BF16)
    bs = bs_ref[...]
    for g in range(C_GROUPS):
        lo = g * C_GW
        u = _gelu(_dot(xb, w_ref[:, lo:lo + C_GW]))
        wg = ws_ref[g]
        for ch in range(tm // C_CHUNK):
            r0 = ch * C_CHUNK
            mixed = _dot(wg, vn[r0:r0 + C_CHUNK, lo:lo + C_GW]) + bs[:, g:g + 1]
            o_ref[r0:r0 + C_CHUNK, lo:lo + C_GW] = (u[r0:r0 + C_CHUNK] * mixed).astype(BF16)


def _gm(x2d, w_in, ln_g_row, ln_b_row, ws_causal, bs_t):
    t = x2d.shape[0] // SLAB
    tm = ROW_TILE
    rows = lambda i: (i, 0)
    const2 = lambda i: (0, 0)
    return pl.pallas_call(
        _gm_kernel,
        grid=(t // tm,),
        in_specs=[
            pl.BlockSpec((tm * SLAB, LANES), rows),
            pl.BlockSpec(w_in.shape, const2),
            pl.BlockSpec(ln_g_row.shape, const2),
            pl.BlockSpec(ln_b_row.shape, const2),
            pl.BlockSpec(ws_causal.shape, lambda i: (0, 0, 0)),
            pl.BlockSpec(bs_t.shape, const2),
        ],
        out_specs=pl.BlockSpec((tm, C_HALF), rows),
        out_shape=jax.ShapeDtypeStruct((t, C_HALF), BF16),
        compiler_params=_cparams(("parallel",)),
        name="gmlp",
    )(x2d, w_in, ln_g_row, ln_b_row, ws_causal, bs_t)


_PAIRS = ((0, 1), (0, 2), (0, 3), (1, 2), (1, 3), (2, 3))


def _out_kernel(y_ref, w_ref, h_ref, lng_ref, lnb_ref, wr_ref, rb_ref, hs_ref, meta_ref, cnt_ref,
                carry_ref):
    tm = y_ref.shape[0]

    @pl.when(pl.program_id(0) == 0)
    def _():
        carry_ref[...] = jnp.zeros(carry_ref.shape, F32)

    hh = DN_ALPHA * _load_tokens(h_ref, tm) + _dot(y_ref[...], w_ref[...])
    h1 = _layer_norm(hh, lng_ref[...], lnb_ref[...])
    _store_tokens(hs_ref, h1, tm)

    h_hi = h1.astype(BF16)
    h_lo = (h1 - h_hi.astype(F32)).astype(BF16)
    both = _dot_nt(wr_ref[...], h_hi)
    logits = both[0:N_EXPERTS] + both[LANES:LANES + N_EXPERTS] + _dot_nt(wr_ref[0:N_EXPERTS, :], h_lo)
    sel = jax.nn.sigmoid(logits) + jnp.concatenate([rb_ref[...]] * (tm // LANES), axis=1)
    rows = [sel[e:e + 1, :] for e in range(N_EXPERTS)]

    def top2_sum(v):
        sums = [v[a] + v[b] for a, b in _PAIRS]
        return functools.reduce(jnp.maximum, sums)

    grp = [top2_sum(rows[g * EXPERTS_PER_GROUP:(g + 1) * EXPERTS_PER_GROUP]) for g in range(N_GROUPS)]
    best, best_val = jnp.zeros_like(grp[0]), grp[0]
    for g in range(1, N_GROUPS):
        better = grp[g] > best_val
        best = jnp.where(better, float(g), best)
        best_val = jnp.where(better, grp[g], best_val)
    member = []
    for k in range(EXPERTS_PER_GROUP):
        v = rows[k]
        for g in range(1, N_GROUPS):
            v = jnp.where(best == float(g), rows[g * EXPERTS_PER_GROUP + k], v)
        member.append(v)
    chosen = []
    for k in range(EXPERTS_PER_GROUP):
        beaten = jnp.zeros_like(best)
        for j in range(EXPERTS_PER_GROUP):
            if j != k:
                ahead = (member[j] >= member[k]) if j < k else (member[j] > member[k])
                beaten = beaten + ahead.astype(F32)
        chosen.append(beaten < 2.0)
    cls = best * float(len(_PAIRS))
    for idx, (a, b) in enumerate(_PAIRS):
        cls = cls + jnp.where(chosen[a] & chosen[b], float(idx), 0.0)

    n_cls_rows = carry_ref.shape[0]
    cls_row = lax.broadcasted_iota(I32, (n_cls_rows, tm), 0).astype(F32)
    onehot = (cls_row == cls).astype(F32)
    t_i = lax.broadcasted_iota(I32, (tm, tm), 0)
    t_j = lax.broadcasted_iota(I32, (tm, tm), 1)
    carry = carry_ref[...]
    before = _dot(onehot.astype(BF16), (t_i < t_j).astype(BF16)) + jnp.concatenate(
        [carry] * (tm // LANES), axis=1)
    rank = jnp.sum(onehot * before, 0, keepdims=True)
    carry_ref[...] = carry + jnp.sum(onehot, 1, keepdims=True)
    cnt_ref[...] = carry_ref[...]

    row_id = lax.broadcasted_iota(I32, meta_ref.shape, 0)
    meta_ref[...] = jnp.where(row_id == 0, cls, jnp.where(row_id == 1, rank, 0.0))


def _out_proj(y, w_out, hs, ln_g_row, ln_b_row, wr_split, rbias_row):
    t, kd = y.shape
    tm = OUT_TILE
    rows = lambda i: (i, 0)
    const = lambda i: (0, 0)
    return pl.pallas_call(
        _out_kernel,
        grid=(t // tm,),
        in_specs=[
            pl.BlockSpec((tm, kd), rows),
            pl.BlockSpec(w_out.shape, const),
            pl.BlockSpec((tm * SLAB, LANES), rows),
            pl.BlockSpec(ln_g_row.shape, const),
            pl.BlockSpec(ln_b_row.shape, const),
            pl.BlockSpec(wr_split.shape, const),
            pl.BlockSpec(rbias_row.shape, const),
        ],
        out_specs=[pl.BlockSpec((tm * SLAB, LANES), rows), pl.BlockSpec((SUBLANES, tm), lambda i: (0, i)),
                   pl.BlockSpec((CLASS_ROWS, LANES), const)],
        out_shape=[jax.ShapeDtypeStruct((t * SLAB, LANES), F32),
                   jax.ShapeDtypeStruct((SUBLANES, t), F32),
                   jax.ShapeDtypeStruct((CLASS_ROWS, LANES), F32)],
        scratch_shapes=[pltpu.VMEM((CLASS_ROWS, LANES), F32)],
        compiler_params=_cparams(("arbitrary",)),
        name="out_proj",
    )(y, w_out, hs, ln_g_row, ln_b_row, wr_split, rbias_row)


def _permute_kernel(pos_ref, src_ref, *refs, scatter):
    dst_ref, sem = refs[-2:]
    n_tok = pos_ref.shape[2]

    def issue(g, carry):
        for u in range(PERMUTE_UNROLL):
            r = g * PERMUTE_UNROLL + u
            here = pl.ds(pl.multiple_of(r * SLAB, SLAB), SLAB)
            there = pl.ds(pl.multiple_of(pos_ref[0, 0, r] * SLAB, SLAB), SLAB)
            src_rows, dst_rows = (here, there) if scatter else (there, here)
            pltpu.make_async_copy(src_ref.at[src_rows, :], dst_ref.at[dst_rows, :], sem).start(
                priority=u % 2)
        return carry

    lax.fori_loop(0, n_tok // PERMUTE_UNROLL, issue, 0)
    whole = pl.ds(0, n_tok * SLAB)
    pltpu.make_async_copy(src_ref.at[whole, :], dst_ref.at[whole, :], sem).wait()


def _permute(pos, src, n_dst_rows, scatter):
    n_tok = pos.shape[2]
    any_spec = pl.BlockSpec(memory_space=pl.ANY)
    tile_spec = pl.BlockSpec((n_tok * SLAB, LANES), lambda i: (i, 0))
    extra = [jnp.zeros((n_dst_rows, LANES), F32)] if scatter else []
    return pl.pallas_call(
        functools.partial(_permute_kernel, scatter=scatter),
        grid=(pos.shape[0],),
        in_specs=[pl.BlockSpec((1, 1, n_tok), lambda i: (i, 0, 0), memory_space=pltpu.SMEM),
                  tile_spec if scatter else any_spec] + [any_spec] * len(extra),
        out_specs=any_spec if scatter else tile_spec,
        out_shape=jax.ShapeDtypeStruct((n_dst_rows, LANES), F32),
        scratch_shapes=[pltpu.SemaphoreType.DMA(())],
        input_output_aliases={2: 0} if scatter else {},
        compiler_params=_cparams(("arbitrary",)),
        name="dispatch" if scatter else "unpermute",
    )(pos, src, *extra)


def _moe_kernel(lo_ref, hi_ref, nv_ref, xs_ref, w1l_ref, w3l_ref, w2l_ref, w1h_ref, w3h_ref, w2h_ref,
                wrl_ref, wrh_ref, lng_ref, lnb_ref, ys_ref):
    del lo_ref, hi_ref
    tm = xs_ref.shape[0] // SLAB

    @pl.when(pl.program_id(0) < nv_ref[0])
    def _():
        x = _load_tokens(xs_ref, tm)
        xb = x.astype(BF16)

        def expert(w1_ref, w3_ref, w2_ref):
            hid = _silu(_dot(xb, w1_ref[0])) * _dot(xb, w3_ref[0])
            return _dot(hid.astype(BF16), w2_ref[0])

        s_lo = jax.nn.sigmoid(jnp.sum(x * wrl_ref[0], -1, keepdims=True))
        s_hi = jax.nn.sigmoid(jnp.sum(x * wrh_ref[0], -1, keepdims=True))
        denom = s_lo + s_hi
        m = (s_lo / denom) * expert(w1l_ref, w3l_ref, w2l_ref)
        m = m + (s_hi / denom) * expert(w1h_ref, w3h_ref, w2h_ref)
        _store_tokens(ys_ref, _layer_norm(DN_ALPHA * x + m, lng_ref[...], lnb_ref[...]), tm)

    @pl.when(pl.program_id(0) >= nv_ref[0])
    def _():
        ys_ref[...] = jnp.zeros(ys_ref.shape, F32)


def _moe(tile_lo, tile_hi, n_valid, xs, w1, w3, w2, wr_rows, ln_g_row, ln_b_row):
    tm = MOE_TILE
    n_tiles = xs.shape[0] // (tm * SLAB)
    rows = lambda i, lo, hi, nv: (jnp.minimum(i, nv[0] - 1), 0)
    w_lo = lambda i, lo, hi, nv: (lo[i], 0, 0)
    w_hi = lambda i, lo, hi, nv: (hi[i], 0, 0)
    const = lambda i, lo, hi, nv: (0, 0)
    up = (1, D_MODEL, D_EXPERT)
    down = (1, D_EXPERT, D_MODEL)
    router_row = (1, 1, D_MODEL)
    grid_spec = pltpu.PrefetchScalarGridSpec(
        num_scalar_prefetch=3,
        grid=(n_tiles,),
        in_specs=[
            pl.BlockSpec((tm * SLAB, LANES), rows),
            pl.BlockSpec(up, w_lo), pl.BlockSpec(up, w_lo), pl.BlockSpec(down, w_lo),
            pl.BlockSpec(up, w_hi), pl.BlockSpec(up, w_hi), pl.BlockSpec(down, w_hi),
            pl.BlockSpec(router_row, w_lo), pl.BlockSpec(router_row, w_hi),
            pl.BlockSpec(ln_g_row.shape, const), pl.BlockSpec(ln_b_row.shape, const),
        ],
        out_specs=pl.BlockSpec((tm * SLAB, LANES), lambda i, lo, hi, nv: (i, 0)),
    )
    return pl.pallas_call(
        _moe_kernel,
        grid_spec=grid_spec,
        out_shape=jax.ShapeDtypeStruct(xs.shape, F32),
        compiler_params=_cparams(("arbitrary",)),
        name="moe",
    )(tile_lo, tile_hi, n_valid, xs, w1, w3, w2, w1, w3, w2, wr_rows, wr_rows, ln_g_row, ln_b_row)


def _routing_plan(meta, counts, n_slots):
    cls = meta[0].astype(I32)
    rank = meta[1].astype(I32)
    cnt = counts[:N_PAIR_CLASSES, 0].astype(I32)
    padded = ((cnt + MOE_TILE - 1) // MOE_TILE) * MOE_TILE
    end = jnp.cumsum(padded)
    start = end - padded
    classes = jnp.arange(N_PAIR_CLASSES, dtype=I32)
    pos = jnp.sum(jnp.where(cls[:, None] == classes[None, :], start[None, :], 0), -1) + rank
    n_tiles = n_slots // MOE_TILE
    n_valid = (end[-1] // MOE_TILE).astype(I32).reshape(1)
    tile_start = jnp.minimum(jnp.arange(n_tiles, dtype=I32), n_valid[0] - 1) * MOE_TILE
    tile_cls = jnp.sum((tile_start[:, None] >= end[None, :]).astype(I32), -1)
    onehot = (tile_cls[:, None] == classes[None, :]).astype(I32)
    tile_lo = jnp.sum(onehot * jnp.asarray(_PAIR_LO)[None, :], -1)
    tile_hi = jnp.sum(onehot * jnp.asarray(_PAIR_HI)[None, :], -1)
    return pos, tile_lo, tile_hi, n_valid


def _moe_layer(hs, meta, counts, w1, w3, w2, wr_rows, ln_g_row, ln_b_row):
    t = meta.shape[1]
    n_slots = t + N_PAIR_CLASSES * MOE_TILE
    pos, tile_lo, tile_hi, n_valid = _routing_plan(meta, counts, n_slots)
    pos = pos.reshape(t // PERMUTE_TOKENS, 1, PERMUTE_TOKENS)
    xs = _permute(pos, hs, n_slots * SLAB, scatter=True)
    ys = _moe(tile_lo, tile_hi, n_valid, xs, w1, w3, w2, wr_rows, ln_g_row, ln_b_row)
    return _permute(pos, ys, t * SLAB, scatter=False)


def _row(v, width=None):
    v = v.astype(F32).reshape(1, -1)
    if width is not None and v.shape[1] < width:
        v = jnp.pad(v, ((0, 0), (0, width - v.shape[1])))
    return v


def _deltanet(h2d, batch, seq, w_in, conv_w, a_log, dt_bias, norm_g):
    qkvg = 4 * D_MODEL
    w_big = w_in[:, :qkvg].astype(BF16)
    w_ab = jnp.pad(w_in[:, qkvg:], ((0, 0), (0, LANES - 2 * A_HEADS))).astype(BF16)
    q, k, v, gate, gb = _dn_in(h2d, batch, seq, w_big, w_ab, conv_w.astype(F32),
                               _row(a_log, LANES), _row(dt_bias, LANES))
    return _dn_scan(q, k, v, gate, gb, _row(norm_g), batch, seq)


def _diff_attention(h2d, batch, seq, w_in, lam, norm_g, rel_bias, lambda_init):
    lamf = lam.astype(F32)
    lam_full = jnp.exp(jnp.sum(lamf[0] * lamf[1])) - jnp.exp(jnp.sum(lamf[2] * lamf[3])) + lambda_init
    scalars = jnp.stack([lam_full, jnp.asarray(1.0 - lambda_init, F32)]).astype(F32)
    qt, k, vt = _at_in(h2d, w_in.astype(BF16))
    return _attn(scalars, qt, k, vt, _attn_bias_table(rel_bias), _row(norm_g), batch, seq)


def _gmlp(h2d, w_in, ln_g, ln_b, w_s, b_s):
    ws_causal = jnp.tril(w_s.astype(F32)).astype(BF16)
    bs_t = jnp.pad(jnp.transpose(b_s.astype(F32)), ((0, 0), (0, LANES - C_GROUPS)))
    return _gm(h2d, w_in.astype(BF16), _row(ln_g), _row(ln_b), ws_causal, bs_t)


def kernel(x, ln_mix_g, ln_mix_b, ln_ffn_g, ln_ffn_b, a_w_in, a_conv, a_a_log, a_dt_bias, a_norm_g, a_w_out, b_w_in, b_lambda, b_norm_g, b_w_out, rel_bias, c_w_in, c_ln_g, c_ln_b, c_w_s, c_b_s, c_w_out, w_router, router_bias, e_w1, e_w3, e_w2):
    batch, seq, d = x.shape
    assert d == D_MODEL and seq % max(ROW_TILE, ATT_BLOCK, DN_CHUNK) == 0
    t = batch * seq
    h = x.astype(F32).reshape(t * SLAB, LANES)

    wr_rows = jnp.transpose(w_router.astype(F32)).reshape(N_EXPERTS, 1, D_MODEL)
    wr = jnp.pad(jnp.transpose(w_router.astype(F32)), ((0, LANES - N_EXPERTS), (0, 0)))
    wr_hi = wr.astype(BF16)
    wr_lo = (wr - wr_hi.astype(F32)).astype(BF16)
    wr_split = jnp.concatenate([wr_hi, wr_lo], axis=0)
    rbias_row = jnp.broadcast_to(router_bias.astype(F32)[:, None], (N_EXPERTS, LANES))

    for i in range(DEPTH):
        kind, j = i % N_MIXERS, i // N_MIXERS
        if kind == 0:
            y = _deltanet(h, batch, seq, a_w_in[j], a_conv[j], a_a_log[j], a_dt_bias[j], a_norm_g[j])
            w_out = a_w_out[j]
        elif kind == 1:
            lambda_init = 0.8 - 0.6 * math.exp(-0.3 * i)
            y = _diff_attention(h, batch, seq, b_w_in[j], b_lambda[j], b_norm_g[j], rel_bias, lambda_init)
            w_out = b_w_out[j]
        else:
            y = _gmlp(h, c_w_in[j], c_ln_g[j], c_ln_b[j], c_w_s[j], c_b_s[j])
            w_out = c_w_out[j]
        hs, meta, counts = _out_proj(y, w_out.astype(BF16), h, _row(ln_mix_g[i]), _row(ln_mix_b[i]),
                                     wr_split, rbias_row)
        h = _moe_layer(hs, meta, counts, e_w1[i].astype(BF16), e_w3[i].astype(BF16),
                       e_w2[i].astype(BF16), wr_rows, _row(ln_ffn_g[i]), _row(ln_ffn_b[i]))
    return h.reshape(batch, seq, d).astype(x.dtype)
```

```python
import functools
import math

import jax
import jax.numpy as jnp
import numpy as np
from jax import lax
from jax.experimental import pallas as pl
from jax.experimental.pallas import tpu as pltpu

F32 = jnp.float32
BF16 = jnp.bfloat16
I32 = jnp.int32

D_MODEL = 1024
DEPTH = 4
N_MIXERS = 3
A_HEADS = 8
A_DK = 128
A_CONV = 4
B_DH = 64
B_HEADS = D_MODEL // (2 * B_DH)
REL_BUCKETS = 32
REL_MAX_DIST = 128
C_CHUNK = 128
C_HALF = 2 * D_MODEL
C_GROUPS = 8
C_GW = C_HALF // C_GROUPS
N_EXPERTS = 16
N_GROUPS = 4
EXPERTS_PER_GROUP = 4
D_EXPERT = 512
DN_ALPHA = (2 * DEPTH) ** 0.25
EPS = 1e-5

LANES = 128
SUBLANES = 8
VMEM_LIMIT_BYTES = 56 * 1024 * 1024

ROW_TILE = 256
OUT_TILE = 512
DN_CHUNK = 128
ATT_BLOCK = 512
V_ROWS = 2 * B_DH + 16
MOE_TILE = 512
PERMUTE_TOKENS = 1024
PERMUTE_UNROLL = 8
N_PAIR_CLASSES = N_GROUPS * 6
CLASS_ROWS = 32
NEG_BIG = -1e30
LOG2E = math.log2(math.e)

_PAIR_LO = np.array([g * 4 + a for g in range(4) for a, b in ((0, 1), (0, 2), (0, 3), (1, 2), (1, 3), (2, 3))], np.int32)
_PAIR_HI = np.array([g * 4 + b for g in range(4) for a, b in ((0, 1), (0, 2), (0, 3), (1, 2), (1, 3), (2, 3))], np.int32)


def _cparams(sem):
    return pltpu.CompilerParams(dimension_semantics=sem, vmem_limit_bytes=VMEM_LIMIT_BYTES)


def _layer_norm(x, g, b):
    mu = jnp.mean(x, -1, keepdims=True)
    xc = x - mu
    var = jnp.mean(xc * xc, -1, keepdims=True)
    return xc * lax.rsqrt(var + EPS) * g + b


def _silu(x):
    return x * jax.nn.sigmoid(x)


def _dot(a, b):
    return jnp.dot(a, b, preferred_element_type=F32)


def _dot_nt(a, b):
    return lax.dot_general(a, b, (((1,), (1,)), ((), ())), preferred_element_type=F32)


SLAB = D_MODEL // LANES


def _load_tokens(ref, n_tok):
    if ref.shape[-1] == D_MODEL:
        return ref[...]
    return jnp.concatenate([ref[pl.ds(s, n_tok, stride=SLAB), :] for s in range(SLAB)], axis=1)


def _token_spec(arr, n_tok, index_map):
    if arr.shape[-1] == D_MODEL:
        return pl.BlockSpec((n_tok, D_MODEL), index_map)
    return pl.BlockSpec((n_tok * SLAB, LANES), index_map)


def _store_tokens(ref, val, n_tok):
    for s in range(SLAB):
        ref[pl.ds(s, n_tok, stride=SLAB), :] = val[:, s * LANES:(s + 1) * LANES]


def _split3(x):
    hi = x.astype(BF16)
    r = x - hi.astype(F32)
    mid = r.astype(BF16)
    lo = (r - mid.astype(F32)).astype(BF16)
    return hi, mid, lo


def _dn_in_kernel(x_ref, w_ref, wab_ref, conv_ref, alog_ref, dtb_ref,
                  q_ref, k_ref, v_ref, gate_ref, gb_ref, ext_ref):
    ts = q_ref.shape[0]
    qkv_w = 3 * D_MODEL
    n_slabs = qkv_w // LANES
    s_idx = pl.program_id(1)

    @pl.when(s_idx == 0)
    def _():
        ext_ref[:, 0:SUBLANES, :] = jnp.zeros((n_slabs, SUBLANES, LANES), F32)

    @pl.when(s_idx > 0)
    def _():
        ext_ref[:, 0:SUBLANES, :] = ext_ref[:, ts:ts + SUBLANES, :]

    xb = _load_tokens(x_ref, ts).astype(BF16)
    gate_ref[...] = _dot(xb, w_ref[:, qkv_w:qkv_w + D_MODEL]).astype(BF16)
    cw = conv_ref[...]
    for part, out_ref in enumerate((q_ref, k_ref, v_ref)):
        pre = _dot(xb, w_ref[:, part * D_MODEL:(part + 1) * D_MODEL])
        for h in range(A_HEADS):
            slab = part * A_HEADS + h
            lo = h * A_DK
            ext_ref[slab, SUBLANES:SUBLANES + ts, :] = pre[:, lo:lo + A_DK]
            col = slice(slab * LANES, (slab + 1) * LANES)
            y = ext_ref[slab, pl.ds(SUBLANES - 3, ts), :] * cw[0:1, col]
            for j in range(1, A_CONV):
                y = y + ext_ref[slab, pl.ds(SUBLANES - 3 + j, ts), :] * cw[j:j + 1, col]
            y = _silu(y)
            if part == 0:
                y = y * (lax.rsqrt(jnp.sum(y * y, -1, keepdims=True) + 1e-6) * (A_DK ** -0.5))
            elif part == 1:
                y = y * lax.rsqrt(jnp.sum(y * y, -1, keepdims=True) + 1e-6)
            out_ref[:, lo:lo + A_DK] = y.astype(BF16)

    ab = _dot(xb, wab_ref[...])
    z = ab + dtb_ref[...]
    softplus = jnp.maximum(z, 0.0) + jnp.log1p(jnp.exp(-jnp.abs(z)))
    g = -jnp.exp(alog_ref[...]) * softplus
    lane = lax.broadcasted_iota(I32, ab.shape, 1)
    gb_ref[...] = jnp.where(lane < A_HEADS, g, jax.nn.sigmoid(ab))


def _dn_in(x2d, batch, seq, w_big, w_ab, conv_w, alog_row, dtb_row):
    ts = ROW_TILE
    ns = seq // ts
    rows = lambda b, s: (b * ns + s, 0)
    const = lambda b, s: (0, 0)
    t = batch * seq
    out_bf = jax.ShapeDtypeStruct((t, D_MODEL), BF16)
    return pl.pallas_call(
        _dn_in_kernel,
        grid=(batch, ns),
        in_specs=[
            _token_spec(x2d, ts, rows),
            pl.BlockSpec(w_big.shape, const),
            pl.BlockSpec(w_ab.shape, const),
            pl.BlockSpec(conv_w.shape, const),
            pl.BlockSpec(alog_row.shape, const),
            pl.BlockSpec(dtb_row.shape, const),
        ],
        out_specs=[pl.BlockSpec((ts, D_MODEL), rows)] * 4 + [pl.BlockSpec((ts, LANES), rows)],
        out_shape=[out_bf, out_bf, out_bf, out_bf, jax.ShapeDtypeStruct((t, LANES), F32)],
        scratch_shapes=[pltpu.VMEM((3 * D_MODEL // LANES, ts + 2 * SUBLANES, LANES), F32)],
        compiler_params=_cparams(("arbitrary", "arbitrary")),
        name="dn_in",
    )(x2d, w_big, w_ab, conv_w, alog_row, dtb_row)


def _dn_scan_kernel(q_ref, k_ref, v_ref, gate_ref, gb_ref, ng_ref, o_ref, state_ref):
    c = q_ref.shape[0]
    n_doublings = int(math.log2(c)) - 1

    @pl.when(pl.program_id(1) == 0)
    def _():
        state_ref[...] = jnp.zeros(state_ref.shape, F32)

    row = lax.broadcasted_iota(I32, (c, c), 0)
    col = lax.broadcasted_iota(I32, (c, c), 1)
    causal = row >= col
    strict = row > col
    eye = (row == col).astype(F32)

    gb = gb_ref[...]
    tril = causal.astype(BF16)
    parts = _split3(gb)
    gc_all = _dot(tril, parts[0]) + _dot(tril, parts[1]) + _dot(tril, parts[2])
    gc_all_t = gc_all.T
    ng = ng_ref[...]

    heads = range(A_HEADS)
    lanes = [slice(h * A_DK, (h + 1) * A_DK) for h in heads]
    qh = [q_ref[:, lanes[h]] for h in heads]
    kh = [k_ref[:, lanes[h]] for h in heads]
    gc_col = [gc_all[:, h:h + 1] for h in heads]
    beta = [gb[:, A_HEADS + h:A_HEADS + h + 1] for h in heads]
    g_last = [gc_all[c - 1:c, h:h + 1] for h in heads]

    kq = [_dot_nt(jnp.concatenate([kh[h], qh[h]], axis=0), kh[h]) for h in heads]
    decay = [jnp.where(causal, jnp.exp(jnp.where(causal, gc_col[h] - gc_all_t[h:h + 1, :], 0.0)), 0.0)
             for h in heads]
    attn = [(kq[h][c:2 * c] * decay[h]).astype(BF16) for h in heads]

    p = [jnp.where(strict, -(kq[h][0:c] * beta[h] * decay[h]), 0.0) for h in heads]
    t_inv = [eye + p[h] for h in heads]
    for _ in range(n_doublings):
        pb = [p[h].astype(BF16) for h in heads]
        p = [_dot(pb[h], pb[h]) for h in heads]
        t_inv = [t_inv[h] + _dot(t_inv[h].astype(BF16), p[h].astype(BF16)) for h in heads]

    e_gc = [jnp.exp(gc_col[h]) for h in heads]
    khf = [kh[h].astype(F32) for h in heads]
    rhs = [jnp.concatenate([v_ref[:, lanes[h]].astype(F32) * beta[h], khf[h] * (beta[h] * e_gc[h])],
                           axis=1).astype(BF16) for h in heads]
    sol = [_dot(t_inv[h].astype(BF16), rhs[h]) for h in heads]

    state = [state_ref[h] for h in heads]
    lhs = [jnp.concatenate([sol[h][:, A_DK:2 * A_DK], qh[h].astype(F32) * e_gc[h]], axis=0).astype(BF16)
           for h in heads]
    ws = [_dot(lhs[h], state[h].astype(BF16)) for h in heads]
    v_new = [(sol[h][:, 0:A_DK] - ws[h][0:c]).astype(BF16) for h in heads]
    k_dec_t = [(khf[h] * jnp.exp(g_last[h] - gc_col[h])).T.astype(BF16) for h in heads]
    o = [ws[h][c:2 * c] + _dot(attn[h], v_new[h]) for h in heads]
    for h in heads:
        state_ref[h] = state[h] * jnp.exp(g_last[h]) + _dot(k_dec_t[h], v_new[h])
    for h in heads:
        on = o[h] * lax.rsqrt(jnp.mean(o[h] * o[h], -1, keepdims=True) + EPS) * ng
        o_ref[:, lanes[h]] = (on * _silu(gate_ref[:, lanes[h]].astype(F32))).astype(BF16)


def _dn_scan(q, k, v, gate, gb, norm_g_row, batch, seq):
    c = DN_CHUNK
    nc = seq // c
    rows = lambda b, n: (b * nc + n, 0)
    const = lambda b, n: (0, 0)
    t = batch * seq
    return pl.pallas_call(
        _dn_scan_kernel,
        grid=(batch, nc),
        in_specs=[pl.BlockSpec((c, D_MODEL), rows)] * 4
        + [pl.BlockSpec((c, LANES), rows), pl.BlockSpec(norm_g_row.shape, const)],
        out_specs=pl.BlockSpec((c, D_MODEL), rows),
        out_shape=jax.ShapeDtypeStruct((t, D_MODEL), BF16),
        scratch_shapes=[pltpu.VMEM((A_HEADS, A_DK, A_DK), F32)],
        compiler_params=_cparams(("arbitrary", "arbitrary")),
        name="dn_scan",
    )(q, k, v, gate, gb, norm_g_row)


def _at_in_kernel(x_ref, w_ref, qt_ref, k_ref, vt_ref):
    tm = k_ref.shape[1]
    xb = _load_tokens(x_ref, tm).astype(BF16)
    q = _dot(xb, w_ref[:, 0:D_MODEL]) * (B_DH ** -0.5 * LOG2E)
    lane = lax.broadcasted_iota(I32, q.shape, 1)
    first_map = (lane % (2 * B_DH)) < B_DH
    qt_ref[0] = jnp.where(first_map, q, 0.0).T.astype(BF16)
    qt_ref[1] = jnp.where(first_map, 0.0, q).T.astype(BF16)
    k = _dot(xb, w_ref[:, D_MODEL:2 * D_MODEL]).astype(BF16)
    vt = _dot(xb, w_ref[:, 2 * D_MODEL:3 * D_MODEL]).T.astype(BF16)
    ones = jnp.ones((V_ROWS - 2 * B_DH, tm), BF16)
    for h in range(B_HEADS):
        k_ref[h] = k[:, h * LANES:(h + 1) * LANES]
        vt_ref[h] = jnp.concatenate([vt[h * LANES:(h + 1) * LANES], ones], axis=0)


def _at_in(x2d, w_in):
    t = x2d.shape[0] // SLAB
    tm = ROW_TILE
    return pl.pallas_call(
        _at_in_kernel,
        grid=(t // tm,),
        in_specs=[pl.BlockSpec((tm * SLAB, LANES), lambda i: (i, 0)), pl.BlockSpec(w_in.shape, lambda i: (0, 0))],
        out_specs=[pl.BlockSpec((2, D_MODEL, tm), lambda i: (0, 0, i)),
                   pl.BlockSpec((B_HEADS, tm, LANES), lambda i: (0, i, 0)),
                   pl.BlockSpec((B_HEADS, V_ROWS, tm), lambda i: (0, 0, i))],
        out_shape=[jax.ShapeDtypeStruct((2, D_MODEL, t), BF16),
                   jax.ShapeDtypeStruct((B_HEADS, t, LANES), BF16),
                   jax.ShapeDtypeStruct((B_HEADS, V_ROWS, t), BF16)],
        compiler_params=_cparams(("parallel",)),
        name="at_in",
    )(x2d, w_in)


def _attn_kernel(sc_ref, q0_ref, q1_ref, k_ref, vt_ref, bias_ref, ng_ref, o_ref,
                 m_ref, acc_ref, sa_ref, sb_ref):
    tq = o_ref.shape[0]
    tk = tq
    i = pl.program_id(2)
    lam = sc_ref[0]
    out_scale = sc_ref[1]

    m_ref[...] = jnp.full(m_ref.shape, NEG_BIG, F32)
    acc_ref[...] = jnp.zeros(acc_ref.shape, F32)
    qts = (q0_ref[0], q1_ref[0])

    maps = range(2)

    def scores(j, s_ref):
        kb = k_ref[0, pl.ds(pl.multiple_of(j * tk, tk), tk), :]
        for mp in maps:
            s_ref[mp] = _dot(kb, qts[mp])

    def absorb(j, s_ref, bias):
        vtb = vt_ref[0, :, pl.ds(pl.multiple_of(j * tk, tk), tk)]
        s = [s_ref[mp] for mp in maps]
        if bias is not None:
            s = [x + bias for x in s]
        m_old = [m_ref[mp] for mp in maps]
        m_new = [jnp.maximum(m_old[mp], jnp.max(s[mp], 0, keepdims=True)) for mp in maps]
        alpha = [jnp.exp2(m_old[mp] - m_new[mp]) for mp in maps]
        p = [jnp.exp2(s[mp] - m_new[mp]) for mp in maps]
        pv = [_dot(vtb, p[mp].astype(BF16)) for mp in maps]
        for mp in maps:
            acc_ref[mp] = alpha[mp] * acc_ref[mp] + pv[mp]
            m_ref[mp] = m_new[mp]

    @pl.when(i == 0)
    def _():
        scores(0, sa_ref)

    @pl.when(i % 2 == 1)
    def _():
        scores(0, sb_ref)

    @pl.when((i % 2 == 0) & (i >= 2))
    def _():
        scores(0, sa_ref)
        scores(1, sb_ref)
        absorb(0, sa_ref, None)

    first = jnp.where(i % 2 == 1, 0, 1)

    def far_pair(jj, carry):
        j = first + 2 * jj
        scores(j + 1, sa_ref)
        absorb(j, sb_ref, None)
        scores(j + 2, sb_ref)
        absorb(j + 1, sa_ref, None)
        return carry

    lax.fori_loop(0, jnp.maximum(i - 1, 0) // 2, far_pair, 0)

    @pl.when(i >= 1)
    def _():
        scores(i, sa_ref)
        absorb(i - 1, sb_ref, bias_ref[0, 1])

    absorb(i, sa_ref, bias_ref[0, 0])

    dv = 2 * B_DH
    o_t = (acc_ref[0, 0:dv] / acc_ref[0, dv:dv + 1]
           - lam * (acc_ref[1, 0:dv] / acc_ref[1, dv:dv + 1]))
    o = o_t.T
    o = o * lax.rsqrt(jnp.mean(o * o, -1, keepdims=True) + EPS) * ng_ref[...] * out_scale
    o_ref[...] = o.astype(BF16)


def _attn(scalars, qt, k, vt, bias_tab, norm_g_row, batch, seq):
    tq = ATT_BLOCK
    nq = seq // tq
    t = batch * seq
    return pl.pallas_call(
        _attn_kernel,
        grid=(batch, B_HEADS, nq),
        in_specs=[
            pl.BlockSpec(memory_space=pltpu.SMEM),
            pl.BlockSpec((1, LANES, tq), lambda b, h, i: (0, h, b * nq + i)),
            pl.BlockSpec((1, LANES, tq), lambda b, h, i: (1, h, b * nq + i)),
            pl.BlockSpec((1, seq, LANES), lambda b, h, i: (h, b, 0)),
            pl.BlockSpec((1, V_ROWS, seq), lambda b, h, i: (h, 0, b)),
            pl.BlockSpec((1, 2, tq, tq), lambda b, h, i: (h, 0, 0, 0)),
            pl.BlockSpec(norm_g_row.shape, lambda b, h, i: (0, 0)),
        ],
        out_specs=pl.BlockSpec((tq, LANES), lambda b, h, i: (b * nq + i, h)),
        out_shape=jax.ShapeDtypeStruct((t, D_MODEL), BF16),
        scratch_shapes=[pltpu.VMEM((2, 1, tq), F32), pltpu.VMEM((2, V_ROWS, tq), F32),
                        pltpu.VMEM((2, tq, tq), F32), pltpu.VMEM((2, tq, tq), F32)],
        compiler_params=_cparams(("parallel", "parallel", "parallel")),
        name="attn",
    )(scalars, qt, qt, k, vt, bias_tab, norm_g_row)


def _rel_bucket(rel):
    n = jnp.maximum(rel, 0)
    max_exact = REL_BUCKETS // 2
    large = max_exact + (jnp.log(jnp.maximum(n, 1).astype(F32) / max_exact)
                         / math.log(REL_MAX_DIST / max_exact) * (REL_BUCKETS - max_exact)).astype(I32)
    large = jnp.minimum(large, REL_BUCKETS - 1)
    return jnp.where(n < max_exact, n, large)


def _attn_bias_table(rel_bias):
    tq = ATT_BLOCK
    assert tq + 1 >= REL_MAX_DIST, "blocks two back must fall in the last bucket"
    table = (rel_bias.astype(F32) - rel_bias.astype(F32)[REL_BUCKETS - 1]) * LOG2E
    period = 2 * tq
    idx = jnp.arange(period)
    diag_vec = jnp.where((idx < tq)[:, None], table[_rel_bucket(jnp.minimum(idx, tq - 1))], NEG_BIG)
    near_vec = table[_rel_bucket(jnp.where(idx < tq, idx + tq, idx - tq))]
    vecs = jnp.transpose(jnp.stack([diag_vec, near_vec], 0), (2, 0, 1))
    flat = jnp.tile(vecs, (1, 1, tq))[:, :, :tq * (period - 1)]
    return flat.reshape(B_HEADS, 2, tq, period - 1)[:, :, :, :tq]


def _gelu(x):
    return 0.5 * x * (1.0 + lax.erf(x * (2.0 ** -0.5)))


def _gm_kernel(x_ref, w_ref, lng_ref, lnb_ref, ws_ref, bs_ref, o_ref):
    tm = o_ref.shape[0]
    xb = _load_tokens(x_ref, tm).astype(BF16)
    v = _gelu(_dot(xb, w_ref[:, C_HALF:2 * C_HALF]))
    vn = _layer_norm(v, lng_ref[...], lnb_ref[...]).astype(BF16)
    bs = bs_ref[...]
    for g in range(C_GROUPS):
        lo = g * C_GW
        u = _gelu(_dot(xb, w_ref[:, lo:lo + C_GW]))
        wg = ws_ref[g]
        for ch in range(tm // C_CHUNK):
            r0 = ch * C_CHUNK
            mixed = _dot(wg, vn[r0:r0 + C_CHUNK, lo:lo + C_GW]) + bs[:, g:g + 1]
            o_ref[r0:r0 + C_CHUNK, lo:lo + C_GW] = (u[r0:r0 + C_CHUNK] * mixed).astype(BF16)


def _gm(x2d, w_in, ln_g_row, ln_b_row, ws_causal, bs_t):
    t = x2d.shape[0] // SLAB
    tm = ROW_TILE
    rows = lambda i: (i, 0)
    const2 = lambda i: (0, 0)
    return pl.pallas_call(
        _gm_kernel,
        grid=(t // tm,),
        in_specs=[
            pl.BlockSpec((tm * SLAB, LANES), rows),
            pl.BlockSpec(w_in.shape, const2),
            pl.BlockSpec(ln_g_row.shape, const2),
            pl.BlockSpec(ln_b_row.shape, const2),
            pl.BlockSpec(ws_causal.shape, lambda i: (0, 0, 0)),
            pl.BlockSpec(bs_t.shape, const2),
        ],
        out_specs=pl.BlockSpec((tm, C_HALF), rows),
        out_shape=jax.ShapeDtypeStruct((t, C_HALF), BF16),
        compiler_params=_cparams(("parallel",)),
        name="gmlp",
    )(x2d, w_in, ln_g_row, ln_b_row, ws_causal, bs_t)


_PAIRS = ((0, 1), (0, 2), (0, 3), (1, 2), (1, 3), (2, 3))


def _out_kernel(y_ref, w_ref, h_ref, lng_ref, lnb_ref, wr_ref, rb_ref, hs_ref, meta_ref, cnt_ref,
                carry_ref):
    tm = y_ref.shape[0]

    @pl.when(pl.program_id(0) == 0)
    def _():
        carry_ref[...] = jnp.zeros(carry_ref.shape, F32)

    hh = DN_ALPHA * _load_tokens(h_ref, tm) + _dot(y_ref[...], w_ref[...])
    h1 = _layer_norm(hh, lng_ref[...], lnb_ref[...])
    _store_tokens(hs_ref, h1, tm)

    h_hi = h1.astype(BF16)
    h_lo = (h1 - h_hi.astype(F32)).astype(BF16)
    both = _dot_nt(wr_ref[...], h_hi)
    logits = both[0:N_EXPERTS] + both[LANES:LANES + N_EXPERTS] + _dot_nt(wr_ref[0:N_EXPERTS, :], h_lo)
    sel = jax.nn.sigmoid(logits) + jnp.concatenate([rb_ref[...]] * (tm // LANES), axis=1)
    rows = [sel[e:e + 1, :] for e in range(N_EXPERTS)]

    def top2_sum(v):
        sums = [v[a] + v[b] for a, b in _PAIRS]
        return functools.reduce(jnp.maximum, sums)

    grp = [top2_sum(rows[g * EXPERTS_PER_GROUP:(g + 1) * EXPERTS_PER_GROUP]) for g in range(N_GROUPS)]
    best, best_val = jnp.zeros_like(grp[0]), grp[0]
    for g in range(1, N_GROUPS):
        better = grp[g] > best_val
        best = jnp.where(better, float(g), best)
        best_val = jnp.where(better, grp[g], best_val)
    member = []
    for k in range(EXPERTS_PER_GROUP):
        v = rows[k]
        for g in range(1, N_GROUPS):
            v = jnp.where(best == float(g), rows[g * EXPERTS_PER_GROUP + k], v)
        member.append(v)
    chosen = []
    for k in range(EXPERTS_PER_GROUP):
        beaten = jnp.zeros_like(best)
        for j in range(EXPERTS_PER_GROUP):
            if j != k:
                ahead = (member[j] >= member[k]) if j < k else (member[j] > member[k])
                beaten = beaten + ahead.astype(F32)
        chosen.append(beaten < 2.0)
    cls = best * float(len(_PAIRS))
    for idx, (a, b) in enumerate(_PAIRS):
        cls = cls + jnp.where(chosen[a] & chosen[b], float(idx), 0.0)

    n_cls_rows = carry_ref.shape[0]
    cls_row = lax.broadcasted_iota(I32, (n_cls_rows, tm), 0).astype(F32)
    onehot = (cls_row == cls).astype(F32)
    t_i = lax.broadcasted_iota(I32, (tm, tm), 0)
    t_j = lax.broadcasted_iota(I32, (tm, tm), 1)
    carry = carry_ref[...]
    before = _dot(onehot.astype(BF16), (t_i < t_j).astype(BF16)) + jnp.concatenate(
        [carry] * (tm // LANES), axis=1)
    rank = jnp.sum(onehot * before, 0, keepdims=True)
    carry_ref[...] = carry + jnp.sum(onehot, 1, keepdims=True)
    cnt_ref[...] = carry_ref[...]

    row_id = lax.broadcasted_iota(I32, meta_ref.shape, 0)
    meta_ref[...] = jnp.where(row_id == 0, cls, jnp.where(row_id == 1, rank, 0.0))


def _out_proj(y, w_out, hs, ln_g_row, ln_b_row, wr_split, rbias_row):
    t, kd = y.shape
    tm = OUT_TILE
    rows = lambda i: (i, 0)
    const = lambda i: (0, 0)
    return pl.pallas_call(
        _out_kernel,
        grid=(t // tm,),
        in_specs=[
            pl.BlockSpec((tm, kd), rows),
            pl.BlockSpec(w_out.shape, const),
            _token_spec(hs, tm, rows),
            pl.BlockSpec(ln_g_row.shape, const),
            pl.BlockSpec(ln_b_row.shape, const),
            pl.BlockSpec(wr_split.shape, const),
            pl.BlockSpec(rbias_row.shape, const),
        ],
        out_specs=[pl.BlockSpec((tm * SLAB, LANES), rows), pl.BlockSpec((SUBLANES, tm), lambda i: (0, i)),
                   pl.BlockSpec((CLASS_ROWS, LANES), const)],
        out_shape=[jax.ShapeDtypeStruct((t * SLAB, LANES), F32),
                   jax.ShapeDtypeStruct((SUBLANES, t), F32),
                   jax.ShapeDtypeStruct((CLASS_ROWS, LANES), F32)],
        scratch_shapes=[pltpu.VMEM((CLASS_ROWS, LANES), F32)],
        compiler_params=_cparams(("arbitrary",)),
        name="out_proj",
    )(y, w_out, hs, ln_g_row, ln_b_row, wr_split, rbias_row)


def _permute_kernel(pos_ref, src_ref, *refs, scatter, row_major_out):
    if row_major_out:
        o_ref, sem, dst_ref = refs[-3:]
    else:
        dst_ref, sem = refs[-2:]
    n_tok = pos_ref.shape[2]

    def issue(g, carry):
        for u in range(PERMUTE_UNROLL):
            r = g * PERMUTE_UNROLL + u
            here = pl.ds(pl.multiple_of(r * SLAB, SLAB), SLAB)
            there = pl.ds(pl.multiple_of(pos_ref[0, 0, r] * SLAB, SLAB), SLAB)
            src_rows, dst_rows = (here, there) if scatter else (there, here)
            pltpu.make_async_copy(src_ref.at[src_rows, :], dst_ref.at[dst_rows, :], sem).start(
                priority=u % 2)
        return carry

    lax.fori_loop(0, n_tok // PERMUTE_UNROLL, issue, 0)
    whole = pl.ds(0, n_tok * SLAB)
    pltpu.make_async_copy(src_ref.at[whole, :], dst_ref.at[whole, :], sem).wait()
    if row_major_out:
        o_ref[...] = _load_tokens(dst_ref, n_tok)


def _permute(pos, src, n_dst_rows, scatter, row_major_out=False):
    n_tok = pos.shape[2]
    any_spec = pl.BlockSpec(memory_space=pl.ANY)
    tile_spec = pl.BlockSpec((n_tok * SLAB, LANES), lambda i: (i, 0))
    out_shape = jax.ShapeDtypeStruct((n_dst_rows, LANES), F32)
    scratch = [pltpu.SemaphoreType.DMA(())]
    if row_major_out:
        assert not scatter
        tile_spec = pl.BlockSpec((n_tok, D_MODEL), lambda i: (i, 0))
        out_shape = jax.ShapeDtypeStruct((n_dst_rows // SLAB, D_MODEL), F32)
        scratch.append(pltpu.VMEM((n_tok * SLAB, LANES), F32))
    extra = [jnp.zeros((n_dst_rows, LANES), F32)] if scatter else []
    return pl.pallas_call(
        functools.partial(_permute_kernel, scatter=scatter, row_major_out=row_major_out),
        grid=(pos.shape[0],),
        in_specs=[pl.BlockSpec((1, 1, n_tok), lambda i: (i, 0, 0), memory_space=pltpu.SMEM),
                  tile_spec if scatter else any_spec] + [any_spec] * len(extra),
        out_specs=any_spec if scatter else tile_spec,
        out_shape=out_shape,
        scratch_shapes=scratch,
        input_output_aliases={2: 0} if scatter else {},
        compiler_params=_cparams(("arbitrary",)),
        name="dispatch" if scatter else "unpermute",
    )(pos, src, *extra)


def _moe_kernel(lo_ref, hi_ref, nv_ref, xs_ref, w1l_ref, w3l_ref, w2l_ref, w1h_ref, w3h_ref, w2h_ref,
                wrl_ref, wrh_ref, lng_ref, lnb_ref, ys_ref):
    del lo_ref, hi_ref
    tm = xs_ref.shape[0] // SLAB

    @pl.when(pl.program_id(0) < nv_ref[0])
    def _():
        x = _load_tokens(xs_ref, tm)
        xb = x.astype(BF16)

        def expert(w1_ref, w3_ref, w2_ref):
            hid = _silu(_dot(xb, w1_ref[0])) * _dot(xb, w3_ref[0])
            return _dot(hid.astype(BF16), w2_ref[0])

        s_lo = jax.nn.sigmoid(jnp.sum(x * wrl_ref[0], -1, keepdims=True))
        s_hi = jax.nn.sigmoid(jnp.sum(x * wrh_ref[0], -1, keepdims=True))
        denom = s_lo + s_hi
        m = (s_lo / denom) * expert(w1l_ref, w3l_ref, w2l_ref)
        m = m + (s_hi / denom) * expert(w1h_ref, w3h_ref, w2h_ref)
        _store_tokens(ys_ref, _layer_norm(DN_ALPHA * x + m, lng_ref[...], lnb_ref[...]), tm)

    @pl.when(pl.program_id(0) >= nv_ref[0])
    def _():
        ys_ref[...] = jnp.zeros(ys_ref.shape, F32)


def _moe(tile_lo, tile_hi, n_valid, xs, w1, w3, w2, wr_rows, ln_g_row, ln_b_row):
    tm = MOE_TILE
    n_tiles = xs.shape[0] // (tm * SLAB)
    rows = lambda i, lo, hi, nv: (jnp.minimum(i, nv[0] - 1), 0)
    w_lo = lambda i, lo, hi, nv: (lo[i], 0, 0)
    w_hi = lambda i, lo, hi, nv: (hi[i], 0, 0)
    const = lambda i, lo, hi, nv: (0, 0)
    up = (1, D_MODEL, D_EXPERT)
    down = (1, D_EXPERT, D_MODEL)
    router_row = (1, 1, D_MODEL)
    grid_spec = pltpu.PrefetchScalarGridSpec(
        num_scalar_prefetch=3,
        grid=(n_tiles,),
        in_specs=[
            pl.BlockSpec((tm * SLAB, LANES), rows),
            pl.BlockSpec(up, w_lo), pl.BlockSpec(up, w_lo), pl.BlockSpec(down, w_lo),
            pl.BlockSpec(up, w_hi), pl.BlockSpec(up, w_hi), pl.BlockSpec(down, w_hi),
            pl.BlockSpec(router_row, w_lo), pl.BlockSpec(router_row, w_hi),
            pl.BlockSpec(ln_g_row.shape, const), pl.BlockSpec(ln_b_row.shape, const),
        ],
        out_specs=pl.BlockSpec((tm * SLAB, LANES), lambda i, lo, hi, nv: (i, 0)),
    )
    return pl.pallas_call(
        _moe_kernel,
        grid_spec=grid_spec,
        out_shape=jax.ShapeDtypeStruct(xs.shape, F32),
        compiler_params=_cparams(("arbitrary",)),
        name="moe",
    )(tile_lo, tile_hi, n_valid, xs, w1, w3, w2, w1, w3, w2, wr_rows, wr_rows, ln_g_row, ln_b_row)


def _routing_plan(meta, counts, n_slots):
    cls = meta[0].astype(I32)
    rank = meta[1].astype(I32)
    cnt = counts[:N_PAIR_CLASSES, 0].astype(I32)
    padded = ((cnt + MOE_TILE - 1) // MOE_TILE) * MOE_TILE
    end = jnp.cumsum(padded)
    start = end - padded
    classes = jnp.arange(N_PAIR_CLASSES, dtype=I32)
    pos = jnp.sum(jnp.where(cls[:, None] == classes[None, :], start[None, :], 0), -1) + rank
    n_tiles = n_slots // MOE_TILE
    n_valid = (end[-1] // MOE_TILE).astype(I32).reshape(1)
    tile_start = jnp.minimum(jnp.arange(n_tiles, dtype=I32), n_valid[0] - 1) * MOE_TILE
    tile_cls = jnp.sum((tile_start[:, None] >= end[None, :]).astype(I32), -1)
    onehot = (tile_cls[:, None] == classes[None, :]).astype(I32)
    tile_lo = jnp.sum(onehot * jnp.asarray(_PAIR_LO)[None, :], -1)
    tile_hi = jnp.sum(onehot * jnp.asarray(_PAIR_HI)[None, :], -1)
    return pos, tile_lo, tile_hi, n_valid


def _moe_layer(hs, meta, counts, w1, w3, w2, wr_rows, ln_g_row, ln_b_row, row_major_out):
    t = meta.shape[1]
    n_slots = t + N_PAIR_CLASSES * MOE_TILE
    pos, tile_lo, tile_hi, n_valid = _routing_plan(meta, counts, n_slots)
    pos = pos.reshape(t // PERMUTE_TOKENS, 1, PERMUTE_TOKENS)
    xs = _permute(pos, hs, n_slots * SLAB, scatter=True)
    ys = _moe(tile_lo, tile_hi, n_valid, xs, w1, w3, w2, wr_rows, ln_g_row, ln_b_row)
    return _permute(pos, ys, t * SLAB, scatter=False, row_major_out=row_major_out)


def _row(v, width=None):
    v = v.astype(F32).reshape(1, -1)
    if width is not None and v.shape[1] < width:
        v = jnp.pad(v, ((0, 0), (0, width - v.shape[1])))
    return v


def _deltanet(h2d, batch, seq, w_in, conv_w, a_log, dt_bias, norm_g):
    qkvg = 4 * D_MODEL
    w_big = w_in[:, :qkvg].astype(BF16)
    w_ab = jnp.pad(w_in[:, qkvg:], ((0, 0), (0, LANES - 2 * A_HEADS))).astype(BF16)
    q, k, v, gate, gb = _dn_in(h2d, batch, seq, w_big, w_ab, conv_w.astype(F32),
                               _row(a_log, LANES), _row(dt_bias, LANES))
    return _dn_scan(q, k, v, gate, gb, _row(norm_g), batch, seq)


def _diff_attention(h2d, batch, seq, w_in, lam, norm_g, rel_bias, lambda_init):
    lamf = lam.astype(F32)
    lam_full = jnp.exp(jnp.sum(lamf[0] * lamf[1])) - jnp.exp(jnp.sum(lamf[2] * lamf[3])) + lambda_init
    scalars = jnp.stack([lam_full, jnp.asarray(1.0 - lambda_init, F32)]).astype(F32)
    qt, k, vt = _at_in(h2d, w_in.astype(BF16))
    return _attn(scalars, qt, k, vt, _attn_bias_table(rel_bias), _row(norm_g), batch, seq)


def _gmlp(h2d, w_in, ln_g, ln_b, w_s, b_s):
    ws_causal = jnp.tril(w_s.astype(F32)).astype(BF16)
    bs_t = jnp.pad(jnp.transpose(b_s.astype(F32)), ((0, 0), (0, LANES - C_GROUPS)))
    return _gm(h2d, w_in.astype(BF16), _row(ln_g), _row(ln_b), ws_causal, bs_t)


def kernel(x, ln_mix_g, ln_mix_b, ln_ffn_g, ln_ffn_b, a_w_in, a_conv, a_a_log, a_dt_bias, a_norm_g, a_w_out, b_w_in, b_lambda, b_norm_g, b_w_out, rel_bias, c_w_in, c_ln_g, c_ln_b, c_w_s, c_b_s, c_w_out, w_router, router_bias, e_w1, e_w3, e_w2):
    batch, seq, d = x.shape
    assert d == D_MODEL and seq % max(ROW_TILE, ATT_BLOCK, DN_CHUNK) == 0
    t = batch * seq
    h = x.astype(F32).reshape(t, d)

    wr_rows = jnp.transpose(w_router.astype(F32)).reshape(N_EXPERTS, 1, D_MODEL)
    wr = jnp.pad(jnp.transpose(w_router.astype(F32)), ((0, LANES - N_EXPERTS), (0, 0)))
    wr_hi = wr.astype(BF16)
    wr_lo = (wr - wr_hi.astype(F32)).astype(BF16)
    wr_split = jnp.concatenate([wr_hi, wr_lo], axis=0)
    rbias_row = jnp.broadcast_to(router_bias.astype(F32)[:, None], (N_EXPERTS, LANES))

    for i in range(DEPTH):
        kind, j = i % N_MIXERS, i // N_MIXERS
        if kind == 0:
            y = _deltanet(h, batch, seq, a_w_in[j], a_conv[j], a_a_log[j], a_dt_bias[j], a_norm_g[j])
            w_out = a_w_out[j]
        elif kind == 1:
            lambda_init = 0.8 - 0.6 * math.exp(-0.3 * i)
            y = _diff_attention(h, batch, seq, b_w_in[j], b_lambda[j], b_norm_g[j], rel_bias, lambda_init)
            w_out = b_w_out[j]
        else:
            y = _gmlp(h, c_w_in[j], c_ln_g[j], c_ln_b[j], c_w_s[j], c_b_s[j])
            w_out = c_w_out[j]
        hs, meta, counts = _out_proj(y, w_out.astype(BF16), h, _row(ln_mix_g[i]), _row(ln_mix_b[i]),
                                     wr_split, rbias_row)
        h = _moe_layer(hs, meta, counts, e_w1[i].astype(BF16), e_w3[i].astype(BF16),
                       e_w2[i].astype(BF16), wr_rows, _row(ln_ffn_g[i]), _row(ln_ffn_b[i]),
                       row_major_out=(i == DEPTH - 1))
    return h.reshape(batch, seq, d).astype(x.dtype)
```

```python
import functools
import math

import jax
import jax.numpy as jnp
import numpy as np
from jax import lax
from jax.experimental import pallas as pl
from jax.experimental.pallas import tpu as pltpu

F32 = jnp.float32
BF16 = jnp.bfloat16
I32 = jnp.int32

D_MODEL = 1024
DEPTH = 4
N_MIXERS = 3
A_HEADS = 8
A_DK = 128
A_CONV = 4
B_DH = 64
B_HEADS = D_MODEL // (2 * B_DH)
REL_BUCKETS = 32
REL_MAX_DIST = 128
C_CHUNK = 128
C_HALF = 2 * D_MODEL
C_GROUPS = 8
C_GW = C_HALF // C_GROUPS
N_EXPERTS = 16
N_GROUPS = 4
EXPERTS_PER_GROUP = 4
D_EXPERT = 512
DN_ALPHA = (2 * DEPTH) ** 0.25
EPS = 1e-5

LANES = 128
SUBLANES = 8
VMEM_LIMIT_BYTES = 56 * 1024 * 1024

ROW_TILE = 256
OUT_TILE = 512
DN_CHUNK = 128
DN_SEQS_PER_STEP = 2
DN_CHAINS_PER_GROUP = 16
ATT_BLOCK = 512
V_ROWS = 2 * B_DH + 16
MOE_TILE = 512
PERMUTE_TOKENS = 1024
PERMUTE_UNROLL = 8
N_PAIR_CLASSES = N_GROUPS * 6
CLASS_ROWS = 32
NEG_BIG = -1e30
LOG2E = math.log2(math.e)

_PAIR_LO = np.array([g * 4 + a for g in range(4) for a, b in ((0, 1), (0, 2), (0, 3), (1, 2), (1, 3), (2, 3))], np.int32)
_PAIR_HI = np.array([g * 4 + b for g in range(4) for a, b in ((0, 1), (0, 2), (0, 3), (1, 2), (1, 3), (2, 3))], np.int32)


def _cparams(sem):
    return pltpu.CompilerParams(dimension_semantics=sem, vmem_limit_bytes=VMEM_LIMIT_BYTES)


def _layer_norm(x, g, b):
    mu = jnp.mean(x, -1, keepdims=True)
    xc = x - mu
    var = jnp.mean(xc * xc, -1, keepdims=True)
    return xc * lax.rsqrt(var + EPS) * g + b


def _silu(x):
    return x * jax.nn.sigmoid(x)


def _dot(a, b):
    return jnp.dot(a, b, preferred_element_type=F32)


def _dot_nt(a, b):
    return lax.dot_general(a, b, (((1,), (1,)), ((), ())), preferred_element_type=F32)


SLAB = D_MODEL // LANES


def _load_tokens(ref, n_tok):
    if ref.shape[-1] == D_MODEL:
        return ref[...]
    return jnp.concatenate([ref[pl.ds(s, n_tok, stride=SLAB), :] for s in range(SLAB)], axis=1)


def _token_spec(arr, n_tok, index_map):
    if arr.shape[-1] == D_MODEL:
        return pl.BlockSpec((n_tok, D_MODEL), index_map)
    return pl.BlockSpec((n_tok * SLAB, LANES), index_map)


def _store_tokens(ref, val, n_tok):
    for s in range(SLAB):
        ref[pl.ds(s, n_tok, stride=SLAB), :] = val[:, s * LANES:(s + 1) * LANES]


def _split3(x):
    hi = x.astype(BF16)
    r = x - hi.astype(F32)
    mid = r.astype(BF16)
    lo = (r - mid.astype(F32)).astype(BF16)
    return hi, mid, lo


def _dn_in_kernel(x_ref, w_ref, wab_ref, conv_ref, alog_ref, dtb_ref,
                  q_ref, k_ref, v_ref, gate_ref, gb_ref, ext_ref):
    ts = q_ref.shape[0]
    qkv_w = 3 * D_MODEL
    n_slabs = qkv_w // LANES
    s_idx = pl.program_id(1)

    @pl.when(s_idx == 0)
    def _():
        ext_ref[:, 0:SUBLANES, :] = jnp.zeros((n_slabs, SUBLANES, LANES), F32)

    @pl.when(s_idx > 0)
    def _():
        ext_ref[:, 0:SUBLANES, :] = ext_ref[:, ts:ts + SUBLANES, :]

    xb = _load_tokens(x_ref, ts).astype(BF16)
    gate_ref[...] = _dot(xb, w_ref[:, qkv_w:qkv_w + D_MODEL]).astype(BF16)
    cw = conv_ref[...]
    for part, out_ref in enumerate((q_ref, k_ref, v_ref)):
        pre = _dot(xb, w_ref[:, part * D_MODEL:(part + 1) * D_MODEL])
        for h in range(A_HEADS):
            slab = part * A_HEADS + h
            lo = h * A_DK
            ext_ref[slab, SUBLANES:SUBLANES + ts, :] = pre[:, lo:lo + A_DK]
            col = slice(slab * LANES, (slab + 1) * LANES)
            y = ext_ref[slab, pl.ds(SUBLANES - 3, ts), :] * cw[0:1, col]
            for j in range(1, A_CONV):
                y = y + ext_ref[slab, pl.ds(SUBLANES - 3 + j, ts), :] * cw[j:j + 1, col]
            y = _silu(y)
            if part == 0:
                y = y * (lax.rsqrt(jnp.sum(y * y, -1, keepdims=True) + 1e-6) * (A_DK ** -0.5))
            elif part == 1:
                y = y * lax.rsqrt(jnp.sum(y * y, -1, keepdims=True) + 1e-6)
            out_ref[:, lo:lo + A_DK] = y.astype(BF16)

    ab = _dot(xb, wab_ref[...])
    z = ab + dtb_ref[...]
    softplus = jnp.maximum(z, 0.0) + jnp.log1p(jnp.exp(-jnp.abs(z)))
    g = -jnp.exp(alog_ref[...]) * softplus
    lane = lax.broadcasted_iota(I32, ab.shape, 1)
    gb_ref[...] = jnp.where(lane < A_HEADS, g, jax.nn.sigmoid(ab))


def _dn_in(x2d, batch, seq, w_big, w_ab, conv_w, alog_row, dtb_row):
    ts = ROW_TILE
    ns = seq // ts
    rows = lambda b, s: (b * ns + s, 0)
    const = lambda b, s: (0, 0)
    t = batch * seq
    out_bf = jax.ShapeDtypeStruct((t, D_MODEL), BF16)
    return pl.pallas_call(
        _dn_in_kernel,
        grid=(batch, ns),
        in_specs=[
            _token_spec(x2d, ts, rows),
            pl.BlockSpec(w_big.shape, const),
            pl.BlockSpec(w_ab.shape, const),
            pl.BlockSpec(conv_w.shape, const),
            pl.BlockSpec(alog_row.shape, const),
            pl.BlockSpec(dtb_row.shape, const),
        ],
        out_specs=[pl.BlockSpec((ts, D_MODEL), rows)] * 4 + [pl.BlockSpec((ts, LANES), rows)],
        out_shape=[out_bf, out_bf, out_bf, out_bf, jax.ShapeDtypeStruct((t, LANES), F32)],
        scratch_shapes=[pltpu.VMEM((3 * D_MODEL // LANES, ts + 2 * SUBLANES, LANES), F32)],
        compiler_params=_cparams(("arbitrary", "arbitrary")),
        name="dn_in",
    )(x2d, w_big, w_ab, conv_w, alog_row, dtb_row)


def _dn_scan_kernel(q_ref, k_ref, v_ref, gate_ref, gb_ref, ng_ref, o_ref, state_ref):
    n_seq, c = q_ref.shape[0], q_ref.shape[1]
    n_doublings = int(math.log2(c)) - 1

    @pl.when(pl.program_id(1) == 0)
    def _():
        state_ref[...] = jnp.zeros(state_ref.shape, F32)

    row = lax.broadcasted_iota(I32, (c, c), 0)
    col = lax.broadcasted_iota(I32, (c, c), 1)
    causal = row >= col
    strict = row > col
    eye = (row == col).astype(F32)

    tril = causal.astype(BF16)
    gbs = [gb_ref[s] for s in range(n_seq)]
    gc_alls, gc_all_ts = [], []
    for gb_s in gbs:
        parts = _split3(gb_s)
        gc = _dot(tril, parts[0]) + _dot(tril, parts[1]) + _dot(tril, parts[2])
        gc_alls.append(gc)
        gc_all_ts.append(gc.T)
    ng = ng_ref[...]

    def chain_group(chains):
        heads = range(len(chains))
        seqs = [s for s, _ in chains]
        hid = [g for _, g in chains]
        lanes = [slice(g * A_DK, (g + 1) * A_DK) for g in hid]
        qh = [q_ref[seqs[h], :, lanes[h]] for h in heads]
        kh = [k_ref[seqs[h], :, lanes[h]] for h in heads]
        gc_col = [gc_alls[seqs[h]][:, hid[h]:hid[h] + 1] for h in heads]
        beta = [gbs[seqs[h]][:, A_HEADS + hid[h]:A_HEADS + hid[h] + 1] for h in heads]
        g_last = [gc_alls[seqs[h]][c - 1:c, hid[h]:hid[h] + 1] for h in heads]

        kq = [_dot_nt(jnp.concatenate([kh[h], qh[h]], axis=0), kh[h]) for h in heads]
        decay = [jnp.where(causal, jnp.exp(jnp.where(
            causal, gc_col[h] - gc_all_ts[seqs[h]][hid[h]:hid[h] + 1, :], 0.0)), 0.0) for h in heads]
        attn = [(kq[h][c:2 * c] * decay[h]).astype(BF16) for h in heads]

        p = [jnp.where(strict, -(kq[h][0:c] * beta[h] * decay[h]), 0.0) for h in heads]
        t_inv = [eye + p[h] for h in heads]
        for _ in range(n_doublings):
            pb = [p[h].astype(BF16) for h in heads]
            p = [_dot(pb[h], pb[h]) for h in heads]
            t_inv = [t_inv[h] + _dot(t_inv[h].astype(BF16), p[h].astype(BF16)) for h in heads]

        e_gc = [jnp.exp(gc_col[h]) for h in heads]
        khf = [kh[h].astype(F32) for h in heads]
        rhs = [jnp.concatenate([v_ref[seqs[h], :, lanes[h]].astype(F32) * beta[h],
                                khf[h] * (beta[h] * e_gc[h])], axis=1).astype(BF16)
               for h in heads]
        sol = [_dot(t_inv[h].astype(BF16), rhs[h]) for h in heads]

        slot = [seqs[h] * A_HEADS + hid[h] for h in heads]
        state = [state_ref[slot[h]] for h in heads]
        lhs = [jnp.concatenate([sol[h][:, A_DK:2 * A_DK], qh[h].astype(F32) * e_gc[h]],
                               axis=0).astype(BF16) for h in heads]
        ws = [_dot(lhs[h], state[h].astype(BF16)) for h in heads]
        v_new = [(sol[h][:, 0:A_DK] - ws[h][0:c]).astype(BF16) for h in heads]
        k_dec_t = [(khf[h] * jnp.exp(g_last[h] - gc_col[h])).T.astype(BF16) for h in heads]
        o = [ws[h][c:2 * c] + _dot(attn[h], v_new[h]) for h in heads]
        for h in heads:
            state_ref[slot[h]] = state[h] * jnp.exp(g_last[h]) + _dot(k_dec_t[h], v_new[h])
        for h in heads:
            on = o[h] * lax.rsqrt(jnp.mean(o[h] * o[h], -1, keepdims=True) + EPS) * ng
            gate = gate_ref[seqs[h], :, lanes[h]].astype(F32)
            o_ref[seqs[h], :, lanes[h]] = (on * _silu(gate)).astype(BF16)

    chains = [(s, g) for s in range(n_seq) for g in range(A_HEADS)]
    for first in range(0, len(chains), DN_CHAINS_PER_GROUP):
        chain_group(chains[first:first + DN_CHAINS_PER_GROUP])


def _dn_scan(q, k, v, gate, gb, norm_g_row, batch, seq):
    c = DN_CHUNK
    nc = seq // c
    n_seq = DN_SEQS_PER_STEP
    assert batch % n_seq == 0
    rows = lambda b, n: (b, n, 0)
    const = lambda b, n: (0, 0)
    wide = lambda a: a.reshape(batch, seq, a.shape[-1])
    out = pl.pallas_call(
        _dn_scan_kernel,
        grid=(batch // n_seq, nc),
        in_specs=[pl.BlockSpec((n_seq, c, D_MODEL), rows)] * 4
        + [pl.BlockSpec((n_seq, c, LANES), rows), pl.BlockSpec(norm_g_row.shape, const)],
        out_specs=pl.BlockSpec((n_seq, c, D_MODEL), rows),
        out_shape=jax.ShapeDtypeStruct((batch, seq, D_MODEL), BF16),
        scratch_shapes=[pltpu.VMEM((n_seq * A_HEADS, A_DK, A_DK), F32)],
        compiler_params=_cparams(("arbitrary", "arbitrary")),
        name="dn_scan",
    )(wide(q), wide(k), wide(v), wide(gate), wide(gb), norm_g_row)
    return out.reshape(batch * seq, D_MODEL)


def _at_in_kernel(x_ref, w_ref, qt_ref, k_ref, vt_ref):
    tm = k_ref.shape[1]
    xb = _load_tokens(x_ref, tm).astype(BF16)
    q = _dot(xb, w_ref[:, 0:D_MODEL]) * (B_DH ** -0.5 * LOG2E)
    lane = lax.broadcasted_iota(I32, q.shape, 1)
    first_map = (lane % (2 * B_DH)) < B_DH
    qt_ref[0] = jnp.where(first_map, q, 0.0).T.astype(BF16)
    qt_ref[1] = jnp.where(first_map, 0.0, q).T.astype(BF16)
    k = _dot(xb, w_ref[:, D_MODEL:2 * D_MODEL]).astype(BF16)
    vt = _dot(xb, w_ref[:, 2 * D_MODEL:3 * D_MODEL]).T.astype(BF16)
    ones = jnp.ones((V_ROWS - 2 * B_DH, tm), BF16)
    for h in range(B_HEADS):
        k_ref[h] = k[:, h * LANES:(h + 1) * LANES]
        vt_ref[h] = jnp.concatenate([vt[h * LANES:(h + 1) * LANES], ones], axis=0)


def _at_in(x2d, w_in):
    t = x2d.shape[0] // SLAB
    tm = ROW_TILE
    return pl.pallas_call(
        _at_in_kernel,
        grid=(t // tm,),
        in_specs=[pl.BlockSpec((tm * SLAB, LANES), lambda i: (i, 0)), pl.BlockSpec(w_in.shape, lambda i: (0, 0))],
        out_specs=[pl.BlockSpec((2, D_MODEL, tm), lambda i: (0, 0, i)),
                   pl.BlockSpec((B_HEADS, tm, LANES), lambda i: (0, i, 0)),
                   pl.BlockSpec((B_HEADS, V_ROWS, tm), lambda i: (0, 0, i))],
        out_shape=[jax.ShapeDtypeStruct((2, D_MODEL, t), BF16),
                   jax.ShapeDtypeStruct((B_HEADS, t, LANES), BF16),
                   jax.ShapeDtypeStruct((B_HEADS, V_ROWS, t), BF16)],
        compiler_params=_cparams(("parallel",)),
        name="at_in",
    )(x2d, w_in)


def _attn_kernel(scal_ref, q0_ref, q1_ref, q0n_ref, q1n_ref, k_ref, vt_ref, bias_ref, ng_ref, o_ref,
                 m_ref, acc_ref, sa_ref, sb_ref, sc_ref):
    tq = o_ref.shape[0]
    tk = tq
    i = pl.program_id(2)
    lam = scal_ref[0]
    out_scale = scal_ref[1]

    m_ref[...] = jnp.full(m_ref.shape, NEG_BIG, F32)
    acc_ref[...] = jnp.zeros(acc_ref.shape, F32)
    qts = (q0_ref[0], q1_ref[0])

    maps = range(2)

    def scores(j, s_ref, q_maps=qts):
        kb = k_ref[0, pl.ds(pl.multiple_of(j * tk, tk), tk), :]
        for mp in maps:
            s_ref[mp] = _dot(kb, q_maps[mp])

    def absorb(j, s_ref, bias):
        vtb = vt_ref[0, :, pl.ds(pl.multiple_of(j * tk, tk), tk)]
        s = [s_ref[mp] for mp in maps]
        if bias is not None:
            s = [x + bias for x in s]
        m_old = [m_ref[mp] for mp in maps]
        m_new = [jnp.maximum(m_old[mp], jnp.max(s[mp], 0, keepdims=True)) for mp in maps]
        alpha = [jnp.exp2(m_old[mp] - m_new[mp]) for mp in maps]
        p = [jnp.exp2(s[mp] - m_new[mp]) for mp in maps]
        pv = [_dot(vtb, p[mp].astype(BF16)) for mp in maps]
        for mp in maps:
            acc_ref[mp] = alpha[mp] * acc_ref[mp] + pv[mp]
            m_ref[mp] = m_new[mp]

    @pl.when(i == 0)
    def _():
        scores(0, sa_ref)

    @pl.when((i % 2 == 0) & (i >= 2))
    def _():
        scores(1, sb_ref)
        absorb(0, sc_ref, None)

    first = jnp.where(i % 2 == 1, 0, 1)

    def far_pair(jj, carry):
        j = first + 2 * jj
        scores(j + 1, sa_ref)
        absorb(j, sb_ref, None)
        scores(j + 2, sb_ref)
        absorb(j + 1, sa_ref, None)
        return carry

    lax.fori_loop(0, jnp.maximum(i - 1, 0) // 2, far_pair, 0)

    @pl.when(i >= 1)
    def _():
        scores(i, sa_ref)
        absorb(i - 1, sb_ref, bias_ref[0, 1])

    last = i == pl.num_programs(2) - 1
    next_qts = (q0n_ref[0], q1n_ref[0])

    @pl.when(last)
    def _():
        absorb(i, sa_ref, bias_ref[0, 0])

    @pl.when(jnp.logical_not(last) & (i % 2 == 0))
    def _():
        scores(0, sb_ref, next_qts)
        absorb(i, sa_ref, bias_ref[0, 0])

    @pl.when(jnp.logical_not(last) & (i % 2 == 1))
    def _():
        scores(0, sc_ref, next_qts)
        absorb(i, sa_ref, bias_ref[0, 0])

    dv = 2 * B_DH
    o_t = (acc_ref[0, 0:dv] / acc_ref[0, dv:dv + 1]
           - lam * (acc_ref[1, 0:dv] / acc_ref[1, dv:dv + 1]))
    o = o_t.T
    o = o * lax.rsqrt(jnp.mean(o * o, -1, keepdims=True) + EPS) * ng_ref[...] * out_scale
    o_ref[...] = o.astype(BF16)


def _attn(scalars, qt, k, vt, bias_tab, norm_g_row, batch, seq):
    tq = ATT_BLOCK
    nq = seq // tq
    t = batch * seq
    return pl.pallas_call(
        _attn_kernel,
        grid=(batch, B_HEADS, nq),
        in_specs=[
            pl.BlockSpec(memory_space=pltpu.SMEM),
            pl.BlockSpec((1, LANES, tq), lambda b, h, i: (0, h, b * nq + i)),
            pl.BlockSpec((1, LANES, tq), lambda b, h, i: (1, h, b * nq + i)),
            pl.BlockSpec((1, LANES, tq), lambda b, h, i: (0, h, b * nq + jnp.minimum(i + 1, nq - 1))),
            pl.BlockSpec((1, LANES, tq), lambda b, h, i: (1, h, b * nq + jnp.minimum(i + 1, nq - 1))),
            pl.BlockSpec((1, seq, LANES), lambda b, h, i: (h, b, 0)),
            pl.BlockSpec((1, V_ROWS, seq), lambda b, h, i: (h, 0, b)),
            pl.BlockSpec((1, 2, tq, tq), lambda b, h, i: (h, 0, 0, 0)),
            pl.BlockSpec(norm_g_row.shape, lambda b, h, i: (0, 0)),
        ],
        out_specs=pl.BlockSpec((tq, LANES), lambda b, h, i: (b * nq + i, h)),
        out_shape=jax.ShapeDtypeStruct((t, D_MODEL), BF16),
        scratch_shapes=[pltpu.VMEM((2, 1, tq), F32), pltpu.VMEM((2, V_ROWS, tq), F32),
                        pltpu.VMEM((2, tq, tq), F32), pltpu.VMEM((2, tq, tq), F32),
                        pltpu.VMEM((2, tq, tq), F32)],
        compiler_params=_cparams(("parallel", "parallel", "arbitrary")),
        name="attn",
    )(scalars, qt, qt, qt, qt, k, vt, bias_tab, norm_g_row)


def _rel_bucket(rel):
    n = jnp.maximum(rel, 0)
    max_exact = REL_BUCKETS // 2
    large = max_exact + (jnp.log(jnp.maximum(n, 1).astype(F32) / max_exact)
                         / math.log(REL_MAX_DIST / max_exact) * (REL_BUCKETS - max_exact)).astype(I32)
    large = jnp.minimum(large, REL_BUCKETS - 1)
    return jnp.where(n < max_exact, n, large)


def _attn_bias_table(rel_bias):
    tq = ATT_BLOCK
    assert tq + 1 >= REL_MAX_DIST, "blocks two back must fall in the last bucket"
    table = (rel_bias.astype(F32) - rel_bias.astype(F32)[REL_BUCKETS - 1]) * LOG2E
    period = 2 * tq
    idx = jnp.arange(period)
    diag_vec = jnp.where((idx < tq)[:, None], table[_rel_bucket(jnp.minimum(idx, tq - 1))], NEG_BIG)
    near_vec = table[_rel_bucket(jnp.where(idx < tq, idx + tq, idx - tq))]
    vecs = jnp.transpose(jnp.stack([diag_vec, near_vec], 0), (2, 0, 1))
    flat = jnp.tile(vecs, (1, 1, tq))[:, :, :tq * (period - 1)]
    return flat.reshape(B_HEADS, 2, tq, period - 1)[:, :, :, :tq]


def _gelu(x):
    return 0.5 * x * (1.0 + lax.erf(x * (2.0 ** -0.5)))


def _gm_kernel(x_ref, w_ref, lng_ref, lnb_ref, ws_ref, bs_ref, o_ref):
    tm = o_ref.shape[0]
    xb = _load_tokens(x_ref, tm).astype(BF16)
    v = _gelu(_dot(xb, w_ref[:, C_HALF:2 * C_HALF]))
    vn = _layer_norm(v, lng_ref[...], lnb_ref[...]).astype(BF16)
    bs = bs_ref[...]
    for g in range(C_GROUPS):
        lo = g * C_GW
        u = _gelu(_dot(xb, w_ref[:, lo:lo + C_GW]))
        wg = ws_ref[g]
        for ch in range(tm // C_CHUNK):
            r0 = ch * C_CHUNK
            mixed = _dot(wg, vn[r0:r0 + C_CHUNK, lo:lo + C_GW]) + bs[:, g:g + 1]
            o_ref[r0:r0 + C_CHUNK, lo:lo + C_GW] = (u[r0:r0 + C_CHUNK] * mixed).astype(BF16)


def _gm(x2d, w_in, ln_g_row, ln_b_row, ws_causal, bs_t):
    t = x2d.shape[0] // SLAB
    tm = ROW_TILE
    rows = lambda i: (i, 0)
    const2 = lambda i: (0, 0)
    return pl.pallas_call(
        _gm_kernel,
        grid=(t // tm,),
        in_specs=[
            pl.BlockSpec((tm * SLAB, LANES), rows),
            pl.BlockSpec(w_in.shape, const2),
            pl.BlockSpec(ln_g_row.shape, const2),
            pl.BlockSpec(ln_b_row.shape, const2),
            pl.BlockSpec(ws_causal.shape, lambda i: (0, 0, 0)),
            pl.BlockSpec(bs_t.shape, const2),
        ],
        out_specs=pl.BlockSpec((tm, C_HALF), rows),
        out_shape=jax.ShapeDtypeStruct((t, C_HALF), BF16),
        compiler_params=_cparams(("parallel",)),
        name="gmlp",
    )(x2d, w_in, ln_g_row, ln_b_row, ws_causal, bs_t)


_PAIRS = ((0, 1), (0, 2), (0, 3), (1, 2), (1, 3), (2, 3))


def _out_kernel(y_ref, w_ref, h_ref, lng_ref, lnb_ref, wr_ref, rb_ref, hs_ref, meta_ref, cnt_ref,
                carry_ref):
    tm = y_ref.shape[0]

    @pl.when(pl.program_id(0) == 0)
    def _():
        carry_ref[...] = jnp.zeros(carry_ref.shape, F32)

    hh = DN_ALPHA * _load_tokens(h_ref, tm) + _dot(y_ref[...], w_ref[...])
    h1 = _layer_norm(hh, lng_ref[...], lnb_ref[...])
    _store_tokens(hs_ref, h1, tm)

    h_hi = h1.astype(BF16)
    h_lo = (h1 - h_hi.astype(F32)).astype(BF16)
    both = _dot_nt(wr_ref[...], h_hi)
    logits = both[0:N_EXPERTS] + both[LANES:LANES + N_EXPERTS] + _dot_nt(wr_ref[0:N_EXPERTS, :], h_lo)
    sel = jax.nn.sigmoid(logits) + jnp.concatenate([rb_ref[...]] * (tm // LANES), axis=1)
    rows = [sel[e:e + 1, :] for e in range(N_EXPERTS)]

    def top2_sum(v):
        sums = [v[a] + v[b] for a, b in _PAIRS]
        return functools.reduce(jnp.maximum, sums)

    grp = [top2_sum(rows[g * EXPERTS_PER_GROUP:(g + 1) * EXPERTS_PER_GROUP]) for g in range(N_GROUPS)]
    best, best_val = jnp.zeros_like(grp[0]), grp[0]
    for g in range(1, N_GROUPS):
        better = grp[g] > best_val
        best = jnp.where(better, float(g), best)
        best_val = jnp.where(better, grp[g], best_val)
    member = []
    for k in range(EXPERTS_PER_GROUP):
        v = rows[k]
        for g in range(1, N_GROUPS):
            v = jnp.where(best == float(g), rows[g * EXPERTS_PER_GROUP + k], v)
        member.append(v)
    chosen = []
    for k in range(EXPERTS_PER_GROUP):
        beaten = jnp.zeros_like(best)
        for j in range(EXPERTS_PER_GROUP):
            if j != k:
                ahead = (member[j] >= member[k]) if j < k else (member[j] > member[k])
                beaten = beaten + ahead.astype(F32)
        chosen.append(beaten < 2.0)
    cls = best * float(len(_PAIRS))
    for idx, (a, b) in enumerate(_PAIRS):
        cls = cls + jnp.where(chosen[a] & chosen[b], float(idx), 0.0)

    n_cls_rows = carry_ref.shape[0]
    cls_row = lax.broadcasted_iota(I32, (n_cls_rows, tm), 0).astype(F32)
    onehot = (cls_row == cls).astype(F32)
    t_i = lax.broadcasted_iota(I32, (tm, tm), 0)
    t_j = lax.broadcasted_iota(I32, (tm, tm), 1)
    carry = carry_ref[...]
    before = _dot(onehot.astype(BF16), (t_i < t_j).astype(BF16)) + jnp.concatenate(
        [carry] * (tm // LANES), axis=1)
    rank = jnp.sum(onehot * before, 0, keepdims=True)
    carry_ref[...] = carry + jnp.sum(onehot, 1, keepdims=True)
    cnt_ref[...] = carry_ref[...]

    row_id = lax.broadcasted_iota(I32, meta_ref.shape, 0)
    meta_ref[...] = jnp.where(row_id == 0, cls, jnp.where(row_id == 1, rank, 0.0))


def _out_proj(y, w_out, hs, ln_g_row, ln_b_row, wr_split, rbias_row):
    t, kd = y.shape
    tm = OUT_TILE
    rows = lambda i: (i, 0)
    const = lambda i: (0, 0)
    return pl.pallas_call(
        _out_kernel,
        grid=(t // tm,),
        in_specs=[
            pl.BlockSpec((tm, kd), rows),
            pl.BlockSpec(w_out.shape, const),
            _token_spec(hs, tm, rows),
            pl.BlockSpec(ln_g_row.shape, const),
            pl.BlockSpec(ln_b_row.shape, const),
            pl.BlockSpec(wr_split.shape, const),
            pl.BlockSpec(rbias_row.shape, const),
        ],
        out_specs=[pl.BlockSpec((tm * SLAB, LANES), rows), pl.BlockSpec((SUBLANES, tm), lambda i: (0, i)),
                   pl.BlockSpec((CLASS_ROWS, LANES), const)],
        out_shape=[jax.ShapeDtypeStruct((t * SLAB, LANES), F32),
                   jax.ShapeDtypeStruct((SUBLANES, t), F32),
                   jax.ShapeDtypeStruct((CLASS_ROWS, LANES), F32)],
        scratch_shapes=[pltpu.VMEM((CLASS_ROWS, LANES), F32)],
        compiler_params=_cparams(("arbitrary",)),
        name="out_proj",
    )(y, w_out, hs, ln_g_row, ln_b_row, wr_split, rbias_row)


def _permute_kernel(pos_ref, src_ref, *refs, scatter, row_major_out):
    if row_major_out:
        o_ref, sem, dst_ref = refs[-3:]
    else:
        dst_ref, sem = refs[-2:]
    n_tok = pos_ref.shape[2]

    def issue(g, carry):
        for u in range(PERMUTE_UNROLL):
            r = g * PERMUTE_UNROLL + u
            here = pl.ds(pl.multiple_of(r * SLAB, SLAB), SLAB)
            there = pl.ds(pl.multiple_of(pos_ref[0, 0, r] * SLAB, SLAB), SLAB)
            src_rows, dst_rows = (here, there) if scatter else (there, here)
            pltpu.make_async_copy(src_ref.at[src_rows, :], dst_ref.at[dst_rows, :], sem).start(
                priority=u % 2)
        return carry

    lax.fori_loop(0, n_tok // PERMUTE_UNROLL, issue, 0)
    whole = pl.ds(0, n_tok * SLAB)
    pltpu.make_async_copy(src_ref.at[whole, :], dst_ref.at[whole, :], sem).wait()
    if row_major_out:
        o_ref[...] = _load_tokens(dst_ref, n_tok)


def _permute(pos, src, n_dst_rows, scatter, row_major_out=False):
    n_tok = pos.shape[2]
    any_spec = pl.BlockSpec(memory_space=pl.ANY)
    tile_spec = pl.BlockSpec((n_tok * SLAB, LANES), lambda i: (i, 0))
    out_shape = jax.ShapeDtypeStruct((n_dst_rows, LANES), F32)
    scratch = [pltpu.SemaphoreType.DMA(())]
    if row_major_out:
        assert not scatter
        tile_spec = pl.BlockSpec((n_tok, D_MODEL), lambda i: (i, 0))
        out_shape = jax.ShapeDtypeStruct((n_dst_rows // SLAB, D_MODEL), F32)
        scratch.append(pltpu.VMEM((n_tok * SLAB, LANES), F32))
    extra = [jnp.zeros((n_dst_rows, LANES), F32)] if scatter else []
    return pl.pallas_call(
        functools.partial(_permute_kernel, scatter=scatter, row_major_out=row_major_out),
        grid=(pos.shape[0],),
        in_specs=[pl.BlockSpec((1, 1, n_tok), lambda i: (i, 0, 0), memory_space=pltpu.SMEM),
                  tile_spec if scatter else any_spec] + [any_spec] * len(extra),
        out_specs=any_spec if scatter else tile_spec,
        out_shape=out_shape,
        scratch_shapes=scratch,
        input_output_aliases={2: 0} if scatter else {},
        compiler_params=_cparams(("arbitrary",)),
        name="dispatch" if scatter else "unpermute",
    )(pos, src, *extra)


def _moe_kernel(lo_ref, hi_ref, nv_ref, xs_ref, w1l_ref, w3l_ref, w2l_ref, w1h_ref, w3h_ref, w2h_ref,
                wrl_ref, wrh_ref, lng_ref, lnb_ref, ys_ref):
    del lo_ref, hi_ref
    tm = xs_ref.shape[0] // SLAB

    @pl.when(pl.program_id(0) < nv_ref[0])
    def _():
        x = _load_tokens(xs_ref, tm)
        xb = x.astype(BF16)

        def expert(w1_ref, w3_ref, w2_ref):
            hid = _silu(_dot(xb, w1_ref[0])) * _dot(xb, w3_ref[0])
            return _dot(hid.astype(BF16), w2_ref[0])

        s_lo = jax.nn.sigmoid(jnp.sum(x * wrl_ref[0], -1, keepdims=True))
        s_hi = jax.nn.sigmoid(jnp.sum(x * wrh_ref[0], -1, keepdims=True))
        denom = s_lo + s_hi
        m = (s_lo / denom) * expert(w1l_ref, w3l_ref, w2l_ref)
        m = m + (s_hi / denom) * expert(w1h_ref, w3h_ref, w2h_ref)
        _store_tokens(ys_ref, _layer_norm(DN_ALPHA * x + m, lng_ref[...], lnb_ref[...]), tm)

    @pl.when(pl.program_id(0) >= nv_ref[0])
    def _():
        ys_ref[...] = jnp.zeros(ys_ref.shape, F32)


def _moe(tile_lo, tile_hi, n_valid, xs, w1, w3, w2, wr_rows, ln_g_row, ln_b_row):
    tm = MOE_TILE
    n_tiles = xs.shape[0] // (tm * SLAB)
    rows = lambda i, lo, hi, nv: (jnp.minimum(i, nv[0] - 1), 0)
    w_lo = lambda i, lo, hi, nv: (lo[i], 0, 0)
    w_hi = lambda i, lo, hi, nv: (hi[i], 0, 0)
    const = lambda i, lo, hi, nv: (0, 0)
    up = (1, D_MODEL, D_EXPERT)
    down = (1, D_EXPERT, D_MODEL)
    router_row = (1, 1, D_MODEL)
    grid_spec = pltpu.PrefetchScalarGridSpec(
        num_scalar_prefetch=3,
        grid=(n_tiles,),
        in_specs=[
            pl.BlockSpec((tm * SLAB, LANES), rows),
            pl.BlockSpec(up, w_lo), pl.BlockSpec(up, w_lo), pl.BlockSpec(down, w_lo),
            pl.BlockSpec(up, w_hi), pl.BlockSpec(up, w_hi), pl.BlockSpec(down, w_hi),
            pl.BlockSpec(router_row, w_lo), pl.BlockSpec(router_row, w_hi),
            pl.BlockSpec(ln_g_row.shape, const), pl.BlockSpec(ln_b_row.shape, const),
        ],
        out_specs=pl.BlockSpec((tm * SLAB, LANES), lambda i, lo, hi, nv: (i, 0)),
    )
    return pl.pallas_call(
        _moe_kernel,
        grid_spec=grid_spec,
        out_shape=jax.ShapeDtypeStruct(xs.shape, F32),
        compiler_params=_cparams(("arbitrary",)),
        name="moe",
    )(tile_lo, tile_hi, n_valid, xs, w1, w3, w2, w1, w3, w2, wr_rows, wr_rows, ln_g_row, ln_b_row)


def _routing_plan(meta, counts, n_slots):
    cls = meta[0].astype(I32)
    rank = meta[1].astype(I32)
    cnt = counts[:N_PAIR_CLASSES, 0].astype(I32)
    padded = ((cnt + MOE_TILE - 1) // MOE_TILE) * MOE_TILE
    end = jnp.cumsum(padded)
    start = end - padded
    classes = jnp.arange(N_PAIR_CLASSES, dtype=I32)
    pos = jnp.sum(jnp.where(cls[:, None] == classes[None, :], start[None, :], 0), -1) + rank
    n_tiles = n_slots // MOE_TILE
    n_valid = (end[-1] // MOE_TILE).astype(I32).reshape(1)
    tile_start = jnp.minimum(jnp.arange(n_tiles, dtype=I32), n_valid[0] - 1) * MOE_TILE
    tile_cls = jnp.sum((tile_start[:, None] >= end[None, :]).astype(I32), -1)
    onehot = (tile_cls[:, None] == classes[None, :]).astype(I32)
    tile_lo = jnp.sum(onehot * jnp.asarray(_PAIR_LO)[None, :], -1)
    tile_hi = jnp.sum(onehot * jnp.asarray(_PAIR_HI)[None, :], -1)
    return pos, tile_lo, tile_hi, n_valid


def _moe_layer(hs, meta, counts, w1, w3, w2, wr_rows, ln_g_row, ln_b_row, row_major_out):
    t = meta.shape[1]
    n_slots = t + N_PAIR_CLASSES * MOE_TILE
    pos, tile_lo, tile_hi, n_valid = _routing_plan(meta, counts, n_slots)
    pos = pos.reshape(t // PERMUTE_TOKENS, 1, PERMUTE_TOKENS)
    xs = _permute(pos, hs, n_slots * SLAB, scatter=True)
    ys = _moe(tile_lo, tile_hi, n_valid, xs, w1, w3, w2, wr_rows, ln_g_row, ln_b_row)
    return _permute(pos, ys, t * SLAB, scatter=False, row_major_out=row_major_out)


def _row(v, width=None):
    v = v.astype(F32).reshape(1, -1)
    if width is not None and v.shape[1] < width:
        v = jnp.pad(v, ((0, 0), (0, width - v.shape[1])))
    return v


def _deltanet(h2d, batch, seq, w_in, conv_w, a_log, dt_bias, norm_g):
    qkvg = 4 * D_MODEL
    w_big = w_in[:, :qkvg].astype(BF16)
    w_ab = jnp.pad(w_in[:, qkvg:], ((0, 0), (0, LANES - 2 * A_HEADS))).astype(BF16)
    q, k, v, gate, gb = _dn_in(h2d, batch, seq, w_big, w_ab, conv_w.astype(F32),
                               _row(a_log, LANES), _row(dt_bias, LANES))
    return _dn_scan(q, k, v, gate, gb, _row(norm_g), batch, seq)


def _diff_attention(h2d, batch, seq, w_in, lam, norm_g, rel_bias, lambda_init):
    lamf = lam.astype(F32)
    lam_full = jnp.exp(jnp.sum(lamf[0] * lamf[1])) - jnp.exp(jnp.sum(lamf[2] * lamf[3])) + lambda_init
    scalars = jnp.stack([lam_full, jnp.asarray(1.0 - lambda_init, F32)]).astype(F32)
    qt, k, vt = _at_in(h2d, w_in.astype(BF16))
    return _attn(scalars, qt, k, vt, _attn_bias_table(rel_bias), _row(norm_g), batch, seq)


def _gmlp(h2d, w_in, ln_g, ln_b, w_s, b_s):
    ws_causal = jnp.tril(w_s.astype(F32)).astype(BF16)
    bs_t = jnp.pad(jnp.transpose(b_s.astype(F32)), ((0, 0), (0, LANES - C_GROUPS)))
    return _gm(h2d, w_in.astype(BF16), _row(ln_g), _row(ln_b), ws_causal, bs_t)


def kernel(x, ln_mix_g, ln_mix_b, ln_ffn_g, ln_ffn_b, a_w_in, a_conv, a_a_log, a_dt_bias, a_norm_g, a_w_out, b_w_in, b_lambda, b_norm_g, b_w_out, rel_bias, c_w_in, c_ln_g, c_ln_b, c_w_s, c_b_s, c_w_out, w_router, router_bias, e_w1, e_w3, e_w2):
    batch, seq, d = x.shape
    assert d == D_MODEL and seq % max(ROW_TILE, ATT_BLOCK, DN_CHUNK) == 0
    t = batch * seq
    h = x.astype(F32).reshape(t, d)

    wr_rows = jnp.transpose(w_router.astype(F32)).reshape(N_EXPERTS, 1, D_MODEL)
    wr = jnp.pad(jnp.transpose(w_router.astype(F32)), ((0, LANES - N_EXPERTS), (0, 0)))
    wr_hi = wr.astype(BF16)
    wr_lo = (wr - wr_hi.astype(F32)).astype(BF16)
    wr_split = jnp.concatenate([wr_hi, wr_lo], axis=0)
    rbias_row = jnp.broadcast_to(router_bias.astype(F32)[:, None], (N_EXPERTS, LANES))

    for i in range(DEPTH):
        kind, j = i % N_MIXERS, i // N_MIXERS
        if kind == 0:
            y = _deltanet(h, batch, seq, a_w_in[j], a_conv[j], a_a_log[j], a_dt_bias[j], a_norm_g[j])
            w_out = a_w_out[j]
        elif kind == 1:
            lambda_init = 0.8 - 0.6 * math.exp(-0.3 * i)
            y = _diff_attention(h, batch, seq, b_w_in[j], b_lambda[j], b_norm_g[j], rel_bias, lambda_init)
            w_out = b_w_out[j]
        else:
            y = _gmlp(h, c_w_in[j], c_ln_g[j], c_ln_b[j], c_w_s[j], c_b_s[j])
            w_out = c_w_out[j]
        hs, meta, counts = _out_proj(y, w_out.astype(BF16), h, _row(ln_mix_g[i]), _row(ln_mix_b[i]),
                                     wr_split, rbias_row)
        h = _moe_layer(hs, meta, counts, e_w1[i].astype(BF16), e_w3[i].astype(BF16),
                       e_w2[i].astype(BF16), wr_rows, _row(ln_ffn_g[i]), _row(ln_ffn_b[i]),
                       row_major_out=(i == DEPTH - 1))
    return h.reshape(batch, seq, d).astype(x.dtype)
```

```python
import functools
import math

import jax
import jax.numpy as jnp
import numpy as np
from jax import lax
from jax.experimental import pallas as pl
from jax.experimental.pallas import tpu as pltpu

F32 = jnp.float32
BF16 = jnp.bfloat16
I32 = jnp.int32

D_MODEL = 1024
DEPTH = 4
N_MIXERS = 3
A_HEADS = 8
A_DK = 128
A_CONV = 4
B_DH = 64
B_HEADS = D_MODEL // (2 * B_DH)
REL_BUCKETS = 32
REL_MAX_DIST = 128
C_CHUNK = 128
C_HALF = 2 * D_MODEL
C_GROUPS = 8
C_GW = C_HALF // C_GROUPS
N_EXPERTS = 16
N_GROUPS = 4
EXPERTS_PER_GROUP = 4
D_EXPERT = 512
DN_ALPHA = (2 * DEPTH) ** 0.25
EPS = 1e-5

LANES = 128
SUBLANES = 8
VMEM_LIMIT_BYTES = 56 * 1024 * 1024

ROW_TILE = 512
OUT_TILE = 512
OUT_TILE_PARTS = 2
DN_CHUNK = 128
DN_SEQS_PER_STEP = 2
DN_CHAINS_PER_GROUP = 16
ATT_BLOCK = 512
V_ROWS = 2 * B_DH + 16
MOE_TILE = 512
PERMUTE_TOKENS = 1024
PERMUTE_UNROLL = 8
N_PAIR_CLASSES = N_GROUPS * 6
CLASS_ROWS = 32
NEG_BIG = -1e30
LOG2E = math.log2(math.e)

_PAIR_LO = np.array([g * 4 + a for g in range(4) for a, b in ((0, 1), (0, 2), (0, 3), (1, 2), (1, 3), (2, 3))], np.int32)
_PAIR_HI = np.array([g * 4 + b for g in range(4) for a, b in ((0, 1), (0, 2), (0, 3), (1, 2), (1, 3), (2, 3))], np.int32)


def _cparams(sem):
    return pltpu.CompilerParams(dimension_semantics=sem, vmem_limit_bytes=VMEM_LIMIT_BYTES)


def _layer_norm(x, g, b):
    mu = jnp.mean(x, -1, keepdims=True)
    xc = x - mu
    var = jnp.mean(xc * xc, -1, keepdims=True)
    return xc * lax.rsqrt(var + EPS) * g + b


def _silu(x):
    return x * jax.nn.sigmoid(x)


def _dot(a, b):
    return jnp.dot(a, b, preferred_element_type=F32)


def _dot_nt(a, b):
    return lax.dot_general(a, b, (((1,), (1,)), ((), ())), preferred_element_type=F32)


SLAB = D_MODEL // LANES


def _load_tokens(ref, n_tok, first=0):
    if ref.shape[-1] == D_MODEL:
        return ref[first:first + n_tok, :]
    return jnp.concatenate([ref[pl.ds(first * SLAB + s, n_tok, stride=SLAB), :] for s in range(SLAB)],
                           axis=1)


def _token_spec(arr, n_tok, index_map):
    if arr.shape[-1] == D_MODEL:
        return pl.BlockSpec((n_tok, D_MODEL), index_map)
    return pl.BlockSpec((n_tok * SLAB, LANES), index_map)


def _store_tokens(ref, val, n_tok, first=0):
    for s in range(SLAB):
        ref[pl.ds(first * SLAB + s, n_tok, stride=SLAB), :] = val[:, s * LANES:(s + 1) * LANES]


def _split3(x):
    hi = x.astype(BF16)
    r = x - hi.astype(F32)
    mid = r.astype(BF16)
    lo = (r - mid.astype(F32)).astype(BF16)
    return hi, mid, lo


def _dn_in_kernel(x_ref, w_ref, wab_ref, conv_ref, alog_ref, dtb_ref,
                  q_ref, k_ref, v_ref, gate_ref, gb_ref, ext_ref):
    ts = q_ref.shape[0]
    qkv_w = 3 * D_MODEL
    n_slabs = qkv_w // LANES
    s_idx = pl.program_id(1)

    @pl.when(s_idx == 0)
    def _():
        ext_ref[:, 0:SUBLANES, :] = jnp.zeros((n_slabs, SUBLANES, LANES), F32)

    @pl.when(s_idx > 0)
    def _():
        ext_ref[:, 0:SUBLANES, :] = ext_ref[:, ts:ts + SUBLANES, :]

    xb = _load_tokens(x_ref, ts).astype(BF16)
    gate_ref[...] = _dot(xb, w_ref[:, qkv_w:qkv_w + D_MODEL]).astype(BF16)
    cw = conv_ref[...]
    for part, out_ref in enumerate((q_ref, k_ref, v_ref)):
        pre = _dot(xb, w_ref[:, part * D_MODEL:(part + 1) * D_MODEL])
        for h in range(A_HEADS):
            slab = part * A_HEADS + h
            lo = h * A_DK
            ext_ref[slab, SUBLANES:SUBLANES + ts, :] = pre[:, lo:lo + A_DK]
            col = slice(slab * LANES, (slab + 1) * LANES)
            y = ext_ref[slab, pl.ds(SUBLANES - 3, ts), :] * cw[0:1, col]
            for j in range(1, A_CONV):
                y = y + ext_ref[slab, pl.ds(SUBLANES - 3 + j, ts), :] * cw[j:j + 1, col]
            y = _silu(y)
            if part == 0:
                y = y * (lax.rsqrt(jnp.sum(y * y, -1, keepdims=True) + 1e-6) * (A_DK ** -0.5))
            elif part == 1:
                y = y * lax.rsqrt(jnp.sum(y * y, -1, keepdims=True) + 1e-6)
            out_ref[:, lo:lo + A_DK] = y.astype(BF16)

    ab = _dot(xb, wab_ref[...])
    z = ab + dtb_ref[...]
    softplus = jnp.maximum(z, 0.0) + jnp.log1p(jnp.exp(-jnp.abs(z)))
    g = -jnp.exp(alog_ref[...]) * softplus
    lane = lax.broadcasted_iota(I32, ab.shape, 1)
    gb_ref[...] = jnp.where(lane < A_HEADS, g, jax.nn.sigmoid(ab))


def _dn_in(x2d, batch, seq, w_big, w_ab, conv_w, alog_row, dtb_row):
    ts = ROW_TILE
    ns = seq // ts
    rows = lambda b, s: (b * ns + s, 0)
    const = lambda b, s: (0, 0)
    t = batch * seq
    out_bf = jax.ShapeDtypeStruct((t, D_MODEL), BF16)
    return pl.pallas_call(
        _dn_in_kernel,
        grid=(batch, ns),
        in_specs=[
            _token_spec(x2d, ts, rows),
            pl.BlockSpec(w_big.shape, const),
            pl.BlockSpec(w_ab.shape, const),
            pl.BlockSpec(conv_w.shape, const),
            pl.BlockSpec(alog_row.shape, const),
            pl.BlockSpec(dtb_row.shape, const),
        ],
        out_specs=[pl.BlockSpec((ts, D_MODEL), rows)] * 4 + [pl.BlockSpec((ts, LANES), rows)],
        out_shape=[out_bf, out_bf, out_bf, out_bf, jax.ShapeDtypeStruct((t, LANES), F32)],
        scratch_shapes=[pltpu.VMEM((3 * D_MODEL // LANES, ts + 2 * SUBLANES, LANES), F32)],
        compiler_params=_cparams(("arbitrary", "arbitrary")),
        name="dn_in",
    )(x2d, w_big, w_ab, conv_w, alog_row, dtb_row)


def _dn_scan_kernel(q_ref, k_ref, v_ref, gate_ref, gb_ref, ng_ref, o_ref, state_ref):
    n_seq, c = q_ref.shape[0], q_ref.shape[1]
    n_doublings = int(math.log2(c)) - 1

    @pl.when(pl.program_id(1) == 0)
    def _():
        state_ref[...] = jnp.zeros(state_ref.shape, F32)

    row = lax.broadcasted_iota(I32, (c, c), 0)
    col = lax.broadcasted_iota(I32, (c, c), 1)
    causal = row >= col
    strict = row > col
    eye = (row == col).astype(F32)

    tril = causal.astype(BF16)
    gbs = [gb_ref[s] for s in range(n_seq)]
    gc_alls, gc_all_ts = [], []
    for gb_s in gbs:
        parts = _split3(gb_s)
        gc = _dot(tril, parts[0]) + _dot(tril, parts[1]) + _dot(tril, parts[2])
        gc_alls.append(gc)
        gc_all_ts.append(gc.T)
    ng = ng_ref[...]

    def chain_group(chains):
        heads = range(len(chains))
        seqs = [s for s, _ in chains]
        hid = [g for _, g in chains]
        lanes = [slice(g * A_DK, (g + 1) * A_DK) for g in hid]
        qh = [q_ref[seqs[h], :, lanes[h]] for h in heads]
        kh = [k_ref[seqs[h], :, lanes[h]] for h in heads]
        gc_col = [gc_alls[seqs[h]][:, hid[h]:hid[h] + 1] for h in heads]
        beta = [gbs[seqs[h]][:, A_HEADS + hid[h]:A_HEADS + hid[h] + 1] for h in heads]
        g_last = [gc_alls[seqs[h]][c - 1:c, hid[h]:hid[h] + 1] for h in heads]

        kq = [_dot_nt(jnp.concatenate([kh[h], qh[h]], axis=0), kh[h]) for h in heads]
        decay = [jnp.where(causal, jnp.exp(jnp.where(
            causal, gc_col[h] - gc_all_ts[seqs[h]][hid[h]:hid[h] + 1, :], 0.0)), 0.0) for h in heads]
        attn = [(kq[h][c:2 * c] * decay[h]).astype(BF16) for h in heads]

        p = [jnp.where(strict, -(kq[h][0:c] * beta[h] * decay[h]), 0.0) for h in heads]
        t_inv = [eye + p[h] for h in heads]
        for _ in range(n_doublings):
            pb = [p[h].astype(BF16) for h in heads]
            p = [_dot(pb[h], pb[h]) for h in heads]
            t_inv = [t_inv[h] + _dot(t_inv[h].astype(BF16), p[h].astype(BF16)) for h in heads]

        e_gc = [jnp.exp(gc_col[h]) for h in heads]
        khf = [kh[h].astype(F32) for h in heads]
        rhs = [jnp.concatenate([v_ref[seqs[h], :, lanes[h]].astype(F32) * beta[h],
                                khf[h] * (beta[h] * e_gc[h])], axis=1).astype(BF16)
               for h in heads]
        sol = [_dot(t_inv[h].astype(BF16), rhs[h]) for h in heads]

        slot = [seqs[h] * A_HEADS + hid[h] for h in heads]
        state = [state_ref[slot[h]] for h in heads]
        lhs = [jnp.concatenate([sol[h][:, A_DK:2 * A_DK], qh[h].astype(F32) * e_gc[h]],
                               axis=0).astype(BF16) for h in heads]
        ws = [_dot(lhs[h], state[h].astype(BF16)) for h in heads]
        v_new = [(sol[h][:, 0:A_DK] - ws[h][0:c]).astype(BF16) for h in heads]
        k_dec_t = [(khf[h] * jnp.exp(g_last[h] - gc_col[h])).T.astype(BF16) for h in heads]
        o = [ws[h][c:2 * c] + _dot(attn[h], v_new[h]) for h in heads]
        for h in heads:
            state_ref[slot[h]] = state[h] * jnp.exp(g_last[h]) + _dot(k_dec_t[h], v_new[h])
        for h in heads:
            on = o[h] * lax.rsqrt(jnp.mean(o[h] * o[h], -1, keepdims=True) + EPS) * ng
            gate = gate_ref[seqs[h], :, lanes[h]].astype(F32)
            o_ref[seqs[h], :, lanes[h]] = (on * _silu(gate)).astype(BF16)

    chains = [(s, g) for s in range(n_seq) for g in range(A_HEADS)]
    for first in range(0, len(chains), DN_CHAINS_PER_GROUP):
        chain_group(chains[first:first + DN_CHAINS_PER_GROUP])


def _dn_scan(q, k, v, gate, gb, norm_g_row, batch, seq):
    c = DN_CHUNK
    nc = seq // c
    n_seq = DN_SEQS_PER_STEP
    assert batch % n_seq == 0
    rows = lambda b, n: (b, n, 0)
    const = lambda b, n: (0, 0)
    wide = lambda a: a.reshape(batch, seq, a.shape[-1])
    out = pl.pallas_call(
        _dn_scan_kernel,
        grid=(batch // n_seq, nc),
        in_specs=[pl.BlockSpec((n_seq, c, D_MODEL), rows)] * 4
        + [pl.BlockSpec((n_seq, c, LANES), rows), pl.BlockSpec(norm_g_row.shape, const)],
        out_specs=pl.BlockSpec((n_seq, c, D_MODEL), rows),
        out_shape=jax.ShapeDtypeStruct((batch, seq, D_MODEL), BF16),
        scratch_shapes=[pltpu.VMEM((n_seq * A_HEADS, A_DK, A_DK), F32)],
        compiler_params=_cparams(("arbitrary", "arbitrary")),
        name="dn_scan",
    )(wide(q), wide(k), wide(v), wide(gate), wide(gb), norm_g_row)
    return out.reshape(batch * seq, D_MODEL)


def _at_in_kernel(x_ref, w_ref, qt_ref, k_ref, vt_ref):
    tm = k_ref.shape[1]
    xb = _load_tokens(x_ref, tm).astype(BF16)
    q = _dot(xb, w_ref[:, 0:D_MODEL]) * (B_DH ** -0.5 * LOG2E)
    lane = lax.broadcasted_iota(I32, q.shape, 1)
    first_map = (lane % (2 * B_DH)) < B_DH
    qt_ref[0] = jnp.where(first_map, q, 0.0).T.astype(BF16)
    qt_ref[1] = jnp.where(first_map, 0.0, q).T.astype(BF16)
    k = _dot(xb, w_ref[:, D_MODEL:2 * D_MODEL]).astype(BF16)
    vt = _dot(xb, w_ref[:, 2 * D_MODEL:3 * D_MODEL]).T.astype(BF16)
    ones = jnp.ones((V_ROWS - 2 * B_DH, tm), BF16)
    for h in range(B_HEADS):
        k_ref[h] = k[:, h * LANES:(h + 1) * LANES]
        vt_ref[h] = jnp.concatenate([vt[h * LANES:(h + 1) * LANES], ones], axis=0)


def _at_in(x2d, w_in):
    t = x2d.shape[0] // SLAB
    tm = ROW_TILE
    return pl.pallas_call(
        _at_in_kernel,
        grid=(t // tm,),
        in_specs=[pl.BlockSpec((tm * SLAB, LANES), lambda i: (i, 0)), pl.BlockSpec(w_in.shape, lambda i: (0, 0))],
        out_specs=[pl.BlockSpec((2, D_MODEL, tm), lambda i: (0, 0, i)),
                   pl.BlockSpec((B_HEADS, tm, LANES), lambda i: (0, i, 0)),
                   pl.BlockSpec((B_HEADS, V_ROWS, tm), lambda i: (0, 0, i))],
        out_shape=[jax.ShapeDtypeStruct((2, D_MODEL, t), BF16),
                   jax.ShapeDtypeStruct((B_HEADS, t, LANES), BF16),
                   jax.ShapeDtypeStruct((B_HEADS, V_ROWS, t), BF16)],
        compiler_params=_cparams(("parallel",)),
        name="at_in",
    )(x2d, w_in)


def _attn_kernel(scal_ref, q0_ref, q1_ref, q0n_ref, q1n_ref, k_ref, vt_ref, bias_ref, ng_ref, o_ref,
                 m_ref, acc_ref, sa_ref, sb_ref, sc_ref):
    tq = o_ref.shape[0]
    tk = tq
    i = pl.program_id(2)
    lam = scal_ref[0]
    out_scale = scal_ref[1]

    m_ref[...] = jnp.full(m_ref.shape, NEG_BIG, F32)
    acc_ref[...] = jnp.zeros(acc_ref.shape, F32)
    qts = (q0_ref[0], q1_ref[0])

    maps = range(2)

    def scores(j, s_ref, q_maps=qts):
        kb = k_ref[0, pl.ds(pl.multiple_of(j * tk, tk), tk), :]
        for mp in maps:
            s_ref[mp] = _dot(kb, q_maps[mp])

    def absorb(j, s_ref, bias):
        vtb = vt_ref[0, :, pl.ds(pl.multiple_of(j * tk, tk), tk)]
        s = [s_ref[mp] for mp in maps]
        if bias is not None:
            s = [x + bias for x in s]
        m_old = [m_ref[mp] for mp in maps]
        m_new = [jnp.maximum(m_old[mp], jnp.max(s[mp], 0, keepdims=True)) for mp in maps]
        alpha = [jnp.exp2(m_old[mp] - m_new[mp]) for mp in maps]
        p = [jnp.exp2(s[mp] - m_new[mp]) for mp in maps]
        pv = [_dot(vtb, p[mp].astype(BF16)) for mp in maps]
        for mp in maps:
            acc_ref[mp] = alpha[mp] * acc_ref[mp] + pv[mp]
            m_ref[mp] = m_new[mp]

    @pl.when(i == 0)
    def _():
        scores(0, sa_ref)

    @pl.when((i % 2 == 0) & (i >= 2))
    def _():
        scores(1, sb_ref)
        absorb(0, sc_ref, None)

    first = jnp.where(i % 2 == 1, 0, 1)

    def far_pair(jj, carry):
        j = first + 2 * jj
        scores(j + 1, sa_ref)
        absorb(j, sb_ref, None)
        scores(j + 2, sb_ref)
        absorb(j + 1, sa_ref, None)
        return carry

    lax.fori_loop(0, jnp.maximum(i - 1, 0) // 2, far_pair, 0)

    @pl.when(i >= 1)
    def _():
        scores(i, sa_ref)
        absorb(i - 1, sb_ref, bias_ref[0, 1])

    last = i == pl.num_programs(2) - 1
    next_qts = (q0n_ref[0], q1n_ref[0])

    @pl.when(last)
    def _():
        absorb(i, sa_ref, bias_ref[0, 0])

    @pl.when(jnp.logical_not(last) & (i % 2 == 0))
    def _():
        scores(0, sb_ref, next_qts)
        absorb(i, sa_ref, bias_ref[0, 0])

    @pl.when(jnp.logical_not(last) & (i % 2 == 1))
    def _():
        scores(0, sc_ref, next_qts)
        absorb(i, sa_ref, bias_ref[0, 0])

    dv = 2 * B_DH
    o_t = (acc_ref[0, 0:dv] * (1.0 / acc_ref[0, dv:dv + 1])
           - acc_ref[1, 0:dv] * (lam / acc_ref[1, dv:dv + 1]))
    o = o_t.T
    o = o * lax.rsqrt(jnp.mean(o * o, -1, keepdims=True) + EPS) * ng_ref[...] * out_scale
    o_ref[...] = o.astype(BF16)


def _attn(scalars, qt, k, vt, bias_tab, norm_g_row, batch, seq):
    tq = ATT_BLOCK
    nq = seq // tq
    t = batch * seq
    return pl.pallas_call(
        _attn_kernel,
        grid=(batch, B_HEADS, nq),
        in_specs=[
            pl.BlockSpec(memory_space=pltpu.SMEM),
            pl.BlockSpec((1, LANES, tq), lambda b, h, i: (0, h, b * nq + i)),
            pl.BlockSpec((1, LANES, tq), lambda b, h, i: (1, h, b * nq + i)),
            pl.BlockSpec((1, LANES, tq), lambda b, h, i: (0, h, b * nq + jnp.minimum(i + 1, nq - 1))),
            pl.BlockSpec((1, LANES, tq), lambda b, h, i: (1, h, b * nq + jnp.minimum(i + 1, nq - 1))),
            pl.BlockSpec((1, seq, LANES), lambda b, h, i: (h, b, 0)),
            pl.BlockSpec((1, V_ROWS, seq), lambda b, h, i: (h, 0, b)),
            pl.BlockSpec((1, 2, tq, tq), lambda b, h, i: (h, 0, 0, 0)),
            pl.BlockSpec(norm_g_row.shape, lambda b, h, i: (0, 0)),
        ],
        out_specs=pl.BlockSpec((tq, LANES), lambda b, h, i: (b * nq + i, h)),
        out_shape=jax.ShapeDtypeStruct((t, D_MODEL), BF16),
        scratch_shapes=[pltpu.VMEM((2, 1, tq), F32), pltpu.VMEM((2, V_ROWS, tq), F32),
                        pltpu.VMEM((2, tq, tq), F32), pltpu.VMEM((2, tq, tq), F32),
                        pltpu.VMEM((2, tq, tq), F32)],
        compiler_params=_cparams(("parallel", "parallel", "arbitrary")),
        name="attn",
    )(scalars, qt, qt, qt, qt, k, vt, bias_tab, norm_g_row)


def _rel_bucket(rel):
    n = jnp.maximum(rel, 0)
    max_exact = REL_BUCKETS // 2
    large = max_exact + (jnp.log(jnp.maximum(n, 1).astype(F32) / max_exact)
                         / math.log(REL_MAX_DIST / max_exact) * (REL_BUCKETS - max_exact)).astype(I32)
    large = jnp.minimum(large, REL_BUCKETS - 1)
    return jnp.where(n < max_exact, n, large)


def _attn_bias_table(rel_bias):
    tq = ATT_BLOCK
    assert tq + 1 >= REL_MAX_DIST, "blocks two back must fall in the last bucket"
    table = (rel_bias.astype(F32) - rel_bias.astype(F32)[REL_BUCKETS - 1]) * LOG2E
    period = 2 * tq
    idx = jnp.arange(period)
    diag_vec = jnp.where((idx < tq)[:, None], table[_rel_bucket(jnp.minimum(idx, tq - 1))], NEG_BIG)
    near_vec = table[_rel_bucket(jnp.where(idx < tq, idx + tq, idx - tq))]
    vecs = jnp.transpose(jnp.stack([diag_vec, near_vec], 0), (2, 0, 1))
    return pl.pallas_call(
        _toeplitz_kernel,
        grid=(B_HEADS, 2),
        in_specs=[pl.BlockSpec((1, 1, 1, period), lambda h, k: (h, k, 0, 0))],
        out_specs=pl.BlockSpec((1, 1, tq, tq), lambda h, k: (h, k, 0, 0)),
        out_shape=jax.ShapeDtypeStruct((B_HEADS, 2, tq, tq), F32),
        name="bias_table",
    )(vecs.reshape(B_HEADS, 2, 1, period))


def _toeplitz_kernel(v_ref, o_ref):
    tq = o_ref.shape[2]
    x = jnp.broadcast_to(v_ref[0, 0], (tq, v_ref.shape[3]))
    o_ref[0, 0] = pltpu.roll(x, 0, 1, stride=1, stride_axis=0)[:, :tq]


def _gelu(x):
    return 0.5 * x * (1.0 + lax.erf(x * (2.0 ** -0.5)))


def _gm_kernel(x_ref, w_ref, lng_ref, lnb_ref, ws_ref, bs_ref, o_ref):
    tm = o_ref.shape[0]
    xb = _load_tokens(x_ref, tm).astype(BF16)
    v = _gelu(_dot(xb, w_ref[:, C_HALF:2 * C_HALF]))
    vn = _layer_norm(v, lng_ref[...], lnb_ref[...]).astype(BF16)
    bs = bs_ref[...]
    for g in range(C_GROUPS):
        lo = g * C_GW
        u = _gelu(_dot(xb, w_ref[:, lo:lo + C_GW]))
        wg = ws_ref[g]
        for ch in range(tm // C_CHUNK):
            r0 = ch * C_CHUNK
            mixed = _dot(wg, vn[r0:r0 + C_CHUNK, lo:lo + C_GW]) + bs[:, g:g + 1]
            o_ref[r0:r0 + C_CHUNK, lo:lo + C_GW] = (u[r0:r0 + C_CHUNK] * mixed).astype(BF16)


def _gm(x2d, w_in, ln_g_row, ln_b_row, ws_causal, bs_t):
    t = x2d.shape[0] // SLAB
    tm = ROW_TILE
    rows = lambda i: (i, 0)
    const2 = lambda i: (0, 0)
    return pl.pallas_call(
        _gm_kernel,
        grid=(t // tm,),
        in_specs=[
            pl.BlockSpec((tm * SLAB, LANES), rows),
            pl.BlockSpec(w_in.shape, const2),
            pl.BlockSpec(ln_g_row.shape, const2),
            pl.BlockSpec(ln_b_row.shape, const2),
            pl.BlockSpec(ws_causal.shape, lambda i: (0, 0, 0)),
            pl.BlockSpec(bs_t.shape, const2),
        ],
        out_specs=pl.BlockSpec((tm, C_HALF), rows),
        out_shape=jax.ShapeDtypeStruct((t, C_HALF), BF16),
        compiler_params=_cparams(("parallel",)),
        name="gmlp",
    )(x2d, w_in, ln_g_row, ln_b_row, ws_causal, bs_t)


_PAIRS = ((0, 1), (0, 2), (0, 3), (1, 2), (1, 3), (2, 3))


def _out_kernel(y_ref, w_ref, h_ref, lng_ref, lnb_ref, wr_ref, rb_ref, hs_ref, meta_ref, cnt_ref,
                carry_ref):
    tm = y_ref.shape[0]

    @pl.when(pl.program_id(0) == 0)
    def _():
        carry_ref[...] = jnp.zeros(carry_ref.shape, F32)

    n_part = OUT_TILE_PARTS
    rows_per = tm // n_part
    parts = range(n_part)
    lng, lnb = lng_ref[...], lnb_ref[...]
    proj = [_dot(y_ref[k * rows_per:(k + 1) * rows_per, :], w_ref[...]) for k in parts]
    h1 = [_layer_norm(DN_ALPHA * _load_tokens(h_ref, rows_per, k * rows_per) + proj[k], lng, lnb)
          for k in parts]
    for k in parts:
        _store_tokens(hs_ref, h1[k], rows_per, k * rows_per)

    h_hi = [h1[k].astype(BF16) for k in parts]
    h_lo = [(h1[k] - h_hi[k].astype(F32)).astype(BF16) for k in parts]
    both = [_dot_nt(wr_ref[...], h_hi[k]) for k in parts]
    low = [_dot_nt(wr_ref[0:N_EXPERTS, :], h_lo[k]) for k in parts]
    logits = jnp.concatenate(
        [both[k][0:N_EXPERTS] + both[k][LANES:LANES + N_EXPERTS] + low[k] for k in parts], axis=1)
    sel = jax.nn.sigmoid(logits) + jnp.concatenate([rb_ref[...]] * (tm // LANES), axis=1)
    rows = [sel[e:e + 1, :] for e in range(N_EXPERTS)]

    def top2_sum(v):
        sums = [v[a] + v[b] for a, b in _PAIRS]
        return functools.reduce(jnp.maximum, sums)

    grp = [top2_sum(rows[g * EXPERTS_PER_GROUP:(g + 1) * EXPERTS_PER_GROUP]) for g in range(N_GROUPS)]
    best, best_val = jnp.zeros_like(grp[0]), grp[0]
    for g in range(1, N_GROUPS):
        better = grp[g] > best_val
        best = jnp.where(better, float(g), best)
        best_val = jnp.where(better, grp[g], best_val)
    member = []
    for k in range(EXPERTS_PER_GROUP):
        v = rows[k]
        for g in range(1, N_GROUPS):
            v = jnp.where(best == float(g), rows[g * EXPERTS_PER_GROUP + k], v)
        member.append(v)
    chosen = []
    for k in range(EXPERTS_PER_GROUP):
        beaten = jnp.zeros_like(best)
        for j in range(EXPERTS_PER_GROUP):
            if j != k:
                ahead = (member[j] >= member[k]) if j < k else (member[j] > member[k])
                beaten = beaten + ahead.astype(F32)
        chosen.append(beaten < 2.0)
    cls = best * float(len(_PAIRS))
    for idx, (a, b) in enumerate(_PAIRS):
        cls = cls + jnp.where(chosen[a] & chosen[b], float(idx), 0.0)

    n_cls_rows = carry_ref.shape[0]
    cls_row = lax.broadcasted_iota(I32, (n_cls_rows, tm), 0).astype(F32)
    onehot = (cls_row == cls).astype(F32)
    t_i = lax.broadcasted_iota(I32, (tm, tm), 0)
    t_j = lax.broadcasted_iota(I32, (tm, tm), 1)
    carry = carry_ref[...]
    before = _dot(onehot.astype(BF16), (t_i < t_j).astype(BF16)) + jnp.concatenate(
        [carry] * (tm // LANES), axis=1)
    rank = jnp.sum(onehot * before, 0, keepdims=True)
    carry_ref[...] = carry + jnp.sum(onehot, 1, keepdims=True)
    cnt_ref[...] = carry_ref[...]

    row_id = lax.broadcasted_iota(I32, meta_ref.shape, 0)
    meta_ref[...] = jnp.where(row_id == 0, cls, jnp.where(row_id == 1, rank, 0.0))


def _out_proj(y, w_out, hs, ln_g_row, ln_b_row, wr_split, rbias_row):
    t, kd = y.shape
    tm = OUT_TILE
    rows = lambda i: (i, 0)
    const = lambda i: (0, 0)
    return pl.pallas_call(
        _out_kernel,
        grid=(t // tm,),
        in_specs=[
            pl.BlockSpec((tm, kd), rows),
            pl.BlockSpec(w_out.shape, const),
            _token_spec(hs, tm, rows),
            pl.BlockSpec(ln_g_row.shape, const),
            pl.BlockSpec(ln_b_row.shape, const),
            pl.BlockSpec(wr_split.shape, const),
            pl.BlockSpec(rbias_row.shape, const),
        ],
        out_specs=[pl.BlockSpec((tm * SLAB, LANES), rows), pl.BlockSpec((SUBLANES, tm), lambda i: (0, i)),
                   pl.BlockSpec((CLASS_ROWS, LANES), const)],
        out_shape=[jax.ShapeDtypeStruct((t * SLAB, LANES), F32),
                   jax.ShapeDtypeStruct((SUBLANES, t), F32),
                   jax.ShapeDtypeStruct((CLASS_ROWS, LANES), F32)],
        scratch_shapes=[pltpu.VMEM((CLASS_ROWS, LANES), F32)],
        compiler_params=_cparams(("arbitrary",)),
        name="out_proj",
    )(y, w_out, hs, ln_g_row, ln_b_row, wr_split, rbias_row)


def _permute_kernel(pos_ref, src_ref, *refs, scatter, row_major_out):
    if row_major_out:
        o_ref, sem, dst_ref = refs[-3:]
    else:
        dst_ref, sem = refs[-2:]
    n_tok = pos_ref.shape[2]

    def issue(g, carry):
        for u in range(PERMUTE_UNROLL):
            r = g * PERMUTE_UNROLL + u
            here = pl.ds(pl.multiple_of(r * SLAB, SLAB), SLAB)
            there = pl.ds(pl.multiple_of(pos_ref[0, 0, r] * SLAB, SLAB), SLAB)
            src_rows, dst_rows = (here, there) if scatter else (there, here)
            pltpu.make_async_copy(src_ref.at[src_rows, :], dst_ref.at[dst_rows, :], sem).start(
                priority=u % 2)
        return carry

    lax.fori_loop(0, n_tok // PERMUTE_UNROLL, issue, 0)
    whole = pl.ds(0, n_tok * SLAB)
    pltpu.make_async_copy(src_ref.at[whole, :], dst_ref.at[whole, :], sem).wait()
    if row_major_out:
        o_ref[...] = _load_tokens(dst_ref, n_tok)


def _permute(pos, src, n_dst_rows, scatter, row_major_out=False, zeroed_dst=None):
    n_tok = pos.shape[2]
    any_spec = pl.BlockSpec(memory_space=pl.ANY)
    tile_spec = pl.BlockSpec((n_tok * SLAB, LANES), lambda i: (i, 0))
    out_shape = jax.ShapeDtypeStruct((n_dst_rows, LANES), F32)
    scratch = [pltpu.SemaphoreType.DMA(())]
    if row_major_out:
        assert not scatter
        tile_spec = pl.BlockSpec((n_tok, D_MODEL), lambda i: (i, 0))
        out_shape = jax.ShapeDtypeStruct((n_dst_rows // SLAB, D_MODEL), F32)
        scratch.append(pltpu.VMEM((n_tok * SLAB, LANES), F32))
    extra = []
    if scatter:
        extra = [jnp.zeros((n_dst_rows, LANES), F32) if zeroed_dst is None else zeroed_dst]
    return pl.pallas_call(
        functools.partial(_permute_kernel, scatter=scatter, row_major_out=row_major_out),
        grid=(pos.shape[0],),
        in_specs=[pl.BlockSpec((1, 1, n_tok), lambda i: (i, 0, 0), memory_space=pltpu.SMEM),
                  tile_spec if scatter else any_spec] + [any_spec] * len(extra),
        out_specs=any_spec if scatter else tile_spec,
        out_shape=out_shape,
        scratch_shapes=scratch,
        input_output_aliases={2: 0} if scatter else {},
        compiler_params=_cparams(("arbitrary",)),
        name="dispatch" if scatter else "unpermute",
    )(pos, src, *extra)


def _moe_kernel(lo_ref, hi_ref, nv_ref, xs_ref, w1l_ref, w3l_ref, w2l_ref, w1h_ref, w3h_ref, w2h_ref,
                wrl_ref, wrh_ref, lng_ref, lnb_ref, ys_ref, zeros_ref):
    del lo_ref, hi_ref
    zeros_ref[...] = jnp.zeros(zeros_ref.shape, F32)
    tm = xs_ref.shape[0] // SLAB

    @pl.when(pl.program_id(0) < nv_ref[0])
    def _():
        x = _load_tokens(xs_ref, tm)
        xb = x.astype(BF16)

        def expert(w1_ref, w3_ref, w2_ref):
            hid = _silu(_dot(xb, w1_ref[0, 0])) * _dot(xb, w3_ref[0, 0])
            return _dot(hid.astype(BF16), w2_ref[0, 0])

        s_lo = jax.nn.sigmoid(jnp.sum(x * wrl_ref[0], -1, keepdims=True))
        s_hi = jax.nn.sigmoid(jnp.sum(x * wrh_ref[0], -1, keepdims=True))
        denom = s_lo + s_hi
        m = (s_lo / denom) * expert(w1l_ref, w3l_ref, w2l_ref)
        m = m + (s_hi / denom) * expert(w1h_ref, w3h_ref, w2h_ref)
        _store_tokens(ys_ref, _layer_norm(DN_ALPHA * x + m, lng_ref[...], lnb_ref[...]), tm)

    @pl.when(pl.program_id(0) >= nv_ref[0])
    def _():
        ys_ref[...] = jnp.zeros(ys_ref.shape, F32)


def _moe(tile_lo, tile_hi, n_valid, xs, w1, w3, w2, layer, wr_rows, ln_g_row, ln_b_row):
    tm = MOE_TILE
    n_tiles = xs.shape[0] // (tm * SLAB)
    rows = lambda i, lo, hi, nv: (jnp.minimum(i, nv[0] - 1), 0)
    w_lo = lambda i, lo, hi, nv: (layer, lo[i], 0, 0)
    w_hi = lambda i, lo, hi, nv: (layer, hi[i], 0, 0)
    r_lo = lambda i, lo, hi, nv: (lo[i], 0, 0)
    r_hi = lambda i, lo, hi, nv: (hi[i], 0, 0)
    const = lambda i, lo, hi, nv: (0, 0)
    up = (1, 1, D_MODEL, D_EXPERT)
    down = (1, 1, D_EXPERT, D_MODEL)
    router_row = (1, 1, D_MODEL)
    grid_spec = pltpu.PrefetchScalarGridSpec(
        num_scalar_prefetch=3,
        grid=(n_tiles,),
        in_specs=[
            pl.BlockSpec((tm * SLAB, LANES), rows),
            pl.BlockSpec(up, w_lo), pl.BlockSpec(up, w_lo), pl.BlockSpec(down, w_lo),
            pl.BlockSpec(up, w_hi), pl.BlockSpec(up, w_hi), pl.BlockSpec(down, w_hi),
            pl.BlockSpec(router_row, r_lo), pl.BlockSpec(router_row, r_hi),
            pl.BlockSpec(ln_g_row.shape, const), pl.BlockSpec(ln_b_row.shape, const),
        ],
        out_specs=[pl.BlockSpec((tm * SLAB, LANES), lambda i, lo, hi, nv: (i, 0))] * 2,
    )
    return pl.pallas_call(
        _moe_kernel,
        grid_spec=grid_spec,
        out_shape=[jax.ShapeDtypeStruct(xs.shape, F32)] * 2,
        compiler_params=_cparams(("arbitrary",)),
        name="moe",
    )(tile_lo, tile_hi, n_valid, xs, w1, w3, w2, w1, w3, w2, wr_rows, wr_rows, ln_g_row, ln_b_row)


def _routing_plan(meta, counts, n_slots):
    cls = meta[0].astype(I32)
    rank = meta[1].astype(I32)
    cnt = counts[:N_PAIR_CLASSES, 0].astype(I32)
    padded = ((cnt + MOE_TILE - 1) // MOE_TILE) * MOE_TILE
    end = jnp.cumsum(padded)
    start = end - padded
    classes = jnp.arange(N_PAIR_CLASSES, dtype=I32)
    pos = jnp.sum(jnp.where(cls[:, None] == classes[None, :], start[None, :], 0), -1) + rank
    n_tiles = n_slots // MOE_TILE
    n_valid = (end[-1] // MOE_TILE).astype(I32).reshape(1)
    tile_start = jnp.minimum(jnp.arange(n_tiles, dtype=I32), n_valid[0] - 1) * MOE_TILE
    tile_cls = jnp.sum((tile_start[:, None] >= end[None, :]).astype(I32), -1)
    onehot = (tile_cls[:, None] == classes[None, :]).astype(I32)
    tile_lo = jnp.sum(onehot * jnp.asarray(_PAIR_LO)[None, :], -1)
    tile_hi = jnp.sum(onehot * jnp.asarray(_PAIR_HI)[None, :], -1)
    return pos, tile_lo, tile_hi, n_valid


def _moe_layer(hs, meta, counts, w1, w3, w2, layer, wr_rows, ln_g_row, ln_b_row, row_major_out,
               zeroed_slots):
    t = meta.shape[1]
    n_slots = t + N_PAIR_CLASSES * MOE_TILE
    pos, tile_lo, tile_hi, n_valid = _routing_plan(meta, counts, n_slots)
    pos = pos.reshape(t // PERMUTE_TOKENS, 1, PERMUTE_TOKENS)
    xs = _permute(pos, hs, n_slots * SLAB, scatter=True, zeroed_dst=zeroed_slots)
    ys, zeroed_next = _moe(tile_lo, tile_hi, n_valid, xs, w1, w3, w2, layer, wr_rows, ln_g_row, ln_b_row)
    return _permute(pos, ys, t * SLAB, scatter=False, row_major_out=row_major_out), zeroed_next


def _row(v, width=None):
    v = v.astype(F32).reshape(1, -1)
    if width is not None and v.shape[1] < width:
        v = jnp.pad(v, ((0, 0), (0, width - v.shape[1])))
    return v


def _deltanet(h2d, batch, seq, w_in, conv_w, a_log, dt_bias, norm_g):
    qkvg = 4 * D_MODEL
    w_big = w_in.astype(BF16)
    w_ab = jnp.pad(w_in[:, qkvg:], ((0, 0), (0, LANES - 2 * A_HEADS))).astype(BF16)
    q, k, v, gate, gb = _dn_in(h2d, batch, seq, w_big, w_ab, conv_w.astype(F32),
                               _row(a_log, LANES), _row(dt_bias, LANES))
    return _dn_scan(q, k, v, gate, gb, _row(norm_g), batch, seq)


def _diff_attention(h2d, batch, seq, w_in, lam, norm_g, rel_bias, lambda_init):
    lamf = lam.astype(F32)
    lam_full = jnp.exp(jnp.sum(lamf[0] * lamf[1])) - jnp.exp(jnp.sum(lamf[2] * lamf[3])) + lambda_init
    scalars = jnp.stack([lam_full, jnp.asarray(1.0 - lambda_init, F32)]).astype(F32)
    qt, k, vt = _at_in(h2d, w_in.astype(BF16))
    return _attn(scalars, qt, k, vt, _attn_bias_table(rel_bias), _row(norm_g), batch, seq)


def _gmlp(h2d, w_in, ln_g, ln_b, w_s, b_s):
    ws_causal = jnp.tril(w_s.astype(F32)).astype(BF16)
    bs_t = jnp.pad(jnp.transpose(b_s.astype(F32)), ((0, 0), (0, LANES - C_GROUPS)))
    return _gm(h2d, w_in.astype(BF16), _row(ln_g), _row(ln_b), ws_causal, bs_t)


def kernel(x, ln_mix_g, ln_mix_b, ln_ffn_g, ln_ffn_b, a_w_in, a_conv, a_a_log, a_dt_bias, a_norm_g, a_w_out, b_w_in, b_lambda, b_norm_g, b_w_out, rel_bias, c_w_in, c_ln_g, c_ln_b, c_w_s, c_b_s, c_w_out, w_router, router_bias, e_w1, e_w3, e_w2):
    batch, seq, d = x.shape
    assert d == D_MODEL and seq % max(ROW_TILE, ATT_BLOCK, DN_CHUNK) == 0
    t = batch * seq
    h = x.astype(F32).reshape(t, d)

    wr_rows = jnp.transpose(w_router.astype(F32)).reshape(N_EXPERTS, 1, D_MODEL)
    wr = jnp.pad(jnp.transpose(w_router.astype(F32)), ((0, LANES - N_EXPERTS), (0, 0)))
    wr_hi = wr.astype(BF16)
    wr_lo = (wr - wr_hi.astype(F32)).astype(BF16)
    wr_split = jnp.concatenate([wr_hi, wr_lo], axis=0)
    rbias_row = jnp.broadcast_to(router_bias.astype(F32)[:, None], (N_EXPERTS, LANES))

    w1_all, w3_all, w2_all = e_w1.astype(BF16), e_w3.astype(BF16), e_w2.astype(BF16)

    zeroed_slots = None
    for i in range(DEPTH):
        kind, j = i % N_MIXERS, i // N_MIXERS
        if kind == 0:
            y = _deltanet(h, batch, seq, a_w_in[j], a_conv[j], a_a_log[j], a_dt_bias[j], a_norm_g[j])
            w_out = a_w_out[j]
        elif kind == 1:
            lambda_init = 0.8 - 0.6 * math.exp(-0.3 * i)
            y = _diff_attention(h, batch, seq, b_w_in[j], b_lambda[j], b_norm_g[j], rel_bias, lambda_init)
            w_out = b_w_out[j]
        else:
            y = _gmlp(h, c_w_in[j], c_ln_g[j], c_ln_b[j], c_w_s[j], c_b_s[j])
            w_out = c_w_out[j]
        hs, meta, counts = _out_proj(y, w_out.astype(BF16), h, _row(ln_mix_g[i]), _row(ln_mix_b[i]),
                                     wr_split, rbias_row)
        h, zeroed_slots = _moe_layer(hs, meta, counts, w1_all, w3_all, w2_all, i, wr_rows,
                                     _row(ln_ffn_g[i]), _row(ln_ffn_b[i]),
                                     row_major_out=(i == DEPTH - 1), zeroed_slots=zeroed_slots)
    return h.reshape(batch, seq, d).astype(x.dtype)
```

```python
import functools
import math

import jax
import jax.numpy as jnp
import numpy as np
from jax import lax
from jax.experimental import pallas as pl
from jax.experimental.pallas import tpu as pltpu

F32 = jnp.float32
BF16 = jnp.bfloat16
I32 = jnp.int32

D_MODEL = 1024
DEPTH = 4
N_MIXERS = 3
A_HEADS = 8
A_DK = 128
A_CONV = 4
B_DH = 64
B_HEADS = D_MODEL // (2 * B_DH)
REL_BUCKETS = 32
REL_MAX_DIST = 128
C_CHUNK = 128
C_HALF = 2 * D_MODEL
C_GROUPS = 8
C_GW = C_HALF // C_GROUPS
N_EXPERTS = 16
N_GROUPS = 4
EXPERTS_PER_GROUP = 4
D_EXPERT = 512
DN_ALPHA = (2 * DEPTH) ** 0.25
EPS = 1e-5

LANES = 128
SUBLANES = 8
VMEM_LIMIT_BYTES = 56 * 1024 * 1024

ROW_TILE = 512
OUT_TILE = 512
OUT_TILE_PARTS = 2
DN_CHUNK = 128
DN_SEQS_PER_STEP = 2
DN_CHAINS_PER_GROUP = 16
ATT_BLOCK = 512
V_ROWS = 2 * B_DH + 16
MOE_TILE = 512
PERMUTE_TOKENS = 1024
PERMUTE_UNROLL = 8
N_PAIR_CLASSES = N_GROUPS * 6
CLASS_ROWS = 32
NEG_BIG = -1e30
LOG2E = math.log2(math.e)

_PAIR_LO = np.array([g * 4 + a for g in range(4) for a, b in ((0, 1), (0, 2), (0, 3), (1, 2), (1, 3), (2, 3))], np.int32)
_PAIR_HI = np.array([g * 4 + b for g in range(4) for a, b in ((0, 1), (0, 2), (0, 3), (1, 2), (1, 3), (2, 3))], np.int32)


def _cparams(sem):
    return pltpu.CompilerParams(dimension_semantics=sem, vmem_limit_bytes=VMEM_LIMIT_BYTES)


def _layer_norm(x, g, b):
    mu = jnp.mean(x, -1, keepdims=True)
    xc = x - mu
    var = jnp.mean(xc * xc, -1, keepdims=True)
    return xc * lax.rsqrt(var + EPS) * g + b


def _silu(x):
    return x * jax.nn.sigmoid(x)


def _dot(a, b):
    return jnp.dot(a, b, preferred_element_type=F32)


def _dot_nt(a, b):
    return lax.dot_general(a, b, (((1,), (1,)), ((), ())), preferred_element_type=F32)


SLAB = D_MODEL // LANES


def _load_tokens(ref, n_tok, first=0):
    if ref.shape[-1] == D_MODEL:
        return ref[first:first + n_tok, :]
    return jnp.concatenate([ref[pl.ds(first * SLAB + s, n_tok, stride=SLAB), :] for s in range(SLAB)],
                           axis=1)


def _token_spec(arr, n_tok, index_map):
    if arr.shape[-1] == D_MODEL:
        return pl.BlockSpec((n_tok, D_MODEL), index_map)
    return pl.BlockSpec((n_tok * SLAB, LANES), index_map)


def _store_tokens(ref, val, n_tok, first=0):
    for s in range(SLAB):
        ref[pl.ds(first * SLAB + s, n_tok, stride=SLAB), :] = val[:, s * LANES:(s + 1) * LANES]


def _split3(x):
    hi = x.astype(BF16)
    r = x - hi.astype(F32)
    mid = r.astype(BF16)
    lo = (r - mid.astype(F32)).astype(BF16)
    return hi, mid, lo


def _dn_in_kernel(x_ref, w_ref, wab_ref, conv_ref, alog_ref, dtb_ref,
                  q_ref, k_ref, v_ref, gate_ref, gb_ref, ext_ref):
    ts = q_ref.shape[0]
    qkv_w = 3 * D_MODEL
    n_slabs = qkv_w // LANES
    s_idx = pl.program_id(1)

    @pl.when(s_idx == 0)
    def _():
        ext_ref[:, 0:SUBLANES, :] = jnp.zeros((n_slabs, SUBLANES, LANES), F32)

    @pl.when(s_idx > 0)
    def _():
        ext_ref[:, 0:SUBLANES, :] = ext_ref[:, ts:ts + SUBLANES, :]

    xb = _load_tokens(x_ref, ts).astype(BF16)
    gate_ref[...] = _dot(xb, w_ref[:, qkv_w:qkv_w + D_MODEL]).astype(BF16)
    cw = conv_ref[...]
    for part, out_ref in enumerate((q_ref, k_ref, v_ref)):
        pre = _dot(xb, w_ref[:, part * D_MODEL:(part + 1) * D_MODEL])
        for h in range(A_HEADS):
            slab = part * A_HEADS + h
            lo = h * A_DK
            ext_ref[slab, SUBLANES:SUBLANES + ts, :] = pre[:, lo:lo + A_DK]
            col = slice(slab * LANES, (slab + 1) * LANES)
            y = ext_ref[slab, pl.ds(SUBLANES - 3, ts), :] * cw[0:1, col]
            for j in range(1, A_CONV):
                y = y + ext_ref[slab, pl.ds(SUBLANES - 3 + j, ts), :] * cw[j:j + 1, col]
            y = _silu(y)
            if part == 0:
                y = y * (lax.rsqrt(jnp.sum(y * y, -1, keepdims=True) + 1e-6) * (A_DK ** -0.5))
            elif part == 1:
                y = y * lax.rsqrt(jnp.sum(y * y, -1, keepdims=True) + 1e-6)
            out_ref[:, lo:lo + A_DK] = y.astype(BF16)

    ab = _dot(xb, wab_ref[...])
    z = ab + dtb_ref[...]
    softplus = jnp.maximum(z, 0.0) + jnp.log1p(jnp.exp(-jnp.abs(z)))
    g = -jnp.exp(alog_ref[...]) * softplus
    lane = lax.broadcasted_iota(I32, ab.shape, 1)
    gb_ref[...] = jnp.where(lane < A_HEADS, g, jax.nn.sigmoid(ab))


def _dn_in(x2d, batch, seq, w_big, w_ab, conv_w, alog_row, dtb_row):
    ts = ROW_TILE
    ns = seq // ts
    rows = lambda b, s: (b * ns + s, 0)
    const = lambda b, s: (0, 0)
    t = batch * seq
    out_bf = jax.ShapeDtypeStruct((t, D_MODEL), BF16)
    return pl.pallas_call(
        _dn_in_kernel,
        grid=(batch, ns),
        in_specs=[
            _token_spec(x2d, ts, rows),
            pl.BlockSpec(w_big.shape, const),
            pl.BlockSpec(w_ab.shape, const),
            pl.BlockSpec(conv_w.shape, const),
            pl.BlockSpec(alog_row.shape, const),
            pl.BlockSpec(dtb_row.shape, const),
        ],
        out_specs=[pl.BlockSpec((ts, D_MODEL), rows)] * 4 + [pl.BlockSpec((ts, LANES), rows)],
        out_shape=[out_bf, out_bf, out_bf, out_bf, jax.ShapeDtypeStruct((t, LANES), F32)],
        scratch_shapes=[pltpu.VMEM((3 * D_MODEL // LANES, ts + 2 * SUBLANES, LANES), F32)],
        compiler_params=_cparams(("arbitrary", "arbitrary")),
        name="dn_in",
    )(x2d, w_big, w_ab, conv_w, alog_row, dtb_row)


def _dn_scan_kernel(q_ref, k_ref, v_ref, gate_ref, gb_ref, ng_ref, o_ref, state_ref):
    n_seq, c = q_ref.shape[0], q_ref.shape[1]
    n_doublings = int(math.log2(c)) - 1

    @pl.when(pl.program_id(1) == 0)
    def _():
        state_ref[...] = jnp.zeros(state_ref.shape, F32)

    row = lax.broadcasted_iota(I32, (c, c), 0)
    col = lax.broadcasted_iota(I32, (c, c), 1)
    causal = row >= col
    strict = row > col
    eye = (row == col).astype(F32)

    tril = causal.astype(BF16)
    gbs = [gb_ref[s] for s in range(n_seq)]
    gc_alls, gc_all_ts = [], []
    for gb_s in gbs:
        parts = _split3(gb_s)
        gc = _dot(tril, parts[0]) + _dot(tril, parts[1]) + _dot(tril, parts[2])
        gc_alls.append(gc)
        gc_all_ts.append(gc.T)
    ng = ng_ref[...]

    def chain_group(chains):
        heads = range(len(chains))
        seqs = [s for s, _ in chains]
        hid = [g for _, g in chains]
        lanes = [slice(g * A_DK, (g + 1) * A_DK) for g in hid]
        qh = [q_ref[seqs[h], :, lanes[h]] for h in heads]
        kh = [k_ref[seqs[h], :, lanes[h]] for h in heads]
        gc_col = [gc_alls[seqs[h]][:, hid[h]:hid[h] + 1] for h in heads]
        beta = [gbs[seqs[h]][:, A_HEADS + hid[h]:A_HEADS + hid[h] + 1] for h in heads]
        g_last = [gc_alls[seqs[h]][c - 1:c, hid[h]:hid[h] + 1] for h in heads]

        kq = [_dot_nt(jnp.concatenate([kh[h], qh[h]], axis=0), kh[h]) for h in heads]
        decay = [jnp.where(causal, jnp.exp(jnp.where(
            causal, gc_col[h] - gc_all_ts[seqs[h]][hid[h]:hid[h] + 1, :], 0.0)), 0.0) for h in heads]
        attn = [(kq[h][c:2 * c] * decay[h]).astype(BF16) for h in heads]

        p = [jnp.where(strict, -(kq[h][0:c] * beta[h] * decay[h]), 0.0) for h in heads]
        t_inv = [eye + p[h] for h in heads]
        for _ in range(n_doublings):
            pb = [p[h].astype(BF16) for h in heads]
            p = [_dot(pb[h], pb[h]) for h in heads]
            t_inv = [t_inv[h] + _dot(t_inv[h].astype(BF16), p[h].astype(BF16)) for h in heads]

        e_gc = [jnp.exp(gc_col[h]) for h in heads]
        khf = [kh[h].astype(F32) for h in heads]
        rhs = [jnp.concatenate([v_ref[seqs[h], :, lanes[h]].astype(F32) * beta[h],
                                khf[h] * (beta[h] * e_gc[h])], axis=1).astype(BF16)
               for h in heads]
        sol = [_dot(t_inv[h].astype(BF16), rhs[h]) for h in heads]

        slot = [seqs[h] * A_HEADS + hid[h] for h in heads]
        state = [state_ref[slot[h]] for h in heads]
        lhs = [jnp.concatenate([sol[h][:, A_DK:2 * A_DK], qh[h].astype(F32) * e_gc[h]],
                               axis=0).astype(BF16) for h in heads]
        ws = [_dot(lhs[h], state[h].astype(BF16)) for h in heads]
        v_new = [(sol[h][:, 0:A_DK] - ws[h][0:c]).astype(BF16) for h in heads]
        k_dec_t = [(khf[h] * jnp.exp(g_last[h] - gc_col[h])).T.astype(BF16) for h in heads]
        o = [ws[h][c:2 * c] + _dot(attn[h], v_new[h]) for h in heads]
        for h in heads:
            state_ref[slot[h]] = state[h] * jnp.exp(g_last[h]) + _dot(k_dec_t[h], v_new[h])
        for h in heads:
            on = o[h] * lax.rsqrt(jnp.mean(o[h] * o[h], -1, keepdims=True) + EPS) * ng
            gate = gate_ref[seqs[h], :, lanes[h]].astype(F32)
            o_ref[seqs[h], :, lanes[h]] = (on * _silu(gate)).astype(BF16)

    chains = [(s, g) for s in range(n_seq) for g in range(A_HEADS)]
    for first in range(0, len(chains), DN_CHAINS_PER_GROUP):
        chain_group(chains[first:first + DN_CHAINS_PER_GROUP])


def _dn_scan(q, k, v, gate, gb, norm_g_row, batch, seq):
    c = DN_CHUNK
    nc = seq // c
    n_seq = DN_SEQS_PER_STEP
    assert batch % n_seq == 0
    rows = lambda b, n: (b, n, 0)
    const = lambda b, n: (0, 0)
    wide = lambda a: a.reshape(batch, seq, a.shape[-1])
    out = pl.pallas_call(
        _dn_scan_kernel,
        grid=(batch // n_seq, nc),
        in_specs=[pl.BlockSpec((n_seq, c, D_MODEL), rows)] * 4
        + [pl.BlockSpec((n_seq, c, LANES), rows), pl.BlockSpec(norm_g_row.shape, const)],
        out_specs=pl.BlockSpec((n_seq, c, D_MODEL), rows),
        out_shape=jax.ShapeDtypeStruct((batch, seq, D_MODEL), BF16),
        scratch_shapes=[pltpu.VMEM((n_seq * A_HEADS, A_DK, A_DK), F32)],
        compiler_params=_cparams(("arbitrary", "arbitrary")),
        name="dn_scan",
    )(wide(q), wide(k), wide(v), wide(gate), wide(gb), norm_g_row)
    return out.reshape(batch * seq, D_MODEL)


def _at_in_kernel(x_ref, w_ref, qt_ref, k_ref, vt_ref):
    tm = k_ref.shape[1]
    xb = _load_tokens(x_ref, tm).astype(BF16)
    q = _dot(xb, w_ref[:, 0:D_MODEL]) * (B_DH ** -0.5 * LOG2E)
    lane = lax.broadcasted_iota(I32, q.shape, 1)
    first_map = (lane % (2 * B_DH)) < B_DH
    qt_ref[0] = jnp.where(first_map, q, 0.0).T.astype(BF16)
    qt_ref[1] = jnp.where(first_map, 0.0, q).T.astype(BF16)
    k = _dot(xb, w_ref[:, D_MODEL:2 * D_MODEL]).astype(BF16)
    vt = _dot(xb, w_ref[:, 2 * D_MODEL:3 * D_MODEL]).T.astype(BF16)
    ones = jnp.ones((V_ROWS - 2 * B_DH, tm), BF16)
    for h in range(B_HEADS):
        k_ref[h] = k[:, h * LANES:(h + 1) * LANES]
        vt_ref[h] = jnp.concatenate([vt[h * LANES:(h + 1) * LANES], ones], axis=0)


def _at_in(x2d, w_in):
    t = x2d.shape[0] // SLAB
    tm = ROW_TILE
    return pl.pallas_call(
        _at_in_kernel,
        grid=(t // tm,),
        in_specs=[pl.BlockSpec((tm * SLAB, LANES), lambda i: (i, 0)), pl.BlockSpec(w_in.shape, lambda i: (0, 0))],
        out_specs=[pl.BlockSpec((2, D_MODEL, tm), lambda i: (0, 0, i)),
                   pl.BlockSpec((B_HEADS, tm, LANES), lambda i: (0, i, 0)),
                   pl.BlockSpec((B_HEADS, V_ROWS, tm), lambda i: (0, 0, i))],
        out_shape=[jax.ShapeDtypeStruct((2, D_MODEL, t), BF16),
                   jax.ShapeDtypeStruct((B_HEADS, t, LANES), BF16),
                   jax.ShapeDtypeStruct((B_HEADS, V_ROWS, t), BF16)],
        compiler_params=_cparams(("parallel",)),
        name="at_in",
    )(x2d, w_in)


def _attn_kernel(scal_ref, q0_ref, q1_ref, q0n_ref, q1n_ref, k_ref, vt_ref, bias_ref, ng_ref, o_ref,
                 m_ref, acc_ref, sa_ref, sb_ref, sc_ref):
    tq = o_ref.shape[0]
    tk = tq
    i = pl.program_id(2)
    lam = scal_ref[0]
    out_scale = scal_ref[1]

    m_ref[...] = jnp.full(m_ref.shape, NEG_BIG, F32)
    acc_ref[...] = jnp.zeros(acc_ref.shape, F32)
    qts = (q0_ref[0], q1_ref[0])

    maps = range(2)

    def scores(j, s_ref, q_maps=qts):
        kb = k_ref[0, pl.ds(pl.multiple_of(j * tk, tk), tk), :]
        for mp in maps:
            s_ref[mp] = _dot(kb, q_maps[mp])

    def absorb(j, s_ref, bias):
        vtb = vt_ref[0, :, pl.ds(pl.multiple_of(j * tk, tk), tk)]
        s = [s_ref[mp] for mp in maps]
        if bias is not None:
            s = [x + bias for x in s]
        m_old = [m_ref[mp] for mp in maps]
        m_new = [jnp.maximum(m_old[mp], jnp.max(s[mp], 0, keepdims=True)) for mp in maps]
        alpha = [jnp.exp2(m_old[mp] - m_new[mp]) for mp in maps]
        p = [jnp.exp2(s[mp] - m_new[mp]) for mp in maps]
        pv = [_dot(vtb, p[mp].astype(BF16)) for mp in maps]
        for mp in maps:
            acc_ref[mp] = alpha[mp] * acc_ref[mp] + pv[mp]
            m_ref[mp] = m_new[mp]

    @pl.when(i == 0)
    def _():
        scores(0, sa_ref)

    @pl.when((i % 2 == 0) & (i >= 2))
    def _():
        scores(1, sb_ref)
        absorb(0, sc_ref, None)

    first = jnp.where(i % 2 == 1, 0, 1)

    def far_pair(jj, carry):
        j = first + 2 * jj
        scores(j + 1, sa_ref)
        absorb(j, sb_ref, None)
        scores(j + 2, sb_ref)
        absorb(j + 1, sa_ref, None)
        return carry

    lax.fori_loop(0, jnp.maximum(i - 1, 0) // 2, far_pair, 0)

    @pl.when(i >= 1)
    def _():
        scores(i, sa_ref)
        absorb(i - 1, sb_ref, bias_ref[0, 1])

    last = i == pl.num_programs(2) - 1
    next_qts = (q0n_ref[0], q1n_ref[0])

    @pl.when(last)
    def _():
        absorb(i, sa_ref, bias_ref[0, 0])

    @pl.when(jnp.logical_not(last) & (i % 2 == 0))
    def _():
        scores(0, sb_ref, next_qts)
        absorb(i, sa_ref, bias_ref[0, 0])

    @pl.when(jnp.logical_not(last) & (i % 2 == 1))
    def _():
        scores(0, sc_ref, next_qts)
        absorb(i, sa_ref, bias_ref[0, 0])

    dv = 2 * B_DH
    o_t = (acc_ref[0, 0:dv] * (1.0 / acc_ref[0, dv:dv + 1])
           - acc_ref[1, 0:dv] * (lam / acc_ref[1, dv:dv + 1]))
    o = o_t.T
    o = o * lax.rsqrt(jnp.mean(o * o, -1, keepdims=True) + EPS) * ng_ref[...] * out_scale
    o_ref[...] = o.astype(BF16)


def _attn(scalars, qt, k, vt, bias_tab, norm_g_row, batch, seq):
    tq = ATT_BLOCK
    nq = seq // tq
    t = batch * seq
    return pl.pallas_call(
        _attn_kernel,
        grid=(batch, B_HEADS, nq),
        in_specs=[
            pl.BlockSpec(memory_space=pltpu.SMEM),
            pl.BlockSpec((1, LANES, tq), lambda b, h, i: (0, h, b * nq + i)),
            pl.BlockSpec((1, LANES, tq), lambda b, h, i: (1, h, b * nq + i)),
            pl.BlockSpec((1, LANES, tq), lambda b, h, i: (0, h, b * nq + jnp.minimum(i + 1, nq - 1))),
            pl.BlockSpec((1, LANES, tq), lambda b, h, i: (1, h, b * nq + jnp.minimum(i + 1, nq - 1))),
            pl.BlockSpec((1, seq, LANES), lambda b, h, i: (h, b, 0)),
            pl.BlockSpec((1, V_ROWS, seq), lambda b, h, i: (h, 0, b)),
            pl.BlockSpec((1, 2, tq, tq), lambda b, h, i: (h, 0, 0, 0)),
            pl.BlockSpec(norm_g_row.shape, lambda b, h, i: (0, 0)),
        ],
        out_specs=pl.BlockSpec((tq, LANES), lambda b, h, i: (b * nq + i, h)),
        out_shape=jax.ShapeDtypeStruct((t, D_MODEL), BF16),
        scratch_shapes=[pltpu.VMEM((2, 1, tq), F32), pltpu.VMEM((2, V_ROWS, tq), F32),
                        pltpu.VMEM((2, tq, tq), F32), pltpu.VMEM((2, tq, tq), F32),
                        pltpu.VMEM((2, tq, tq), F32)],
        compiler_params=_cparams(("parallel", "parallel", "arbitrary")),
        name="attn",
    )(scalars, qt, qt, qt, qt, k, vt, bias_tab, norm_g_row)


def _rel_bucket(rel):
    n = jnp.maximum(rel, 0)
    max_exact = REL_BUCKETS // 2
    large = max_exact + (jnp.log(jnp.maximum(n, 1).astype(F32) / max_exact)
                         / math.log(REL_MAX_DIST / max_exact) * (REL_BUCKETS - max_exact)).astype(I32)
    large = jnp.minimum(large, REL_BUCKETS - 1)
    return jnp.where(n < max_exact, n, large)


def _attn_bias_table(rel_bias):
    tq = ATT_BLOCK
    assert tq + 1 >= REL_MAX_DIST, "blocks two back must fall in the last bucket"
    table = (rel_bias.astype(F32) - rel_bias.astype(F32)[REL_BUCKETS - 1]) * LOG2E
    period = 2 * tq
    idx = jnp.arange(period)
    diag_vec = jnp.where((idx < tq)[:, None], table[_rel_bucket(jnp.minimum(idx, tq - 1))], NEG_BIG)
    near_vec = table[_rel_bucket(jnp.where(idx < tq, idx + tq, idx - tq))]
    vecs = jnp.transpose(jnp.stack([diag_vec, near_vec], 0), (2, 0, 1))
    return pl.pallas_call(
        _toeplitz_kernel,
        grid=(B_HEADS, 2),
        in_specs=[pl.BlockSpec((1, 1, 1, period), lambda h, k: (h, k, 0, 0))],
        out_specs=pl.BlockSpec((1, 1, tq, tq), lambda h, k: (h, k, 0, 0)),
        out_shape=jax.ShapeDtypeStruct((B_HEADS, 2, tq, tq), F32),
        name="bias_table",
    )(vecs.reshape(B_HEADS, 2, 1, period))


def _toeplitz_kernel(v_ref, o_ref):
    tq = o_ref.shape[2]
    x = jnp.broadcast_to(v_ref[0, 0], (tq, v_ref.shape[3]))
    o_ref[0, 0] = pltpu.roll(x, 0, 1, stride=1, stride_axis=0)[:, :tq]


def _gelu(x):
    return 0.5 * x * (1.0 + lax.erf(x * (2.0 ** -0.5)))


def _gm_kernel(x_ref, w_ref, lng_ref, lnb_ref, ws_ref, bs_ref, o_ref):
    tm = o_ref.shape[0]
    xb = _load_tokens(x_ref, tm).astype(BF16)
    v = _gelu(_dot(xb, w_ref[:, C_HALF:2 * C_HALF]))
    vn = _layer_norm(v, lng_ref[...], lnb_ref[...]).astype(BF16)
    bs = bs_ref[...]
    for g in range(C_GROUPS):
        lo = g * C_GW
        u = _gelu(_dot(xb, w_ref[:, lo:lo + C_GW]))
        wg = ws_ref[g]
        for ch in range(tm // C_CHUNK):
            r0 = ch * C_CHUNK
            mixed = _dot(wg, vn[r0:r0 + C_CHUNK, lo:lo + C_GW]) + bs[:, g:g + 1]
            o_ref[r0:r0 + C_CHUNK, lo:lo + C_GW] = (u[r0:r0 + C_CHUNK] * mixed).astype(BF16)


def _gm(x2d, w_in, ln_g_row, ln_b_row, ws_causal, bs_t):
    t = x2d.shape[0] // SLAB
    tm = ROW_TILE
    rows = lambda i: (i, 0)
    const2 = lambda i: (0, 0)
    return pl.pallas_call(
        _gm_kernel,
        grid=(t // tm,),
        in_specs=[
            pl.BlockSpec((tm * SLAB, LANES), rows),
            pl.BlockSpec(w_in.shape, const2),
            pl.BlockSpec(ln_g_row.shape, const2),
            pl.BlockSpec(ln_b_row.shape, const2),
            pl.BlockSpec(ws_causal.shape, lambda i: (0, 0, 0)),
            pl.BlockSpec(bs_t.shape, const2),
        ],
        out_specs=pl.BlockSpec((tm, C_HALF), rows),
        out_shape=jax.ShapeDtypeStruct((t, C_HALF), BF16),
        compiler_params=_cparams(("parallel",)),
        name="gmlp",
    )(x2d, w_in, ln_g_row, ln_b_row, ws_causal, bs_t)


_PAIRS = ((0, 1), (0, 2), (0, 3), (1, 2), (1, 3), (2, 3))


def _out_kernel(y_ref, w_ref, h_ref, lng_ref, lnb_ref, wr_ref, rb_ref, hs_ref, meta_ref, cnt_ref,
                carry_ref):
    tm = y_ref.shape[0]

    @pl.when(pl.program_id(0) == 0)
    def _():
        carry_ref[...] = jnp.zeros(carry_ref.shape, F32)

    n_part = OUT_TILE_PARTS
    rows_per = tm // n_part
    parts = range(n_part)
    lng, lnb = lng_ref[...], lnb_ref[...]
    proj = [_dot(y_ref[k * rows_per:(k + 1) * rows_per, :], w_ref[...]) for k in parts]
    h1 = [_layer_norm(DN_ALPHA * _load_tokens(h_ref, rows_per, k * rows_per) + proj[k], lng, lnb)
          for k in parts]
    for k in parts:
        _store_tokens(hs_ref, h1[k], rows_per, k * rows_per)

    h_hi = [h1[k].astype(BF16) for k in parts]
    h_lo = [(h1[k] - h_hi[k].astype(F32)).astype(BF16) for k in parts]
    both = [_dot_nt(wr_ref[...], h_hi[k]) for k in parts]
    low = [_dot_nt(wr_ref[0:N_EXPERTS, :], h_lo[k]) for k in parts]
    logits = jnp.concatenate(
        [both[k][0:N_EXPERTS] + both[k][LANES:LANES + N_EXPERTS] + low[k] for k in parts], axis=1)
    sel = jax.nn.sigmoid(logits) + jnp.concatenate([rb_ref[...]] * (tm // LANES), axis=1)
    rows = [sel[e:e + 1, :] for e in range(N_EXPERTS)]

    def top2_sum(v):
        sums = [v[a] + v[b] for a, b in _PAIRS]
        return functools.reduce(jnp.maximum, sums)

    grp = [top2_sum(rows[g * EXPERTS_PER_GROUP:(g + 1) * EXPERTS_PER_GROUP]) for g in range(N_GROUPS)]
    best, best_val = jnp.zeros_like(grp[0]), grp[0]
    for g in range(1, N_GROUPS):
        better = grp[g] > best_val
        best = jnp.where(better, float(g), best)
        best_val = jnp.where(better, grp[g], best_val)
    member = []
    for k in range(EXPERTS_PER_GROUP):
        v = rows[k]
        for g in range(1, N_GROUPS):
            v = jnp.where(best == float(g), rows[g * EXPERTS_PER_GROUP + k], v)
        member.append(v)
    chosen = []
    for k in range(EXPERTS_PER_GROUP):
        beaten = jnp.zeros_like(best)
        for j in range(EXPERTS_PER_GROUP):
            if j != k:
                ahead = (member[j] >= member[k]) if j < k else (member[j] > member[k])
                beaten = beaten + ahead.astype(F32)
        chosen.append(beaten < 2.0)
    cls = best * float(len(_PAIRS))
    for idx, (a, b) in enumerate(_PAIRS):
        cls = cls + jnp.where(chosen[a] & chosen[b], float(idx), 0.0)

    n_cls_rows = carry_ref.shape[0]
    cls_row = lax.broadcasted_iota(I32, (n_cls_rows, tm), 0).astype(F32)
    onehot = (cls_row == cls).astype(F32)
    t_i = lax.broadcasted_iota(I32, (tm, tm), 0)
    t_j = lax.broadcasted_iota(I32, (tm, tm), 1)
    carry = carry_ref[...]
    before = _dot(onehot.astype(BF16), (t_i < t_j).astype(BF16)) + jnp.concatenate(
        [carry] * (tm // LANES), axis=1)
    rank = jnp.sum(onehot * before, 0, keepdims=True)
    carry_ref[...] = carry + jnp.sum(onehot, 1, keepdims=True)
    cnt_ref[...] = carry_ref[...]

    row_id = lax.broadcasted_iota(I32, meta_ref.shape, 0)
    meta_ref[...] = jnp.where(row_id == 0, cls, jnp.where(row_id == 1, rank, 0.0))


def _out_proj(y, w_out, hs, ln_g_row, ln_b_row, wr_split, rbias_row):
    t, kd = y.shape
    tm = OUT_TILE
    rows = lambda i: (i, 0)
    const = lambda i: (0, 0)
    return pl.pallas_call(
        _out_kernel,
        grid=(t // tm,),
        in_specs=[
            pl.BlockSpec((tm, kd), rows),
            pl.BlockSpec(w_out.shape, const),
            _token_spec(hs, tm, rows),
            pl.BlockSpec(ln_g_row.shape, const),
            pl.BlockSpec(ln_b_row.shape, const),
            pl.BlockSpec(wr_split.shape, const),
            pl.BlockSpec(rbias_row.shape, const),
        ],
        out_specs=[pl.BlockSpec((tm * SLAB, LANES), rows), pl.BlockSpec((SUBLANES, tm), lambda i: (0, i)),
                   pl.BlockSpec((CLASS_ROWS, LANES), const)],
        out_shape=[jax.ShapeDtypeStruct((t * SLAB, LANES), F32),
                   jax.ShapeDtypeStruct((SUBLANES, t), F32),
                   jax.ShapeDtypeStruct((CLASS_ROWS, LANES), F32)],
        scratch_shapes=[pltpu.VMEM((CLASS_ROWS, LANES), F32)],
        compiler_params=_cparams(("arbitrary",)),
        name="out_proj",
    )(y, w_out, hs, ln_g_row, ln_b_row, wr_split, rbias_row)


def _permute_kernel(pos_ref, src_ref, *refs, scatter, row_major_out):
    if row_major_out:
        o_ref, sem, dst_ref = refs[-3:]
    else:
        dst_ref, sem = refs[-2:]
    n_tok = pos_ref.shape[2]

    def issue(g, carry):
        for u in range(PERMUTE_UNROLL):
            r = g * PERMUTE_UNROLL + u
            here = pl.ds(pl.multiple_of(r * SLAB, SLAB), SLAB)
            there = pl.ds(pl.multiple_of(pos_ref[0, 0, r] * SLAB, SLAB), SLAB)
            src_rows, dst_rows = (here, there) if scatter else (there, here)
            pltpu.make_async_copy(src_ref.at[src_rows, :], dst_ref.at[dst_rows, :], sem).start(
                priority=u % 2)
        return carry

    lax.fori_loop(0, n_tok // PERMUTE_UNROLL, issue, 0)
    whole = pl.ds(0, n_tok * SLAB)
    pltpu.make_async_copy(src_ref.at[whole, :], dst_ref.at[whole, :], sem).wait()
    if row_major_out:
        o_ref[...] = _load_tokens(dst_ref, n_tok)


def _permute(pos, src, n_dst_rows, scatter, row_major_out=False, zeroed_dst=None):
    n_tok = pos.shape[2]
    any_spec = pl.BlockSpec(memory_space=pl.ANY)
    tile_spec = pl.BlockSpec((n_tok * SLAB, LANES), lambda i: (i, 0))
    out_shape = jax.ShapeDtypeStruct((n_dst_rows, LANES), F32)
    scratch = [pltpu.SemaphoreType.DMA(())]
    if row_major_out:
        assert not scatter
        tile_spec = pl.BlockSpec((n_tok, D_MODEL), lambda i: (i, 0))
        out_shape = jax.ShapeDtypeStruct((n_dst_rows // SLAB, D_MODEL), F32)
        scratch.append(pltpu.VMEM((n_tok * SLAB, LANES), F32))
    extra = []
    if scatter:
        extra = [jnp.zeros((n_dst_rows, LANES), F32) if zeroed_dst is None else zeroed_dst]
    return pl.pallas_call(
        functools.partial(_permute_kernel, scatter=scatter, row_major_out=row_major_out),
        grid=(pos.shape[0],),
        in_specs=[pl.BlockSpec((1, 1, n_tok), lambda i: (i, 0, 0), memory_space=pltpu.SMEM),
                  tile_spec if scatter else any_spec] + [any_spec] * len(extra),
        out_specs=any_spec if scatter else tile_spec,
        out_shape=out_shape,
        scratch_shapes=scratch,
        input_output_aliases={2: 0} if scatter else {},
        compiler_params=_cparams(("arbitrary",)),
        name="dispatch" if scatter else "unpermute",
    )(pos, src, *extra)


def _moe_kernel(lo_ref, hi_ref, nv_ref, xs_ref, w1l_ref, w3l_ref, w2l_ref, w1h_ref, w3h_ref, w2h_ref,
                wrl_ref, wrh_ref, lng_ref, lnb_ref, ys_ref, zeros_ref, wl_bf_ref, wh_bf_ref, w2_bf_ref):
    zeros_ref[...] = jnp.zeros(zeros_ref.shape, F32)
    tm = xs_ref.shape[0] // SLAB

    i = pl.program_id(0)
    prev = jnp.maximum(i - 1, 0)
    valid = i < nv_ref[0]

    @pl.when(valid & ((i == 0) | (lo_ref[i] != lo_ref[prev])))
    def _():
        wl_bf_ref[0] = w1l_ref[0, 0].astype(BF16)
        wl_bf_ref[1] = w3l_ref[0, 0].astype(BF16)
        w2_bf_ref[0] = w2l_ref[0, 0].astype(BF16)

    @pl.when(valid & ((i == 0) | (hi_ref[i] != hi_ref[prev])))
    def _():
        wh_bf_ref[0] = w1h_ref[0, 0].astype(BF16)
        wh_bf_ref[1] = w3h_ref[0, 0].astype(BF16)
        w2_bf_ref[1] = w2h_ref[0, 0].astype(BF16)

    @pl.when(valid)
    def _():
        x = _load_tokens(xs_ref, tm)
        xb = x.astype(BF16)

        up = (wl_bf_ref, wh_bf_ref)
        gate_in = [_dot(xb, up[e][0]) for e in range(2)]
        lin_in = [_dot(xb, up[e][1]) for e in range(2)]
        hid = [(_silu(gate_in[e]) * lin_in[e]).astype(BF16) for e in range(2)]
        y = [_dot(hid[e], w2_bf_ref[e]) for e in range(2)]

        s_lo = jax.nn.sigmoid(jnp.sum(x * wrl_ref[0], -1, keepdims=True))
        s_hi = jax.nn.sigmoid(jnp.sum(x * wrh_ref[0], -1, keepdims=True))
        denom = s_lo + s_hi
        m = (s_lo / denom) * y[0] + (s_hi / denom) * y[1]
        _store_tokens(ys_ref, _layer_norm(DN_ALPHA * x + m, lng_ref[...], lnb_ref[...]), tm)

    @pl.when(pl.program_id(0) >= nv_ref[0])
    def _():
        ys_ref[...] = jnp.zeros(ys_ref.shape, F32)


def _moe(tile_lo, tile_hi, n_valid, xs, w1, w3, w2, layer, wr_rows, ln_g_row, ln_b_row):
    tm = MOE_TILE
    n_tiles = xs.shape[0] // (tm * SLAB)
    rows = lambda i, lo, hi, nv: (jnp.minimum(i, nv[0] - 1), 0)
    w_lo = lambda i, lo, hi, nv: (layer, lo[i], 0, 0)
    w_hi = lambda i, lo, hi, nv: (layer, hi[i], 0, 0)
    r_lo = lambda i, lo, hi, nv: (lo[i], 0, 0)
    r_hi = lambda i, lo, hi, nv: (hi[i], 0, 0)
    const = lambda i, lo, hi, nv: (0, 0)
    up = (1, 1, D_MODEL, D_EXPERT)
    down = (1, 1, D_EXPERT, D_MODEL)
    router_row = (1, 1, D_MODEL)
    grid_spec = pltpu.PrefetchScalarGridSpec(
        num_scalar_prefetch=3,
        grid=(n_tiles,),
        in_specs=[
            pl.BlockSpec((tm * SLAB, LANES), rows),
            pl.BlockSpec(up, w_lo), pl.BlockSpec(up, w_lo), pl.BlockSpec(down, w_lo),
            pl.BlockSpec(up, w_hi), pl.BlockSpec(up, w_hi), pl.BlockSpec(down, w_hi),
            pl.BlockSpec(router_row, r_lo), pl.BlockSpec(router_row, r_hi),
            pl.BlockSpec(ln_g_row.shape, const), pl.BlockSpec(ln_b_row.shape, const),
        ],
        out_specs=[pl.BlockSpec((tm * SLAB, LANES), lambda i, lo, hi, nv: (i, 0))] * 2,
        scratch_shapes=[pltpu.VMEM((2, D_MODEL, D_EXPERT), BF16), pltpu.VMEM((2, D_MODEL, D_EXPERT), BF16),
                        pltpu.VMEM((2, D_EXPERT, D_MODEL), BF16)],
    )
    return pl.pallas_call(
        _moe_kernel,
        grid_spec=grid_spec,
        out_shape=[jax.ShapeDtypeStruct(xs.shape, F32)] * 2,
        compiler_params=_cparams(("arbitrary",)),
        name="moe",
    )(tile_lo, tile_hi, n_valid, xs, w1, w3, w2, w1, w3, w2, wr_rows, wr_rows, ln_g_row, ln_b_row)


def _routing_plan(meta, counts, n_slots):
    cls = meta[0].astype(I32)
    rank = meta[1].astype(I32)
    cnt = counts[:N_PAIR_CLASSES, 0].astype(I32)
    padded = ((cnt + MOE_TILE - 1) // MOE_TILE) * MOE_TILE
    end = jnp.cumsum(padded)
    start = end - padded
    classes = jnp.arange(N_PAIR_CLASSES, dtype=I32)
    pos = jnp.sum(jnp.where(cls[:, None] == classes[None, :], start[None, :], 0), -1) + rank
    n_tiles = n_slots // MOE_TILE
    n_valid = (end[-1] // MOE_TILE).astype(I32).reshape(1)
    tile_start = jnp.minimum(jnp.arange(n_tiles, dtype=I32), n_valid[0] - 1) * MOE_TILE
    tile_cls = jnp.sum((tile_start[:, None] >= end[None, :]).astype(I32), -1)
    onehot = (tile_cls[:, None] == classes[None, :]).astype(I32)
    tile_lo = jnp.sum(onehot * jnp.asarray(_PAIR_LO)[None, :], -1)
    tile_hi = jnp.sum(onehot * jnp.asarray(_PAIR_HI)[None, :], -1)
    return pos, tile_lo, tile_hi, n_valid


def _moe_layer(hs, meta, counts, w1, w3, w2, layer, wr_rows, ln_g_row, ln_b_row, row_major_out,
               zeroed_slots):
    t = meta.shape[1]
    n_slots = t + N_PAIR_CLASSES * MOE_TILE
    pos, tile_lo, tile_hi, n_valid = _routing_plan(meta, counts, n_slots)
    pos = pos.reshape(t // PERMUTE_TOKENS, 1, PERMUTE_TOKENS)
    xs = _permute(pos, hs, n_slots * SLAB, scatter=True, zeroed_dst=zeroed_slots)
    ys, zeroed_next = _moe(tile_lo, tile_hi, n_valid, xs, w1, w3, w2, layer, wr_rows, ln_g_row, ln_b_row)
    return _permute(pos, ys, t * SLAB, scatter=False, row_major_out=row_major_out), zeroed_next


def _row(v, width=None):
    v = v.astype(F32).reshape(1, -1)
    if width is not None and v.shape[1] < width:
        v = jnp.pad(v, ((0, 0), (0, width - v.shape[1])))
    return v


def _deltanet(h2d, batch, seq, w_in, conv_w, a_log, dt_bias, norm_g):
    qkvg = 4 * D_MODEL
    w_big = w_in.astype(BF16)
    w_ab = jnp.pad(w_in[:, qkvg:], ((0, 0), (0, LANES - 2 * A_HEADS))).astype(BF16)
    q, k, v, gate, gb = _dn_in(h2d, batch, seq, w_big, w_ab, conv_w.astype(F32),
                               _row(a_log, LANES), _row(dt_bias, LANES))
    return _dn_scan(q, k, v, gate, gb, _row(norm_g), batch, seq)


def _diff_attention(h2d, batch, seq, w_in, lam, norm_g, rel_bias, lambda_init):
    lamf = lam.astype(F32)
    lam_full = jnp.exp(jnp.sum(lamf[0] * lamf[1])) - jnp.exp(jnp.sum(lamf[2] * lamf[3])) + lambda_init
    scalars = jnp.stack([lam_full, jnp.asarray(1.0 - lambda_init, F32)]).astype(F32)
    qt, k, vt = _at_in(h2d, w_in.astype(BF16))
    return _attn(scalars, qt, k, vt, _attn_bias_table(rel_bias), _row(norm_g), batch, seq)


def _gmlp(h2d, w_in, ln_g, ln_b, w_s, b_s):
    ws_causal = jnp.tril(w_s.astype(F32)).astype(BF16)
    bs_t = jnp.pad(jnp.transpose(b_s.astype(F32)), ((0, 0), (0, LANES - C_GROUPS)))
    return _gm(h2d, w_in.astype(BF16), _row(ln_g), _row(ln_b), ws_causal, bs_t)


def kernel(x, ln_mix_g, ln_mix_b, ln_ffn_g, ln_ffn_b, a_w_in, a_conv, a_a_log, a_dt_bias, a_norm_g, a_w_out, b_w_in, b_lambda, b_norm_g, b_w_out, rel_bias, c_w_in, c_ln_g, c_ln_b, c_w_s, c_b_s, c_w_out, w_router, router_bias, e_w1, e_w3, e_w2):
    batch, seq, d = x.shape
    assert d == D_MODEL and seq % max(ROW_TILE, ATT_BLOCK, DN_CHUNK) == 0
    t = batch * seq
    h = x.astype(F32).reshape(t, d)

    wr_rows = jnp.transpose(w_router.astype(F32)).reshape(N_EXPERTS, 1, D_MODEL)
    wr = jnp.pad(jnp.transpose(w_router.astype(F32)), ((0, LANES - N_EXPERTS), (0, 0)))
    wr_hi = wr.astype(BF16)
    wr_lo = (wr - wr_hi.astype(F32)).astype(BF16)
    wr_split = jnp.concatenate([wr_hi, wr_lo], axis=0)
    rbias_row = jnp.broadcast_to(router_bias.astype(F32)[:, None], (N_EXPERTS, LANES))

    w1_all, w3_all, w2_all = e_w1, e_w3, e_w2

    zeroed_slots = None
    for i in range(DEPTH):
        kind, j = i % N_MIXERS, i // N_MIXERS
        if kind == 0:
            y = _deltanet(h, batch, seq, a_w_in[j], a_conv[j], a_a_log[j], a_dt_bias[j], a_norm_g[j])
            w_out = a_w_out[j]
        elif kind == 1:
            lambda_init = 0.8 - 0.6 * math.exp(-0.3 * i)
            y = _diff_attention(h, batch, seq, b_w_in[j], b_lambda[j], b_norm_g[j], rel_bias, lambda_init)
            w_out = b_w_out[j]
        else:
            y = _gmlp(h, c_w_in[j], c_ln_g[j], c_ln_b[j], c_w_s[j], c_b_s[j])
            w_out = c_w_out[j]
        hs, meta, counts = _out_proj(y, w_out.astype(BF16), h, _row(ln_mix_g[i]), _row(ln_mix_b[i]),
                                     wr_split, rbias_row)
        h, zeroed_slots = _moe_layer(hs, meta, counts, w1_all, w3_all, w2_all, i, wr_rows,
                                     _row(ln_ffn_g[i]), _row(ln_ffn_b[i]),
                                     row_major_out=(i == DEPTH - 1), zeroed_slots=zeroed_slots)
    return h.reshape(batch, seq, d).astype(x.dtype)
```

```python
import functools
import math

import jax
import jax.numpy as jnp
import numpy as np
from jax import lax
from jax.experimental import pallas as pl
from jax.experimental.pallas import tpu as pltpu

F32 = jnp.float32
BF16 = jnp.bfloat16
I32 = jnp.int32

D_MODEL = 1024
DEPTH = 4
N_MIXERS = 3
A_HEADS = 8
A_DK = 128
A_CONV = 4
B_DH = 64
B_HEADS = D_MODEL // (2 * B_DH)
REL_BUCKETS = 32
REL_MAX_DIST = 128
C_CHUNK = 128
C_HALF = 2 * D_MODEL
C_GROUPS = 8
C_GW = C_HALF // C_GROUPS
N_EXPERTS = 16
N_GROUPS = 4
EXPERTS_PER_GROUP = 4
D_EXPERT = 512
DN_ALPHA = (2 * DEPTH) ** 0.25
EPS = 1e-5

LANES = 128
SUBLANES = 8
BF16_ROWS = 16
VMEM_LIMIT_BYTES = 56 * 1024 * 1024
SLAB = D_MODEL // LANES

ROW_TILE = 512
OUT_TILE = 512
OUT_TILE_PARTS = 2
DN_CHUNK = 128
DN_SEQS_PER_STEP = 2
DN_CHAINS_PER_GROUP = 16
ATT_BLOCK = 512
V_ROWS = 2 * B_DH + BF16_ROWS
MOE_TILE = 512
MOE_TILE_PARTS = 2
PERMUTE_TOKENS = 1024
PERMUTE_UNROLL = 8
_PAIRS = ((0, 1), (0, 2), (0, 3), (1, 2), (1, 3), (2, 3))
N_PAIR_CLASSES = N_GROUPS * len(_PAIRS)
CLASS_ROWS = -(-N_PAIR_CLASSES // SUBLANES) * SUBLANES
NEG_BIG = -1e30
LOG2E = math.log2(math.e)

_PAIR_LO = np.array([g * EXPERTS_PER_GROUP + a for g in range(N_GROUPS) for a, _ in _PAIRS], np.int32)
_PAIR_HI = np.array([g * EXPERTS_PER_GROUP + b for g in range(N_GROUPS) for _, b in _PAIRS], np.int32)


def _cparams(sem):
    return pltpu.CompilerParams(dimension_semantics=sem, vmem_limit_bytes=VMEM_LIMIT_BYTES)


def _layer_norm(x, g, b):
    mu = jnp.mean(x, -1, keepdims=True)
    xc = x - mu
    var = jnp.mean(xc * xc, -1, keepdims=True)
    return xc * lax.rsqrt(var + EPS) * g + b


def _silu(x):
    return x * jax.nn.sigmoid(x)


def _dot(a, b):
    return jnp.dot(a, b, preferred_element_type=F32)


def _dot_nt(a, b):
    return lax.dot_general(a, b, (((1,), (1,)), ((), ())), preferred_element_type=F32)


def _load_tokens(ref, n_tok, first=0):
    if ref.shape[-1] == D_MODEL:
        return ref[first:first + n_tok, :]
    return jnp.concatenate([ref[pl.ds(first * SLAB + s, n_tok, stride=SLAB), :] for s in range(SLAB)],
                           axis=1)


def _token_spec(arr, n_tok, index_map):
    if arr.shape[-1] == D_MODEL:
        return pl.BlockSpec((n_tok, D_MODEL), index_map)
    return pl.BlockSpec((n_tok * SLAB, LANES), index_map)


def _store_tokens(ref, val, n_tok, first=0):
    for s in range(SLAB):
        ref[pl.ds(first * SLAB + s, n_tok, stride=SLAB), :] = val[:, s * LANES:(s + 1) * LANES]


def _split3(x):
    hi = x.astype(BF16)
    r = x - hi.astype(F32)
    mid = r.astype(BF16)
    lo = (r - mid.astype(F32)).astype(BF16)
    return hi, mid, lo


def _dn_in_kernel(x_ref, w_ref, wab_ref, conv_ref, alog_ref, dtb_ref,
                  q_ref, k_ref, v_ref, gate_ref, gb_ref, ext_ref):
    ts = q_ref.shape[0]
    qkv_w = 3 * D_MODEL
    n_slabs = qkv_w // LANES
    s_idx = pl.program_id(1)

    @pl.when(s_idx == 0)
    def _():
        ext_ref[:, 0:SUBLANES, :] = jnp.zeros((n_slabs, SUBLANES, LANES), F32)

    @pl.when(s_idx > 0)
    def _():
        ext_ref[:, 0:SUBLANES, :] = ext_ref[:, ts:ts + SUBLANES, :]

    xb = _load_tokens(x_ref, ts).astype(BF16)
    gate_ref[...] = _dot(xb, w_ref[:, qkv_w:qkv_w + D_MODEL]).astype(BF16)
    cw = conv_ref[...]
    for part, out_ref in enumerate((q_ref, k_ref, v_ref)):
        pre = _dot(xb, w_ref[:, part * D_MODEL:(part + 1) * D_MODEL])
        for h in range(A_HEADS):
            slab = part * A_HEADS + h
            lo = h * A_DK
            ext_ref[slab, SUBLANES:SUBLANES + ts, :] = pre[:, lo:lo + A_DK]
            col = slice(slab * LANES, (slab + 1) * LANES)
            y = ext_ref[slab, pl.ds(SUBLANES - 3, ts), :] * cw[0:1, col]
            for j in range(1, A_CONV):
                y = y + ext_ref[slab, pl.ds(SUBLANES - 3 + j, ts), :] * cw[j:j + 1, col]
            y = _silu(y)
            if part == 0:
                y = y * (lax.rsqrt(jnp.sum(y * y, -1, keepdims=True) + 1e-6) * (A_DK ** -0.5))
            elif part == 1:
                y = y * lax.rsqrt(jnp.sum(y * y, -1, keepdims=True) + 1e-6)
            out_ref[:, lo:lo + A_DK] = y.astype(BF16)

    ab = _dot(xb, wab_ref[...])
    z = ab + dtb_ref[...]
    softplus = jnp.maximum(z, 0.0) + jnp.log1p(jnp.exp(-jnp.abs(z)))
    g = -jnp.exp(alog_ref[...]) * softplus
    lane = lax.broadcasted_iota(I32, ab.shape, 1)
    gb_ref[...] = jnp.where(lane < A_HEADS, g, jax.nn.sigmoid(ab))


def _dn_in(x2d, batch, seq, w_big, w_ab, conv_w, alog_row, dtb_row):
    ts = ROW_TILE
    ns = seq // ts
    rows = lambda b, s: (b * ns + s, 0)
    const = lambda b, s: (0, 0)
    t = batch * seq
    out_bf = jax.ShapeDtypeStruct((t, D_MODEL), BF16)
    return pl.pallas_call(
        _dn_in_kernel,
        grid=(batch, ns),
        in_specs=[
            _token_spec(x2d, ts, rows),
            pl.BlockSpec(w_big.shape, const),
            pl.BlockSpec(w_ab.shape, const),
            pl.BlockSpec(conv_w.shape, const),
            pl.BlockSpec(alog_row.shape, const),
            pl.BlockSpec(dtb_row.shape, const),
        ],
        out_specs=[pl.BlockSpec((ts, D_MODEL), rows)] * 4 + [pl.BlockSpec((ts, LANES), rows)],
        out_shape=[out_bf, out_bf, out_bf, out_bf, jax.ShapeDtypeStruct((t, LANES), F32)],
        scratch_shapes=[pltpu.VMEM((3 * D_MODEL // LANES, ts + 2 * SUBLANES, LANES), F32)],
        compiler_params=_cparams(("arbitrary", "arbitrary")),
        name="dn_in",
    )(x2d, w_big, w_ab, conv_w, alog_row, dtb_row)


def _dn_scan_kernel(q_ref, k_ref, v_ref, gate_ref, gb_ref, ng_ref, o_ref, state_ref):
    n_seq, c = q_ref.shape[0], q_ref.shape[1]
    n_doublings = int(math.log2(c)) - 1

    @pl.when(pl.program_id(1) == 0)
    def _():
        state_ref[...] = jnp.zeros(state_ref.shape, F32)

    row = lax.broadcasted_iota(I32, (c, c), 0)
    col = lax.broadcasted_iota(I32, (c, c), 1)
    causal = row >= col
    strict = row > col
    eye = (row == col).astype(F32)

    tril = causal.astype(BF16)
    gbs = [gb_ref[s] for s in range(n_seq)]
    gc_alls, gc_all_ts = [], []
    for gb_s in gbs:
        parts = _split3(gb_s)
        gc = _dot(tril, parts[0]) + _dot(tril, parts[1]) + _dot(tril, parts[2])
        gc_alls.append(gc)
        gc_all_ts.append(gc.T)
    ng = ng_ref[...]

    def chain_group(chains):
        heads = range(len(chains))
        seqs = [s for s, _ in chains]
        hid = [g for _, g in chains]
        lanes = [slice(g * A_DK, (g + 1) * A_DK) for g in hid]
        qh = [q_ref[seqs[h], :, lanes[h]] for h in heads]
        kh = [k_ref[seqs[h], :, lanes[h]] for h in heads]
        gc_col = [gc_alls[seqs[h]][:, hid[h]:hid[h] + 1] for h in heads]
        beta = [gbs[seqs[h]][:, A_HEADS + hid[h]:A_HEADS + hid[h] + 1] for h in heads]
        g_last = [gc_alls[seqs[h]][c - 1:c, hid[h]:hid[h] + 1] for h in heads]

        kq = [_dot_nt(jnp.concatenate([kh[h], qh[h]], axis=0), kh[h]) for h in heads]
        decay = [jnp.where(causal, jnp.exp(jnp.where(
            causal, gc_col[h] - gc_all_ts[seqs[h]][hid[h]:hid[h] + 1, :], 0.0)), 0.0) for h in heads]
        attn = [(kq[h][c:2 * c] * decay[h]).astype(BF16) for h in heads]

        p = [jnp.where(strict, -(kq[h][0:c] * beta[h] * decay[h]), 0.0) for h in heads]
        t_inv = [eye + p[h] for h in heads]
        for _ in range(n_doublings):
            pb = [p[h].astype(BF16) for h in heads]
            p = [_dot(pb[h], pb[h]) for h in heads]
            t_inv = [t_inv[h] + _dot(t_inv[h].astype(BF16), p[h].astype(BF16)) for h in heads]

        e_gc = [jnp.exp(gc_col[h]) for h in heads]
        khf = [kh[h].astype(F32) for h in heads]
        rhs = [jnp.concatenate([v_ref[seqs[h], :, lanes[h]].astype(F32) * beta[h],
                                khf[h] * (beta[h] * e_gc[h])], axis=1).astype(BF16)
               for h in heads]
        sol = [_dot(t_inv[h].astype(BF16), rhs[h]) for h in heads]

        slot = [seqs[h] * A_HEADS + hid[h] for h in heads]
        state = [state_ref[slot[h]] for h in heads]
        lhs = [jnp.concatenate([sol[h][:, A_DK:2 * A_DK], qh[h].astype(F32) * e_gc[h]],
                               axis=0).astype(BF16) for h in heads]
        ws = [_dot(lhs[h], state[h].astype(BF16)) for h in heads]
        v_new = [(sol[h][:, 0:A_DK] - ws[h][0:c]).astype(BF16) for h in heads]
        k_dec_t = [(khf[h] * jnp.exp(g_last[h] - gc_col[h])).T.astype(BF16) for h in heads]
        o = [ws[h][c:2 * c] + _dot(attn[h], v_new[h]) for h in heads]
        for h in heads:
            state_ref[slot[h]] = state[h] * jnp.exp(g_last[h]) + _dot(k_dec_t[h], v_new[h])
        for h in heads:
            on = o[h] * lax.rsqrt(jnp.mean(o[h] * o[h], -1, keepdims=True) + EPS) * ng
            gate = gate_ref[seqs[h], :, lanes[h]].astype(F32)
            o_ref[seqs[h], :, lanes[h]] = (on * _silu(gate)).astype(BF16)

    chains = [(s, g) for s in range(n_seq) for g in range(A_HEADS)]
    for first in range(0, len(chains), DN_CHAINS_PER_GROUP):
        chain_group(chains[first:first + DN_CHAINS_PER_GROUP])


def _dn_scan(q, k, v, gate, gb, norm_g_row, batch, seq):
    c = DN_CHUNK
    nc = seq // c
    n_seq = DN_SEQS_PER_STEP
    assert batch % n_seq == 0
    rows = lambda b, n: (b, n, 0)
    const = lambda b, n: (0, 0)
    wide = lambda a: a.reshape(batch, seq, a.shape[-1])
    out = pl.pallas_call(
        _dn_scan_kernel,
        grid=(batch // n_seq, nc),
        in_specs=[pl.BlockSpec((n_seq, c, D_MODEL), rows)] * 4
        + [pl.BlockSpec((n_seq, c, LANES), rows), pl.BlockSpec(norm_g_row.shape, const)],
        out_specs=pl.BlockSpec((n_seq, c, D_MODEL), rows),
        out_shape=jax.ShapeDtypeStruct((batch, seq, D_MODEL), BF16),
        scratch_shapes=[pltpu.VMEM((n_seq * A_HEADS, A_DK, A_DK), F32)],
        compiler_params=_cparams(("arbitrary", "arbitrary")),
        name="dn_scan",
    )(wide(q), wide(k), wide(v), wide(gate), wide(gb), norm_g_row)
    return out.reshape(batch * seq, D_MODEL)


def _at_in_kernel(x_ref, w_ref, qt_ref, k_ref, vt_ref):
    tm = k_ref.shape[1]
    xb = _load_tokens(x_ref, tm).astype(BF16)
    q = _dot(xb, w_ref[:, 0:D_MODEL]) * (B_DH ** -0.5 * LOG2E)
    lane = lax.broadcasted_iota(I32, q.shape, 1)
    first_map = (lane % (2 * B_DH)) < B_DH
    qt_ref[0] = jnp.where(first_map, q, 0.0).T.astype(BF16)
    qt_ref[1] = jnp.where(first_map, 0.0, q).T.astype(BF16)
    k = _dot(xb, w_ref[:, D_MODEL:2 * D_MODEL]).astype(BF16)
    vt = _dot(xb, w_ref[:, 2 * D_MODEL:3 * D_MODEL]).T.astype(BF16)
    ones = jnp.ones((V_ROWS - 2 * B_DH, tm), BF16)
    for h in range(B_HEADS):
        k_ref[h] = k[:, h * LANES:(h + 1) * LANES]
        vt_ref[h] = jnp.concatenate([vt[h * LANES:(h + 1) * LANES], ones], axis=0)


def _at_in(x2d, w_in):
    t = x2d.shape[0] // SLAB
    tm = ROW_TILE
    return pl.pallas_call(
        _at_in_kernel,
        grid=(t // tm,),
        in_specs=[pl.BlockSpec((tm * SLAB, LANES), lambda i: (i, 0)), pl.BlockSpec(w_in.shape, lambda i: (0, 0))],
        out_specs=[pl.BlockSpec((2, D_MODEL, tm), lambda i: (0, 0, i)),
                   pl.BlockSpec((B_HEADS, tm, LANES), lambda i: (0, i, 0)),
                   pl.BlockSpec((B_HEADS, V_ROWS, tm), lambda i: (0, 0, i))],
        out_shape=[jax.ShapeDtypeStruct((2, D_MODEL, t), BF16),
                   jax.ShapeDtypeStruct((B_HEADS, t, LANES), BF16),
                   jax.ShapeDtypeStruct((B_HEADS, V_ROWS, t), BF16)],
        compiler_params=_cparams(("parallel",)),
        name="at_in",
    )(x2d, w_in)


def _attn_kernel(scal_ref, q0_ref, q1_ref, q0n_ref, q1n_ref, k_ref, vt_ref, bias_ref, ng_ref, o_ref,
                 m_ref, acc_ref, sa_ref, sb_ref, sc_ref):
    tq = o_ref.shape[0]
    tk = tq
    i = pl.program_id(2)
    lam = scal_ref[0]
    out_scale = scal_ref[1]

    m_ref[...] = jnp.full(m_ref.shape, NEG_BIG, F32)
    acc_ref[...] = jnp.zeros(acc_ref.shape, F32)
    qts = (q0_ref[0], q1_ref[0])

    maps = range(2)

    def scores(j, s_ref, q_maps=qts):
        kb = k_ref[0, pl.ds(pl.multiple_of(j * tk, tk), tk), :]
        for mp in maps:
            s_ref[mp] = _dot(kb, q_maps[mp])

    def absorb(j, s_ref, bias):
        vtb = vt_ref[0, :, pl.ds(pl.multiple_of(j * tk, tk), tk)]
        s = [s_ref[mp] for mp in maps]
        if bias is not None:
            s = [x + bias for x in s]
        m_old = [m_ref[mp] for mp in maps]
        m_new = [jnp.maximum(m_old[mp], jnp.max(s[mp], 0, keepdims=True)) for mp in maps]
        alpha = [jnp.exp2(m_old[mp] - m_new[mp]) for mp in maps]
        p = [jnp.exp2(s[mp] - m_new[mp]) for mp in maps]
        pv = [_dot(vtb, p[mp].astype(BF16)) for mp in maps]
        for mp in maps:
            acc_ref[mp] = alpha[mp] * acc_ref[mp] + pv[mp]
            m_ref[mp] = m_new[mp]

    @pl.when(i == 0)
    def _():
        scores(0, sa_ref)

    @pl.when((i % 2 == 0) & (i >= 2))
    def _():
        scores(1, sb_ref)
        absorb(0, sc_ref, None)

    first = jnp.where(i % 2 == 1, 0, 1)

    def far_pair(jj, carry):
        j = first + 2 * jj
        scores(j + 1, sa_ref)
        absorb(j, sb_ref, None)
        scores(j + 2, sb_ref)
        absorb(j + 1, sa_ref, None)
        return carry

    lax.fori_loop(0, jnp.maximum(i - 1, 0) // 2, far_pair, 0)

    @pl.when(i >= 1)
    def _():
        scores(i, sa_ref)
        absorb(i - 1, sb_ref, bias_ref[0, 1])

    last = i == pl.num_programs(2) - 1
    next_qts = (q0n_ref[0], q1n_ref[0])

    @pl.when(last)
    def _():
        absorb(i, sa_ref, bias_ref[0, 0])

    @pl.when(jnp.logical_not(last) & (i % 2 == 0))
    def _():
        scores(0, sb_ref, next_qts)
        absorb(i, sa_ref, bias_ref[0, 0])

    @pl.when(jnp.logical_not(last) & (i % 2 == 1))
    def _():
        scores(0, sc_ref, next_qts)
        absorb(i, sa_ref, bias_ref[0, 0])

    dv = 2 * B_DH
    o_t = (acc_ref[0, 0:dv] * (1.0 / acc_ref[0, dv:dv + 1])
           - acc_ref[1, 0:dv] * (lam / acc_ref[1, dv:dv + 1]))
    o = o_t.T
    o = o * lax.rsqrt(jnp.mean(o * o, -1, keepdims=True) + EPS) * ng_ref[...] * out_scale
    o_ref[...] = o.astype(BF16)


def _attn(scalars, qt, k, vt, bias_tab, norm_g_row, batch, seq):
    tq = ATT_BLOCK
    nq = seq // tq
    t = batch * seq
    return pl.pallas_call(
        _attn_kernel,
        grid=(batch, B_HEADS, nq),
        in_specs=[
            pl.BlockSpec(memory_space=pltpu.SMEM),
            pl.BlockSpec((1, LANES, tq), lambda b, h, i: (0, h, b * nq + i)),
            pl.BlockSpec((1, LANES, tq), lambda b, h, i: (1, h, b * nq + i)),
            pl.BlockSpec((1, LANES, tq), lambda b, h, i: (0, h, b * nq + jnp.minimum(i + 1, nq - 1))),
            pl.BlockSpec((1, LANES, tq), lambda b, h, i: (1, h, b * nq + jnp.minimum(i + 1, nq - 1))),
            pl.BlockSpec((1, seq, LANES), lambda b, h, i: (h, b, 0)),
            pl.BlockSpec((1, V_ROWS, seq), lambda b, h, i: (h, 0, b)),
            pl.BlockSpec((1, 2, tq, tq), lambda b, h, i: (h, 0, 0, 0)),
            pl.BlockSpec(norm_g_row.shape, lambda b, h, i: (0, 0)),
        ],
        out_specs=pl.BlockSpec((tq, LANES), lambda b, h, i: (b * nq + i, h)),
        out_shape=jax.ShapeDtypeStruct((t, D_MODEL), BF16),
        scratch_shapes=[pltpu.VMEM((2, 1, tq), F32), pltpu.VMEM((2, V_ROWS, tq), F32),
                        pltpu.VMEM((2, tq, tq), F32), pltpu.VMEM((2, tq, tq), F32),
                        pltpu.VMEM((2, tq, tq), F32)],
        compiler_params=_cparams(("parallel", "parallel", "arbitrary")),
        name="attn",
    )(scalars, qt, qt, qt, qt, k, vt, bias_tab, norm_g_row)


def _rel_bucket(rel):
    n = jnp.maximum(rel, 0)
    max_exact = REL_BUCKETS // 2
    large = max_exact + (jnp.log(jnp.maximum(n, 1).astype(F32) / max_exact)
                         / math.log(REL_MAX_DIST / max_exact) * (REL_BUCKETS - max_exact)).astype(I32)
    large = jnp.minimum(large, REL_BUCKETS - 1)
    return jnp.where(n < max_exact, n, large)


def _attn_bias_table(rel_bias):
    tq = ATT_BLOCK
    assert tq + 1 >= REL_MAX_DIST, "blocks two back must fall in the last bucket"
    table = (rel_bias.astype(F32) - rel_bias.astype(F32)[REL_BUCKETS - 1]) * LOG2E
    period = 2 * tq
    idx = jnp.arange(period)
    diag_vec = jnp.where((idx < tq)[:, None], table[_rel_bucket(jnp.minimum(idx, tq - 1))], NEG_BIG)
    near_vec = table[_rel_bucket(jnp.where(idx < tq, idx + tq, idx - tq))]
    vecs = jnp.transpose(jnp.stack([diag_vec, near_vec], 0), (2, 0, 1))
    return pl.pallas_call(
        _toeplitz_kernel,
        grid=(B_HEADS, 2),
        in_specs=[pl.BlockSpec((1, 1, 1, period), lambda h, k: (h, k, 0, 0))],
        out_specs=pl.BlockSpec((1, 1, tq, tq), lambda h, k: (h, k, 0, 0)),
        out_shape=jax.ShapeDtypeStruct((B_HEADS, 2, tq, tq), F32),
        name="bias_table",
    )(vecs.reshape(B_HEADS, 2, 1, period))


def _toeplitz_kernel(v_ref, o_ref):
    tq = o_ref.shape[2]
    x = jnp.broadcast_to(v_ref[0, 0], (tq, v_ref.shape[3]))
    o_ref[0, 0] = pltpu.roll(x, 0, 1, stride=1, stride_axis=0)[:, :tq]


def _gelu(x):
    return 0.5 * x * (1.0 + lax.erf(x * (2.0 ** -0.5)))


def _gm_kernel(x_ref, w_ref, lng_ref, lnb_ref, ws_ref, bs_ref, o_ref):
    tm = o_ref.shape[0]
    xb = _load_tokens(x_ref, tm).astype(BF16)
    v = _gelu(_dot(xb, w_ref[:, C_HALF:2 * C_HALF]))
    vn = _layer_norm(v, lng_ref[...], lnb_ref[...]).astype(BF16)
    bs = bs_ref[...]
    for g in range(C_GROUPS):
        lo = g * C_GW
        u = _gelu(_dot(xb, w_ref[:, lo:lo + C_GW]))
        wg = ws_ref[g]
        for ch in range(tm // C_CHUNK):
            r0 = ch * C_CHUNK
            mixed = _dot(wg, vn[r0:r0 + C_CHUNK, lo:lo + C_GW]) + bs[:, g:g + 1]
            o_ref[r0:r0 + C_CHUNK, lo:lo + C_GW] = (u[r0:r0 + C_CHUNK] * mixed).astype(BF16)


def _gm(x2d, w_in, ln_g_row, ln_b_row, ws_causal, bs_t):
    t = x2d.shape[0] // SLAB
    tm = ROW_TILE
    rows = lambda i: (i, 0)
    const2 = lambda i: (0, 0)
    return pl.pallas_call(
        _gm_kernel,
        grid=(t // tm,),
        in_specs=[
            pl.BlockSpec((tm * SLAB, LANES), rows),
            pl.BlockSpec(w_in.shape, const2),
            pl.BlockSpec(ln_g_row.shape, const2),
            pl.BlockSpec(ln_b_row.shape, const2),
            pl.BlockSpec(ws_causal.shape, lambda i: (0, 0, 0)),
            pl.BlockSpec(bs_t.shape, const2),
        ],
        out_specs=pl.BlockSpec((tm, C_HALF), rows),
        out_shape=jax.ShapeDtypeStruct((t, C_HALF), BF16),
        compiler_params=_cparams(("parallel",)),
        name="gmlp",
    )(x2d, w_in, ln_g_row, ln_b_row, ws_causal, bs_t)


def _out_kernel(y_ref, w_ref, h_ref, lng_ref, lnb_ref, wr_ref, rb_ref, hs_ref, meta_ref, cnt_ref,
                carry_ref):
    tm = y_ref.shape[0]

    @pl.when(pl.program_id(0) == 0)
    def _():
        carry_ref[...] = jnp.zeros(carry_ref.shape, F32)

    n_part = OUT_TILE_PARTS
    rows_per = tm // n_part
    parts = range(n_part)
    lng, lnb = lng_ref[...], lnb_ref[...]
    proj = [_dot(y_ref[k * rows_per:(k + 1) * rows_per, :], w_ref[...]) for k in parts]
    h1 = [_layer_norm(DN_ALPHA * _load_tokens(h_ref, rows_per, k * rows_per) + proj[k], lng, lnb)
          for k in parts]
    for k in parts:
        _store_tokens(hs_ref, h1[k], rows_per, k * rows_per)

    h_hi = [h1[k].astype(BF16) for k in parts]
    h_lo = [(h1[k] - h_hi[k].astype(F32)).astype(BF16) for k in parts]
    both = [_dot_nt(wr_ref[...], h_hi[k]) for k in parts]
    low = [_dot_nt(wr_ref[0:N_EXPERTS, :], h_lo[k]) for k in parts]
    logits = jnp.concatenate(
        [both[k][0:N_EXPERTS] + both[k][LANES:LANES + N_EXPERTS] + low[k] for k in parts], axis=1)
    sel = jax.nn.sigmoid(logits) + jnp.concatenate([rb_ref[...]] * (tm // LANES), axis=1)
    rows = [sel[e:e + 1, :] for e in range(N_EXPERTS)]

    def top2_sum(v):
        sums = [v[a] + v[b] for a, b in _PAIRS]
        return functools.reduce(jnp.maximum, sums)

    grp = [top2_sum(rows[g * EXPERTS_PER_GROUP:(g + 1) * EXPERTS_PER_GROUP]) for g in range(N_GROUPS)]
    best, best_val = jnp.zeros_like(grp[0]), grp[0]
    for g in range(1, N_GROUPS):
        better = grp[g] > best_val
        best = jnp.where(better, float(g), best)
        best_val = jnp.where(better, grp[g], best_val)
    member = []
    for k in range(EXPERTS_PER_GROUP):
        v = rows[k]
        for g in range(1, N_GROUPS):
            v = jnp.where(best == float(g), rows[g * EXPERTS_PER_GROUP + k], v)
        member.append(v)
    chosen = []
    for k in range(EXPERTS_PER_GROUP):
        beaten = jnp.zeros_like(best)
        for j in range(EXPERTS_PER_GROUP):
            if j != k:
                ahead = (member[j] >= member[k]) if j < k else (member[j] > member[k])
                beaten = beaten + ahead.astype(F32)
        chosen.append(beaten < 2.0)
    cls = best * float(len(_PAIRS))
    for idx, (a, b) in enumerate(_PAIRS):
        cls = cls + jnp.where(chosen[a] & chosen[b], float(idx), 0.0)

    n_cls_rows = carry_ref.shape[0]
    cls_row = lax.broadcasted_iota(I32, (n_cls_rows, tm), 0).astype(F32)
    onehot = (cls_row == cls).astype(F32)
    t_i = lax.broadcasted_iota(I32, (tm, tm), 0)
    t_j = lax.broadcasted_iota(I32, (tm, tm), 1)
    carry = carry_ref[...]
    before = _dot(onehot.astype(BF16), (t_i < t_j).astype(BF16)) + jnp.concatenate(
        [carry] * (tm // LANES), axis=1)
    rank = jnp.sum(onehot * before, 0, keepdims=True)
    carry_ref[...] = carry + jnp.sum(onehot, 1, keepdims=True)
    cnt_ref[...] = carry_ref[...]

    row_id = lax.broadcasted_iota(I32, meta_ref.shape, 0)
    meta_ref[...] = jnp.where(row_id == 0, cls, jnp.where(row_id == 1, rank, 0.0))


def _out_proj(y, w_out, hs, ln_g_row, ln_b_row, wr_split, rbias_row):
    t, kd = y.shape
    tm = OUT_TILE
    rows = lambda i: (i, 0)
    const = lambda i: (0, 0)
    return pl.pallas_call(
        _out_kernel,
        grid=(t // tm,),
        in_specs=[
            pl.BlockSpec((tm, kd), rows),
            pl.BlockSpec(w_out.shape, const),
            _token_spec(hs, tm, rows),
            pl.BlockSpec(ln_g_row.shape, const),
            pl.BlockSpec(ln_b_row.shape, const),
            pl.BlockSpec(wr_split.shape, const),
            pl.BlockSpec(rbias_row.shape, const),
        ],
        out_specs=[pl.BlockSpec((tm * SLAB, LANES), rows), pl.BlockSpec((SUBLANES, tm), lambda i: (0, i)),
                   pl.BlockSpec((CLASS_ROWS, LANES), const)],
        out_shape=[jax.ShapeDtypeStruct((t * SLAB, LANES), F32),
                   jax.ShapeDtypeStruct((SUBLANES, t), F32),
                   jax.ShapeDtypeStruct((CLASS_ROWS, LANES), F32)],
        scratch_shapes=[pltpu.VMEM((CLASS_ROWS, LANES), F32)],
        compiler_params=_cparams(("arbitrary",)),
        name="out_proj",
    )(y, w_out, hs, ln_g_row, ln_b_row, wr_split, rbias_row)


def _permute_kernel(pos_ref, src_ref, *refs, scatter, row_major_out):
    if row_major_out:
        o_ref, sem, dst_ref = refs[-3:]
    else:
        dst_ref, sem = refs[-2:]
    n_tok = pos_ref.shape[2]

    def issue(g, carry):
        for u in range(PERMUTE_UNROLL):
            r = g * PERMUTE_UNROLL + u
            here = pl.ds(pl.multiple_of(r * SLAB, SLAB), SLAB)
            there = pl.ds(pl.multiple_of(pos_ref[0, 0, r] * SLAB, SLAB), SLAB)
            src_rows, dst_rows = (here, there) if scatter else (there, here)
            pltpu.make_async_copy(src_ref.at[src_rows, :], dst_ref.at[dst_rows, :], sem).start(
                priority=u % 2)
        return carry

    lax.fori_loop(0, n_tok // PERMUTE_UNROLL, issue, 0)
    whole = pl.ds(0, n_tok * SLAB)
    pltpu.make_async_copy(src_ref.at[whole, :], dst_ref.at[whole, :], sem).wait()
    if row_major_out:
        o_ref[...] = _load_tokens(dst_ref, n_tok)


def _permute(pos, src, n_dst_rows, scatter, row_major_out=False, zeroed_dst=None):
    n_tok = pos.shape[2]
    any_spec = pl.BlockSpec(memory_space=pl.ANY)
    tile_spec = pl.BlockSpec((n_tok * SLAB, LANES), lambda i: (i, 0))
    out_shape = jax.ShapeDtypeStruct((n_dst_rows, LANES), F32)
    scratch = [pltpu.SemaphoreType.DMA(())]
    if row_major_out:
        assert not scatter
        tile_spec = pl.BlockSpec((n_tok, D_MODEL), lambda i: (i, 0))
        out_shape = jax.ShapeDtypeStruct((n_dst_rows // SLAB, D_MODEL), F32)
        scratch.append(pltpu.VMEM((n_tok * SLAB, LANES), F32))
    extra = []
    if scatter:
        extra = [jnp.zeros((n_dst_rows, LANES), F32) if zeroed_dst is None else zeroed_dst]
    return pl.pallas_call(
        functools.partial(_permute_kernel, scatter=scatter, row_major_out=row_major_out),
        grid=(pos.shape[0],),
        in_specs=[pl.BlockSpec((1, 1, n_tok), lambda i: (i, 0, 0), memory_space=pltpu.SMEM),
                  tile_spec if scatter else any_spec] + [any_spec] * len(extra),
        out_specs=any_spec if scatter else tile_spec,
        out_shape=out_shape,
        scratch_shapes=scratch,
        input_output_aliases={2: 0} if scatter else {},
        compiler_params=_cparams(("arbitrary",)),
        name="dispatch" if scatter else "unpermute",
    )(pos, src, *extra)


def _moe_kernel(lo_ref, hi_ref, nv_ref, xs_ref, w1l_ref, w3l_ref, w2l_ref, w1h_ref, w3h_ref, w2h_ref,
                wrl_ref, wrh_ref, lng_ref, lnb_ref, ys_ref, zeros_ref, wl_bf_ref, wh_bf_ref, w2_bf_ref):
    zeros_ref[...] = jnp.zeros(zeros_ref.shape, F32)
    tm = xs_ref.shape[0] // SLAB

    i = pl.program_id(0)
    prev = jnp.maximum(i - 1, 0)
    valid = i < nv_ref[0]

    @pl.when(valid & ((i == 0) | (lo_ref[i] != lo_ref[prev])))
    def _():
        wl_bf_ref[0] = w1l_ref[0, 0].astype(BF16)
        wl_bf_ref[1] = w3l_ref[0, 0].astype(BF16)
        w2_bf_ref[0] = w2l_ref[0, 0].astype(BF16)

    @pl.when(valid & ((i == 0) | (hi_ref[i] != hi_ref[prev])))
    def _():
        wh_bf_ref[0] = w1h_ref[0, 0].astype(BF16)
        wh_bf_ref[1] = w3h_ref[0, 0].astype(BF16)
        w2_bf_ref[1] = w2h_ref[0, 0].astype(BF16)

    @pl.when(valid)
    def _():
        n_grp = MOE_TILE_PARTS
        rows = tm // n_grp
        up = (wl_bf_ref, wh_bf_ref)
        chains = [(g, e) for g in range(n_grp) for e in range(2)]
        x = [_load_tokens(xs_ref, rows, g * rows) for g in range(n_grp)]
        xb = [x[g].astype(BF16) for g in range(n_grp)]
        gate_in = {(g, e): _dot(xb[g], up[e][0]) for g, e in chains}
        lin_in = {(g, e): _dot(xb[g], up[e][1]) for g, e in chains}
        hid = {c: (_silu(gate_in[c]) * lin_in[c]).astype(BF16) for c in chains}
        y = {(g, e): _dot(hid[(g, e)], w2_bf_ref[e]) for g, e in chains}
        for g in range(n_grp):
            s_lo = jax.nn.sigmoid(jnp.sum(x[g] * wrl_ref[0], -1, keepdims=True))
            s_hi = jax.nn.sigmoid(jnp.sum(x[g] * wrh_ref[0], -1, keepdims=True))
            denom = s_lo + s_hi
            m = (s_lo / denom) * y[(g, 0)] + (s_hi / denom) * y[(g, 1)]
            out = _layer_norm(DN_ALPHA * x[g] + m, lng_ref[...], lnb_ref[...])
            _store_tokens(ys_ref, out, rows, g * rows)

    @pl.when(pl.program_id(0) >= nv_ref[0])
    def _():
        ys_ref[...] = jnp.zeros(ys_ref.shape, F32)


def _moe(tile_lo, tile_hi, n_valid, xs, w1, w3, w2, layer, wr_rows, ln_g_row, ln_b_row):
    tm = MOE_TILE
    n_tiles = xs.shape[0] // (tm * SLAB)
    rows = lambda i, lo, hi, nv: (jnp.minimum(i, nv[0] - 1), 0)
    w_lo = lambda i, lo, hi, nv: (layer, lo[i], 0, 0)
    w_hi = lambda i, lo, hi, nv: (layer, hi[i], 0, 0)
    r_lo = lambda i, lo, hi, nv: (lo[i], 0, 0)
    r_hi = lambda i, lo, hi, nv: (hi[i], 0, 0)
    const = lambda i, lo, hi, nv: (0, 0)
    up = (1, 1, D_MODEL, D_EXPERT)
    down = (1, 1, D_EXPERT, D_MODEL)
    router_row = (1, 1, D_MODEL)
    grid_spec = pltpu.PrefetchScalarGridSpec(
        num_scalar_prefetch=3,
        grid=(n_tiles,),
        in_specs=[
            pl.BlockSpec((tm * SLAB, LANES), rows),
            pl.BlockSpec(up, w_lo), pl.BlockSpec(up, w_lo), pl.BlockSpec(down, w_lo),
            pl.BlockSpec(up, w_hi), pl.BlockSpec(up, w_hi), pl.BlockSpec(down, w_hi),
            pl.BlockSpec(router_row, r_lo), pl.BlockSpec(router_row, r_hi),
            pl.BlockSpec(ln_g_row.shape, const), pl.BlockSpec(ln_b_row.shape, const),
        ],
        out_specs=[pl.BlockSpec((tm * SLAB, LANES), lambda i, lo, hi, nv: (i, 0))] * 2,
        scratch_shapes=[pltpu.VMEM((2, D_MODEL, D_EXPERT), BF16), pltpu.VMEM((2, D_MODEL, D_EXPERT), BF16),
                        pltpu.VMEM((2, D_EXPERT, D_MODEL), BF16)],
    )
    return pl.pallas_call(
        _moe_kernel,
        grid_spec=grid_spec,
        out_shape=[jax.ShapeDtypeStruct(xs.shape, F32)] * 2,
        compiler_params=_cparams(("arbitrary",)),
        name="moe",
    )(tile_lo, tile_hi, n_valid, xs, w1, w3, w2, w1, w3, w2, wr_rows, wr_rows, ln_g_row, ln_b_row)


def _routing_plan(meta, counts, n_slots):
    cls = meta[0].astype(I32)
    rank = meta[1].astype(I32)
    cnt = counts[:N_PAIR_CLASSES, 0].astype(I32)
    padded = ((cnt + MOE_TILE - 1) // MOE_TILE) * MOE_TILE
    end = jnp.cumsum(padded)
    start = end - padded
    classes = jnp.arange(N_PAIR_CLASSES, dtype=I32)
    pos = jnp.sum(jnp.where(cls[:, None] == classes[None, :], start[None, :], 0), -1) + rank
    n_tiles = n_slots // MOE_TILE
    n_valid = (end[-1] // MOE_TILE).astype(I32).reshape(1)
    tile_start = jnp.minimum(jnp.arange(n_tiles, dtype=I32), n_valid[0] - 1) * MOE_TILE
    tile_cls = jnp.sum((tile_start[:, None] >= end[None, :]).astype(I32), -1)
    onehot = (tile_cls[:, None] == classes[None, :]).astype(I32)
    tile_lo = jnp.sum(onehot * jnp.asarray(_PAIR_LO)[None, :], -1)
    tile_hi = jnp.sum(onehot * jnp.asarray(_PAIR_HI)[None, :], -1)
    return pos, tile_lo, tile_hi, n_valid


def _moe_layer(hs, meta, counts, w1, w3, w2, layer, wr_rows, ln_g_row, ln_b_row, row_major_out,
               zeroed_slots):
    t = meta.shape[1]
    n_slots = t + N_PAIR_CLASSES * MOE_TILE
    pos, tile_lo, tile_hi, n_valid = _routing_plan(meta, counts, n_slots)
    pos = pos.reshape(t // PERMUTE_TOKENS, 1, PERMUTE_TOKENS)
    xs = _permute(pos, hs, n_slots * SLAB, scatter=True, zeroed_dst=zeroed_slots)
    ys, zeroed_next = _moe(tile_lo, tile_hi, n_valid, xs, w1, w3, w2, layer, wr_rows, ln_g_row, ln_b_row)
    return _permute(pos, ys, t * SLAB, scatter=False, row_major_out=row_major_out), zeroed_next


def _row(v, width=None):
    v = v.astype(F32).reshape(1, -1)
    if width is not None and v.shape[1] < width:
        v = jnp.pad(v, ((0, 0), (0, width - v.shape[1])))
    return v


def _deltanet(h2d, batch, seq, w_in, conv_w, a_log, dt_bias, norm_g):
    qkvg = 4 * D_MODEL
    w_big = w_in.astype(BF16)
    w_ab = jnp.pad(w_in[:, qkvg:], ((0, 0), (0, LANES - 2 * A_HEADS))).astype(BF16)
    q, k, v, gate, gb = _dn_in(h2d, batch, seq, w_big, w_ab, conv_w.astype(F32),
                               _row(a_log, LANES), _row(dt_bias, LANES))
    return _dn_scan(q, k, v, gate, gb, _row(norm_g), batch, seq)


def _diff_attention(h2d, batch, seq, w_in, lam, norm_g, rel_bias, lambda_init):
    lamf = lam.astype(F32)
    lam_full = jnp.exp(jnp.sum(lamf[0] * lamf[1])) - jnp.exp(jnp.sum(lamf[2] * lamf[3])) + lambda_init
    scalars = jnp.stack([lam_full, jnp.asarray(1.0 - lambda_init, F32)]).astype(F32)
    qt, k, vt = _at_in(h2d, w_in.astype(BF16))
    return _attn(scalars, qt, k, vt, _attn_bias_table(rel_bias), _row(norm_g), batch, seq)


def _gmlp(h2d, w_in, ln_g, ln_b, w_s, b_s):
    ws_causal = jnp.tril(w_s.astype(F32)).astype(BF16)
    bs_t = jnp.pad(jnp.transpose(b_s.astype(F32)), ((0, 0), (0, LANES - C_GROUPS)))
    return _gm(h2d, w_in.astype(BF16), _row(ln_g), _row(ln_b), ws_causal, bs_t)


def kernel(x, ln_mix_g, ln_mix_b, ln_ffn_g, ln_ffn_b, a_w_in, a_conv, a_a_log, a_dt_bias, a_norm_g, a_w_out, b_w_in, b_lambda, b_norm_g, b_w_out, rel_bias, c_w_in, c_ln_g, c_ln_b, c_w_s, c_b_s, c_w_out, w_router, router_bias, e_w1, e_w3, e_w2):
    batch, seq, d = x.shape
    assert d == D_MODEL and seq % max(ROW_TILE, ATT_BLOCK, DN_CHUNK) == 0
    t = batch * seq
    h = x.astype(F32).reshape(t, d)

    wr_rows = jnp.transpose(w_router.astype(F32)).reshape(N_EXPERTS, 1, D_MODEL)
    wr = jnp.pad(jnp.transpose(w_router.astype(F32)), ((0, LANES - N_EXPERTS), (0, 0)))
    wr_hi = wr.astype(BF16)
    wr_lo = (wr - wr_hi.astype(F32)).astype(BF16)
    wr_split = jnp.concatenate([wr_hi, wr_lo], axis=0)
    rbias_row = jnp.broadcast_to(router_bias.astype(F32)[:, None], (N_EXPERTS, LANES))

    w1_all, w3_all, w2_all = e_w1, e_w3, e_w2

    zeroed_slots = None
    for i in range(DEPTH):
        kind, j = i % N_MIXERS, i // N_MIXERS
        if kind == 0:
            y = _deltanet(h, batch, seq, a_w_in[j], a_conv[j], a_a_log[j], a_dt_bias[j], a_norm_g[j])
            w_out = a_w_out[j]
        elif kind == 1:
            lambda_init = 0.8 - 0.6 * math.exp(-0.3 * i)
            y = _diff_attention(h, batch, seq, b_w_in[j], b_lambda[j], b_norm_g[j], rel_bias, lambda_init)
            w_out = b_w_out[j]
        else:
            y = _gmlp(h, c_w_in[j], c_ln_g[j], c_ln_b[j], c_w_s[j], c_b_s[j])
            w_out = c_w_out[j]
        hs, meta, counts = _out_proj(y, w_out.astype(BF16), h, _row(ln_mix_g[i]), _row(ln_mix_b[i]),
                                     wr_split, rbias_row)
        h, zeroed_slots = _moe_layer(hs, meta, counts, w1_all, w3_all, w2_all, i, wr_rows,
                                     _row(ln_ffn_g[i]), _row(ln_ffn_b[i]),
                                     row_major_out=(i == DEPTH - 1), zeroed_slots=zeroed_slots)
    return h.reshape(batch, seq, d).astype(x.dtype)
```

```python
import functools
import math

import jax
import jax.numpy as jnp
import numpy as np
from jax import lax
from jax.experimental import pallas as pl
from jax.experimental.pallas import tpu as pltpu

F32 = jnp.float32
BF16 = jnp.bfloat16
I32 = jnp.int32

D_MODEL = 1024
DEPTH = 4
N_MIXERS = 3
A_HEADS = 8
A_DK = 128
A_CONV = 4
B_DH = 64
B_HEADS = D_MODEL // (2 * B_DH)
REL_BUCKETS = 32
REL_MAX_DIST = 128
C_CHUNK = 128
C_HALF = 2 * D_MODEL
C_GROUPS = 8
C_GW = C_HALF // C_GROUPS
N_EXPERTS = 16
N_GROUPS = 4
EXPERTS_PER_GROUP = 4
D_EXPERT = 512
DN_ALPHA = (2 * DEPTH) ** 0.25
EPS = 1e-5

LANES = 128
SUBLANES = 8
BF16_ROWS = 16
VMEM_LIMIT_BYTES = 56 * 1024 * 1024
SLAB = D_MODEL // LANES

ROW_TILE = 512
OUT_TILE = 1024
OUT_TILE_PARTS = 2
DN_CHUNK = 128
DN_SEQS_PER_STEP = 2
DN_CHAINS_PER_GROUP = 16
ATT_BLOCK = 512
V_ROWS = 2 * B_DH + BF16_ROWS
MOE_TILE = 512
MOE_TILE_PARTS = 2
PERMUTE_TOKENS = 2048
PERMUTE_UNROLL = 8
_PAIRS = ((0, 1), (0, 2), (0, 3), (1, 2), (1, 3), (2, 3))
N_PAIR_CLASSES = N_GROUPS * len(_PAIRS)
CLASS_ROWS = -(-N_PAIR_CLASSES // SUBLANES) * SUBLANES
NEG_BIG = -1e30
LOG2E = math.log2(math.e)

_PAIR_LO = np.array([g * EXPERTS_PER_GROUP + a for g in range(N_GROUPS) for a, _ in _PAIRS], np.int32)
_PAIR_HI = np.array([g * EXPERTS_PER_GROUP + b for g in range(N_GROUPS) for _, b in _PAIRS], np.int32)


def _cparams(sem):
    return pltpu.CompilerParams(dimension_semantics=sem, vmem_limit_bytes=VMEM_LIMIT_BYTES)


def _layer_norm(x, g, b):
    mu = jnp.mean(x, -1, keepdims=True)
    xc = x - mu
    var = jnp.mean(xc * xc, -1, keepdims=True)
    return xc * lax.rsqrt(var + EPS) * g + b


def _silu(x):
    return x * jax.nn.sigmoid(x)


def _dot(a, b):
    return jnp.dot(a, b, preferred_element_type=F32)


def _dot_nt(a, b):
    return lax.dot_general(a, b, (((1,), (1,)), ((), ())), preferred_element_type=F32)


def _load_tokens(ref, n_tok, first=0):
    if ref.shape[-1] == D_MODEL:
        return ref[first:first + n_tok, :]
    return jnp.concatenate([ref[pl.ds(first * SLAB + s, n_tok, stride=SLAB), :] for s in range(SLAB)],
                           axis=1)


def _token_spec(arr, n_tok, index_map):
    if arr.shape[-1] == D_MODEL:
        return pl.BlockSpec((n_tok, D_MODEL), index_map)
    return pl.BlockSpec((n_tok * SLAB, LANES), index_map)


def _store_tokens(ref, val, n_tok, first=0):
    for s in range(SLAB):
        ref[pl.ds(first * SLAB + s, n_tok, stride=SLAB), :] = val[:, s * LANES:(s + 1) * LANES]


def _split3(x):
    hi = x.astype(BF16)
    r = x - hi.astype(F32)
    mid = r.astype(BF16)
    lo = (r - mid.astype(F32)).astype(BF16)
    return hi, mid, lo


def _dn_in_kernel(x_ref, w_ref, wab_ref, conv_ref, alog_ref, dtb_ref,
                  q_ref, k_ref, v_ref, gate_ref, gb_ref, ext_ref):
    ts = q_ref.shape[0]
    qkv_w = 3 * D_MODEL
    n_slabs = qkv_w // LANES
    s_idx = pl.program_id(1)

    @pl.when(s_idx == 0)
    def _():
        ext_ref[:, 0:SUBLANES, :] = jnp.zeros((n_slabs, SUBLANES, LANES), F32)

    @pl.when(s_idx > 0)
    def _():
        ext_ref[:, 0:SUBLANES, :] = ext_ref[:, ts:ts + SUBLANES, :]

    xb = _load_tokens(x_ref, ts).astype(BF16)
    gate_ref[...] = _dot(xb, w_ref[:, qkv_w:qkv_w + D_MODEL]).astype(BF16)
    cw = conv_ref[...]
    for part, out_ref in enumerate((q_ref, k_ref, v_ref)):
        pre = _dot(xb, w_ref[:, part * D_MODEL:(part + 1) * D_MODEL])
        for h in range(A_HEADS):
            slab = part * A_HEADS + h
            lo = h * A_DK
            ext_ref[slab, SUBLANES:SUBLANES + ts, :] = pre[:, lo:lo + A_DK]
            col = slice(slab * LANES, (slab + 1) * LANES)
            y = ext_ref[slab, pl.ds(SUBLANES - 3, ts), :] * cw[0:1, col]
            for j in range(1, A_CONV):
                y = y + ext_ref[slab, pl.ds(SUBLANES - 3 + j, ts), :] * cw[j:j + 1, col]
            y = _silu(y)
            if part == 0:
                y = y * (lax.rsqrt(jnp.sum(y * y, -1, keepdims=True) + 1e-6) * (A_DK ** -0.5))
            elif part == 1:
                y = y * lax.rsqrt(jnp.sum(y * y, -1, keepdims=True) + 1e-6)
            out_ref[:, lo:lo + A_DK] = y.astype(BF16)

    ab = _dot(xb, wab_ref[...])
    z = ab + dtb_ref[...]
    softplus = jnp.maximum(z, 0.0) + jnp.log1p(jnp.exp(-jnp.abs(z)))
    g = -jnp.exp(alog_ref[...]) * softplus
    lane = lax.broadcasted_iota(I32, ab.shape, 1)
    gb_ref[...] = jnp.where(lane < A_HEADS, g, jax.nn.sigmoid(ab))


def _dn_in(x2d, batch, seq, w_big, w_ab, conv_w, alog_row, dtb_row):
    ts = ROW_TILE
    ns = seq // ts
    rows = lambda b, s: (b * ns + s, 0)
    const = lambda b, s: (0, 0)
    t = batch * seq
    out_bf = jax.ShapeDtypeStruct((t, D_MODEL), BF16)
    return pl.pallas_call(
        _dn_in_kernel,
        grid=(batch, ns),
        in_specs=[
            _token_spec(x2d, ts, rows),
            pl.BlockSpec(w_big.shape, const),
            pl.BlockSpec(w_ab.shape, const),
            pl.BlockSpec(conv_w.shape, const),
            pl.BlockSpec(alog_row.shape, const),
            pl.BlockSpec(dtb_row.shape, const),
        ],
        out_specs=[pl.BlockSpec((ts, D_MODEL), rows)] * 4 + [pl.BlockSpec((ts, LANES), rows)],
        out_shape=[out_bf, out_bf, out_bf, out_bf, jax.ShapeDtypeStruct((t, LANES), F32)],
        scratch_shapes=[pltpu.VMEM((3 * D_MODEL // LANES, ts + 2 * SUBLANES, LANES), F32)],
        compiler_params=_cparams(("arbitrary", "arbitrary")),
        name="dn_in",
    )(x2d, w_big, w_ab, conv_w, alog_row, dtb_row)


def _dn_scan_kernel(q_ref, k_ref, v_ref, gate_ref, gb_ref, ng_ref, o_ref, state_ref):
    n_seq, c = q_ref.shape[0], q_ref.shape[1]
    n_doublings = int(math.log2(c)) - 1

    @pl.when(pl.program_id(1) == 0)
    def _():
        state_ref[...] = jnp.zeros(state_ref.shape, F32)

    row = lax.broadcasted_iota(I32, (c, c), 0)
    col = lax.broadcasted_iota(I32, (c, c), 1)
    causal = row >= col
    strict = row > col
    eye = (row == col).astype(F32)

    tril = causal.astype(BF16)
    gbs = [gb_ref[s] for s in range(n_seq)]
    gc_alls, gc_all_ts = [], []
    for gb_s in gbs:
        parts = _split3(gb_s)
        gc = _dot(tril, parts[0]) + _dot(tril, parts[1]) + _dot(tril, parts[2])
        gc_alls.append(gc)
        gc_all_ts.append(gc.T)
    ng = ng_ref[...]

    def chain_group(chains):
        heads = range(len(chains))
        seqs = [s for s, _ in chains]
        hid = [g for _, g in chains]
        lanes = [slice(g * A_DK, (g + 1) * A_DK) for g in hid]
        qh = [q_ref[seqs[h], :, lanes[h]] for h in heads]
        kh = [k_ref[seqs[h], :, lanes[h]] for h in heads]
        gc_col = [gc_alls[seqs[h]][:, hid[h]:hid[h] + 1] for h in heads]
        beta = [gbs[seqs[h]][:, A_HEADS + hid[h]:A_HEADS + hid[h] + 1] for h in heads]
        g_last = [gc_alls[seqs[h]][c - 1:c, hid[h]:hid[h] + 1] for h in heads]

        kq = [_dot_nt(jnp.concatenate([kh[h], qh[h]], axis=0), kh[h]) for h in heads]
        decay = [jnp.where(causal, jnp.exp(jnp.where(
            causal, gc_col[h] - gc_all_ts[seqs[h]][hid[h]:hid[h] + 1, :], 0.0)), 0.0) for h in heads]
        attn = [(kq[h][c:2 * c] * decay[h]).astype(BF16) for h in heads]

        p = [jnp.where(strict, -(kq[h][0:c] * beta[h] * decay[h]), 0.0) for h in heads]
        t_inv = [eye + p[h] for h in heads]
        for _ in range(n_doublings):
            pb = [p[h].astype(BF16) for h in heads]
            p = [_dot(pb[h], pb[h]) for h in heads]
            t_inv = [t_inv[h] + _dot(t_inv[h].astype(BF16), p[h].astype(BF16)) for h in heads]

        e_gc = [jnp.exp(gc_col[h]) for h in heads]
        khf = [kh[h].astype(F32) for h in heads]
        rhs = [jnp.concatenate([v_ref[seqs[h], :, lanes[h]].astype(F32) * beta[h],
                                khf[h] * (beta[h] * e_gc[h])], axis=1).astype(BF16)
               for h in heads]
        sol = [_dot(t_inv[h].astype(BF16), rhs[h]) for h in heads]

        slot = [seqs[h] * A_HEADS + hid[h] for h in heads]
        state = [state_ref[slot[h]] for h in heads]
        lhs = [jnp.concatenate([sol[h][:, A_DK:2 * A_DK], qh[h].astype(F32) * e_gc[h]],
                               axis=0).astype(BF16) for h in heads]
        ws = [_dot(lhs[h], state[h].astype(BF16)) for h in heads]
        v_new = [(sol[h][:, 0:A_DK] - ws[h][0:c]).astype(BF16) for h in heads]
        k_dec_t = [(khf[h] * jnp.exp(g_last[h] - gc_col[h])).T.astype(BF16) for h in heads]
        o = [ws[h][c:2 * c] + _dot(attn[h], v_new[h]) for h in heads]
        for h in heads:
            state_ref[slot[h]] = state[h] * jnp.exp(g_last[h]) + _dot(k_dec_t[h], v_new[h])
        for h in heads:
            on = o[h] * lax.rsqrt(jnp.mean(o[h] * o[h], -1, keepdims=True) + EPS) * ng
            gate = gate_ref[seqs[h], :, lanes[h]].astype(F32)
            o_ref[seqs[h], :, lanes[h]] = (on * _silu(gate)).astype(BF16)

    chains = [(s, g) for s in range(n_seq) for g in range(A_HEADS)]
    for first in range(0, len(chains), DN_CHAINS_PER_GROUP):
        chain_group(chains[first:first + DN_CHAINS_PER_GROUP])


def _dn_scan(q, k, v, gate, gb, norm_g_row, batch, seq):
    c = DN_CHUNK
    nc = seq // c
    n_seq = DN_SEQS_PER_STEP
    assert batch % n_seq == 0
    rows = lambda b, n: (b, n, 0)
    const = lambda b, n: (0, 0)
    wide = lambda a: a.reshape(batch, seq, a.shape[-1])
    out = pl.pallas_call(
        _dn_scan_kernel,
        grid=(batch // n_seq, nc),
        in_specs=[pl.BlockSpec((n_seq, c, D_MODEL), rows)] * 4
        + [pl.BlockSpec((n_seq, c, LANES), rows), pl.BlockSpec(norm_g_row.shape, const)],
        out_specs=pl.BlockSpec((n_seq, c, D_MODEL), rows),
        out_shape=jax.ShapeDtypeStruct((batch, seq, D_MODEL), BF16),
        scratch_shapes=[pltpu.VMEM((n_seq * A_HEADS, A_DK, A_DK), F32)],
        compiler_params=_cparams(("arbitrary", "arbitrary")),
        name="dn_scan",
    )(wide(q), wide(k), wide(v), wide(gate), wide(gb), norm_g_row)
    return out.reshape(batch * seq, D_MODEL)


def _at_in_kernel(x_ref, w_ref, qt_ref, k_ref, vt_ref):
    tm = k_ref.shape[1]
    xb = _load_tokens(x_ref, tm).astype(BF16)
    q = _dot(xb, w_ref[:, 0:D_MODEL]) * (B_DH ** -0.5 * LOG2E)
    lane = lax.broadcasted_iota(I32, q.shape, 1)
    first_map = (lane % (2 * B_DH)) < B_DH
    qt_ref[0] = jnp.where(first_map, q, 0.0).T.astype(BF16)
    qt_ref[1] = jnp.where(first_map, 0.0, q).T.astype(BF16)
    k = _dot(xb, w_ref[:, D_MODEL:2 * D_MODEL]).astype(BF16)
    vt = _dot(xb, w_ref[:, 2 * D_MODEL:3 * D_MODEL]).T.astype(BF16)
    ones = jnp.ones((V_ROWS - 2 * B_DH, tm), BF16)
    for h in range(B_HEADS):
        k_ref[h] = k[:, h * LANES:(h + 1) * LANES]
        vt_ref[h] = jnp.concatenate([vt[h * LANES:(h + 1) * LANES], ones], axis=0)


def _at_in(x2d, w_in):
    t = x2d.shape[0] // SLAB
    tm = ROW_TILE
    return pl.pallas_call(
        _at_in_kernel,
        grid=(t // tm,),
        in_specs=[pl.BlockSpec((tm * SLAB, LANES), lambda i: (i, 0)), pl.BlockSpec(w_in.shape, lambda i: (0, 0))],
        out_specs=[pl.BlockSpec((2, D_MODEL, tm), lambda i: (0, 0, i)),
                   pl.BlockSpec((B_HEADS, tm, LANES), lambda i: (0, i, 0)),
                   pl.BlockSpec((B_HEADS, V_ROWS, tm), lambda i: (0, 0, i))],
        out_shape=[jax.ShapeDtypeStruct((2, D_MODEL, t), BF16),
                   jax.ShapeDtypeStruct((B_HEADS, t, LANES), BF16),
                   jax.ShapeDtypeStruct((B_HEADS, V_ROWS, t), BF16)],
        compiler_params=_cparams(("parallel",)),
        name="at_in",
    )(x2d, w_in)


def _attn_kernel(scal_ref, q0_ref, q1_ref, q0n_ref, q1n_ref, k_ref, vt_ref, bias_ref, ng_ref, o_ref,
                 m_ref, acc_ref, sa_ref, sb_ref, sc_ref):
    tq = o_ref.shape[0]
    tk = tq
    i = pl.program_id(2)
    lam = scal_ref[0]
    out_scale = scal_ref[1]

    m_ref[...] = jnp.full(m_ref.shape, NEG_BIG, F32)
    acc_ref[...] = jnp.zeros(acc_ref.shape, F32)
    qts = (q0_ref[0], q1_ref[0])

    maps = range(2)

    def scores(j, s_ref, q_maps=qts):
        kb = k_ref[0, pl.ds(pl.multiple_of(j * tk, tk), tk), :]
        for mp in maps:
            s_ref[mp] = _dot(kb, q_maps[mp])

    def absorb(j, s_ref, bias):
        vtb = vt_ref[0, :, pl.ds(pl.multiple_of(j * tk, tk), tk)]
        s = [s_ref[mp] for mp in maps]
        if bias is not None:
            s = [x + bias for x in s]
        m_old = [m_ref[mp] for mp in maps]
        m_new = [jnp.maximum(m_old[mp], jnp.max(s[mp], 0, keepdims=True)) for mp in maps]
        alpha = [jnp.exp2(m_old[mp] - m_new[mp]) for mp in maps]
        p = [jnp.exp2(s[mp] - m_new[mp]) for mp in maps]
        pv = [_dot(vtb, p[mp].astype(BF16)) for mp in maps]
        for mp in maps:
            acc_ref[mp] = alpha[mp] * acc_ref[mp] + pv[mp]
            m_ref[mp] = m_new[mp]

    @pl.when(i == 0)
    def _():
        scores(0, sa_ref)

    @pl.when((i % 2 == 0) & (i >= 2))
    def _():
        scores(1, sb_ref)
        absorb(0, sc_ref, None)

    first = jnp.where(i % 2 == 1, 0, 1)

    def far_pair(jj, carry):
        j = first + 2 * jj
        scores(j + 1, sa_ref)
        absorb(j, sb_ref, None)
        scores(j + 2, sb_ref)
        absorb(j + 1, sa_ref, None)
        return carry

    lax.fori_loop(0, jnp.maximum(i - 1, 0) // 2, far_pair, 0)

    @pl.when(i >= 1)
    def _():
        scores(i, sa_ref)
        absorb(i - 1, sb_ref, bias_ref[0, 1])

    last = i == pl.num_programs(2) - 1
    next_qts = (q0n_ref[0], q1n_ref[0])

    @pl.when(last)
    def _():
        absorb(i, sa_ref, bias_ref[0, 0])

    @pl.when(jnp.logical_not(last) & (i % 2 == 0))
    def _():
        scores(0, sb_ref, next_qts)
        absorb(i, sa_ref, bias_ref[0, 0])

    @pl.when(jnp.logical_not(last) & (i % 2 == 1))
    def _():
        scores(0, sc_ref, next_qts)
        absorb(i, sa_ref, bias_ref[0, 0])

    dv = 2 * B_DH
    o_t = (acc_ref[0, 0:dv] * (1.0 / acc_ref[0, dv:dv + 1])
           - acc_ref[1, 0:dv] * (lam / acc_ref[1, dv:dv + 1]))
    o = o_t.T
    o = o * lax.rsqrt(jnp.mean(o * o, -1, keepdims=True) + EPS) * ng_ref[...] * out_scale
    o_ref[...] = o.astype(BF16)


def _attn(scalars, qt, k, vt, bias_tab, norm_g_row, batch, seq):
    tq = ATT_BLOCK
    nq = seq // tq
    t = batch * seq
    return pl.pallas_call(
        _attn_kernel,
        grid=(batch, B_HEADS, nq),
        in_specs=[
            pl.BlockSpec(memory_space=pltpu.SMEM),
            pl.BlockSpec((1, LANES, tq), lambda b, h, i: (0, h, b * nq + i)),
            pl.BlockSpec((1, LANES, tq), lambda b, h, i: (1, h, b * nq + i)),
            pl.BlockSpec((1, LANES, tq), lambda b, h, i: (0, h, b * nq + jnp.minimum(i + 1, nq - 1))),
            pl.BlockSpec((1, LANES, tq), lambda b, h, i: (1, h, b * nq + jnp.minimum(i + 1, nq - 1))),
            pl.BlockSpec((1, seq, LANES), lambda b, h, i: (h, b, 0)),
            pl.BlockSpec((1, V_ROWS, seq), lambda b, h, i: (h, 0, b)),
            pl.BlockSpec((1, 2, tq, tq), lambda b, h, i: (h, 0, 0, 0)),
            pl.BlockSpec(norm_g_row.shape, lambda b, h, i: (0, 0)),
        ],
        out_specs=pl.BlockSpec((tq, LANES), lambda b, h, i: (b * nq + i, h)),
        out_shape=jax.ShapeDtypeStruct((t, D_MODEL), BF16),
        scratch_shapes=[pltpu.VMEM((2, 1, tq), F32), pltpu.VMEM((2, V_ROWS, tq), F32),
                        pltpu.VMEM((2, tq, tq), F32), pltpu.VMEM((2, tq, tq), F32),
                        pltpu.VMEM((2, tq, tq), F32)],
        compiler_params=_cparams(("parallel", "parallel", "arbitrary")),
        name="attn",
    )(scalars, qt, qt, qt, qt, k, vt, bias_tab, norm_g_row)


def _rel_bucket(rel):
    n = jnp.maximum(rel, 0)
    max_exact = REL_BUCKETS // 2
    large = max_exact + (jnp.log(jnp.maximum(n, 1).astype(F32) / max_exact)
                         / math.log(REL_MAX_DIST / max_exact) * (REL_BUCKETS - max_exact)).astype(I32)
    large = jnp.minimum(large, REL_BUCKETS - 1)
    return jnp.where(n < max_exact, n, large)


def _attn_bias_table(rel_bias):
    tq = ATT_BLOCK
    assert tq + 1 >= REL_MAX_DIST, "blocks two back must fall in the last bucket"
    table = (rel_bias.astype(F32) - rel_bias.astype(F32)[REL_BUCKETS - 1]) * LOG2E
    period = 2 * tq
    idx = jnp.arange(period)
    diag_vec = jnp.where((idx < tq)[:, None], table[_rel_bucket(jnp.minimum(idx, tq - 1))], NEG_BIG)
    near_vec = table[_rel_bucket(jnp.where(idx < tq, idx + tq, idx - tq))]
    vecs = jnp.transpose(jnp.stack([diag_vec, near_vec], 0), (2, 0, 1))
    return pl.pallas_call(
        _toeplitz_kernel,
        grid=(B_HEADS, 2),
        in_specs=[pl.BlockSpec((1, 1, 1, period), lambda h, k: (h, k, 0, 0))],
        out_specs=pl.BlockSpec((1, 1, tq, tq), lambda h, k: (h, k, 0, 0)),
        out_shape=jax.ShapeDtypeStruct((B_HEADS, 2, tq, tq), F32),
        name="bias_table",
    )(vecs.reshape(B_HEADS, 2, 1, period))


def _toeplitz_kernel(v_ref, o_ref):
    tq = o_ref.shape[2]
    x = jnp.broadcast_to(v_ref[0, 0], (tq, v_ref.shape[3]))
    o_ref[0, 0] = pltpu.roll(x, 0, 1, stride=1, stride_axis=0)[:, :tq]


def _gelu(x):
    return 0.5 * x * (1.0 + lax.erf(x * (2.0 ** -0.5)))


def _gm_kernel(x_ref, w_ref, lng_ref, lnb_ref, ws_ref, bs_ref, o_ref):
    tm = o_ref.shape[0]
    xb = _load_tokens(x_ref, tm).astype(BF16)
    v = _gelu(_dot(xb, w_ref[:, C_HALF:2 * C_HALF]))
    vn = _layer_norm(v, lng_ref[...], lnb_ref[...]).astype(BF16)
    bs = bs_ref[...]
    for g in range(C_GROUPS):
        lo = g * C_GW
        u = _gelu(_dot(xb, w_ref[:, lo:lo + C_GW]))
        wg = ws_ref[g]
        for ch in range(tm // C_CHUNK):
            r0 = ch * C_CHUNK
            mixed = _dot(wg, vn[r0:r0 + C_CHUNK, lo:lo + C_GW]) + bs[:, g:g + 1]
            o_ref[r0:r0 + C_CHUNK, lo:lo + C_GW] = (u[r0:r0 + C_CHUNK] * mixed).astype(BF16)


def _gm(x2d, w_in, ln_g_row, ln_b_row, ws_causal, bs_t):
    t = x2d.shape[0] // SLAB
    tm = ROW_TILE
    rows = lambda i: (i, 0)
    const2 = lambda i: (0, 0)
    return pl.pallas_call(
        _gm_kernel,
        grid=(t // tm,),
        in_specs=[
            pl.BlockSpec((tm * SLAB, LANES), rows),
            pl.BlockSpec(w_in.shape, const2),
            pl.BlockSpec(ln_g_row.shape, const2),
            pl.BlockSpec(ln_b_row.shape, const2),
            pl.BlockSpec(ws_causal.shape, lambda i: (0, 0, 0)),
            pl.BlockSpec(bs_t.shape, const2),
        ],
        out_specs=pl.BlockSpec((tm, C_HALF), rows),
        out_shape=jax.ShapeDtypeStruct((t, C_HALF), BF16),
        compiler_params=_cparams(("parallel",)),
        name="gmlp",
    )(x2d, w_in, ln_g_row, ln_b_row, ws_causal, bs_t)


def _out_kernel(y_ref, w_ref, h_ref, lng_ref, lnb_ref, wr_ref, rb_ref, hs_ref, meta_ref, cnt_ref,
                carry_ref):
    tm = y_ref.shape[0]

    @pl.when(pl.program_id(0) == 0)
    def _():
        carry_ref[...] = jnp.zeros(carry_ref.shape, F32)

    n_part = OUT_TILE_PARTS
    rows_per = tm // n_part
    parts = range(n_part)
    lng, lnb = lng_ref[...], lnb_ref[...]
    proj = [_dot(y_ref[k * rows_per:(k + 1) * rows_per, :], w_ref[...]) for k in parts]
    h1 = [_layer_norm(DN_ALPHA * _load_tokens(h_ref, rows_per, k * rows_per) + proj[k], lng, lnb)
          for k in parts]
    for k in parts:
        _store_tokens(hs_ref, h1[k], rows_per, k * rows_per)

    h_hi = [h1[k].astype(BF16) for k in parts]
    h_lo = [(h1[k] - h_hi[k].astype(F32)).astype(BF16) for k in parts]
    both = [_dot_nt(wr_ref[...], h_hi[k]) for k in parts]
    low = [_dot_nt(wr_ref[0:N_EXPERTS, :], h_lo[k]) for k in parts]
    logits = jnp.concatenate(
        [both[k][0:N_EXPERTS] + both[k][LANES:LANES + N_EXPERTS] + low[k] for k in parts], axis=1)
    sel = jax.nn.sigmoid(logits) + jnp.concatenate([rb_ref[...]] * (tm // LANES), axis=1)
    rows = [sel[e:e + 1, :] for e in range(N_EXPERTS)]

    def top2_sum(v):
        sums = [v[a] + v[b] for a, b in _PAIRS]
        return functools.reduce(jnp.maximum, sums)

    grp = [top2_sum(rows[g * EXPERTS_PER_GROUP:(g + 1) * EXPERTS_PER_GROUP]) for g in range(N_GROUPS)]
    best, best_val = jnp.zeros_like(grp[0]), grp[0]
    for g in range(1, N_GROUPS):
        better = grp[g] > best_val
        best = jnp.where(better, float(g), best)
        best_val = jnp.where(better, grp[g], best_val)
    member = []
    for k in range(EXPERTS_PER_GROUP):
        v = rows[k]
        for g in range(1, N_GROUPS):
            v = jnp.where(best == float(g), rows[g * EXPERTS_PER_GROUP + k], v)
        member.append(v)
    chosen = []
    for k in range(EXPERTS_PER_GROUP):
        beaten = jnp.zeros_like(best)
        for j in range(EXPERTS_PER_GROUP):
            if j != k:
                ahead = (member[j] >= member[k]) if j < k else (member[j] > member[k])
                beaten = beaten + ahead.astype(F32)
        chosen.append(beaten < 2.0)
    cls = best * float(len(_PAIRS))
    for idx, (a, b) in enumerate(_PAIRS):
        cls = cls + jnp.where(chosen[a] & chosen[b], float(idx), 0.0)

    n_cls_rows = carry_ref.shape[0]
    cls_row = lax.broadcasted_iota(I32, (n_cls_rows, tm), 0).astype(F32)
    onehot = (cls_row == cls).astype(F32)
    t_i = lax.broadcasted_iota(I32, (tm, tm), 0)
    t_j = lax.broadcasted_iota(I32, (tm, tm), 1)
    carry = carry_ref[...]
    before = _dot(onehot.astype(BF16), (t_i < t_j).astype(BF16)) + jnp.concatenate(
        [carry] * (tm // LANES), axis=1)
    rank = jnp.sum(onehot * before, 0, keepdims=True)
    carry_ref[...] = carry + jnp.sum(onehot, 1, keepdims=True)
    cnt_ref[...] = carry_ref[...]

    row_id = lax.broadcasted_iota(I32, meta_ref.shape, 0)
    meta_ref[...] = jnp.where(row_id == 0, cls, jnp.where(row_id == 1, rank, 0.0))


def _out_proj(y, w_out, hs, ln_g_row, ln_b_row, wr_split, rbias_row):
    t, kd = y.shape
    tm = OUT_TILE
    rows = lambda i: (i, 0)
    const = lambda i: (0, 0)
    return pl.pallas_call(
        _out_kernel,
        grid=(t // tm,),
        in_specs=[
            pl.BlockSpec((tm, kd), rows),
            pl.BlockSpec(w_out.shape, const),
            _token_spec(hs, tm, rows),
            pl.BlockSpec(ln_g_row.shape, const),
            pl.BlockSpec(ln_b_row.shape, const),
            pl.BlockSpec(wr_split.shape, const),
            pl.BlockSpec(rbias_row.shape, const),
        ],
        out_specs=[pl.BlockSpec((tm * SLAB, LANES), rows), pl.BlockSpec((SUBLANES, tm), lambda i: (0, i)),
                   pl.BlockSpec((CLASS_ROWS, LANES), const)],
        out_shape=[jax.ShapeDtypeStruct((t * SLAB, LANES), F32),
                   jax.ShapeDtypeStruct((SUBLANES, t), F32),
                   jax.ShapeDtypeStruct((CLASS_ROWS, LANES), F32)],
        scratch_shapes=[pltpu.VMEM((CLASS_ROWS, LANES), F32)],
        compiler_params=_cparams(("arbitrary",)),
        name="out_proj",
    )(y, w_out, hs, ln_g_row, ln_b_row, wr_split, rbias_row)


def _permute_kernel(pos_ref, src_ref, *refs, scatter, row_major_out):
    if row_major_out:
        o_ref, sem, dst_ref = refs[-3:]
    else:
        dst_ref, sem = refs[-2:]
    n_tok = pos_ref.shape[2]

    def issue(g, carry):
        for u in range(PERMUTE_UNROLL):
            r = g * PERMUTE_UNROLL + u
            here = pl.ds(pl.multiple_of(r * SLAB, SLAB), SLAB)
            there = pl.ds(pl.multiple_of(pos_ref[0, 0, r] * SLAB, SLAB), SLAB)
            src_rows, dst_rows = (here, there) if scatter else (there, here)
            pltpu.make_async_copy(src_ref.at[src_rows, :], dst_ref.at[dst_rows, :], sem).start(
                priority=u % 2)
        return carry

    lax.fori_loop(0, n_tok // PERMUTE_UNROLL, issue, 0)
    whole = pl.ds(0, n_tok * SLAB)
    pltpu.make_async_copy(src_ref.at[whole, :], dst_ref.at[whole, :], sem).wait()
    if row_major_out:
        o_ref[...] = _load_tokens(dst_ref, n_tok)


def _permute(pos, src, n_dst_rows, scatter, row_major_out=False, zeroed_dst=None):
    n_tok = pos.shape[2]
    any_spec = pl.BlockSpec(memory_space=pl.ANY)
    tile_spec = pl.BlockSpec((n_tok * SLAB, LANES), lambda i: (i, 0))
    out_shape = jax.ShapeDtypeStruct((n_dst_rows, LANES), F32)
    scratch = [pltpu.SemaphoreType.DMA(())]
    if row_major_out:
        assert not scatter
        tile_spec = pl.BlockSpec((n_tok, D_MODEL), lambda i: (i, 0))
        out_shape = jax.ShapeDtypeStruct((n_dst_rows // SLAB, D_MODEL), F32)
        scratch.append(pltpu.VMEM((n_tok * SLAB, LANES), F32))
    extra = []
    if scatter:
        extra = [jnp.zeros((n_dst_rows, LANES), F32) if zeroed_dst is None else zeroed_dst]
    return pl.pallas_call(
        functools.partial(_permute_kernel, scatter=scatter, row_major_out=row_major_out),
        grid=(pos.shape[0],),
        in_specs=[pl.BlockSpec((1, 1, n_tok), lambda i: (i, 0, 0), memory_space=pltpu.SMEM),
                  tile_spec if scatter else any_spec] + [any_spec] * len(extra),
        out_specs=any_spec if scatter else tile_spec,
        out_shape=out_shape,
        scratch_shapes=scratch,
        input_output_aliases={2: 0} if scatter else {},
        compiler_params=_cparams(("arbitrary",)),
        name="dispatch" if scatter else "unpermute",
    )(pos, src, *extra)


def _moe_kernel(lo_ref, hi_ref, nv_ref, xs_ref, w1l_ref, w3l_ref, w2l_ref, w1h_ref, w3h_ref, w2h_ref,
                wrl_ref, wrh_ref, lng_ref, lnb_ref, ys_ref, zeros_ref, wl_bf_ref, wh_bf_ref, w2_bf_ref):
    zeros_ref[...] = jnp.zeros(zeros_ref.shape, F32)
    tm = xs_ref.shape[0] // SLAB

    i = pl.program_id(0)
    prev = jnp.maximum(i - 1, 0)
    valid = i < nv_ref[0]

    @pl.when(valid & ((i == 0) | (lo_ref[i] != lo_ref[prev])))
    def _():
        wl_bf_ref[0] = w1l_ref[0, 0].astype(BF16)
        wl_bf_ref[1] = w3l_ref[0, 0].astype(BF16)
        w2_bf_ref[0] = w2l_ref[0, 0].astype(BF16)

    @pl.when(valid & ((i == 0) | (hi_ref[i] != hi_ref[prev])))
    def _():
        wh_bf_ref[0] = w1h_ref[0, 0].astype(BF16)
        wh_bf_ref[1] = w3h_ref[0, 0].astype(BF16)
        w2_bf_ref[1] = w2h_ref[0, 0].astype(BF16)

    @pl.when(valid)
    def _():
        n_grp = MOE_TILE_PARTS
        rows = tm // n_grp
        up = (wl_bf_ref, wh_bf_ref)
        chains = [(g, e) for g in range(n_grp) for e in range(2)]
        x = [_load_tokens(xs_ref, rows, g * rows) for g in range(n_grp)]
        xb = [x[g].astype(BF16) for g in range(n_grp)]
        gate_in = {(g, e): _dot(xb[g], up[e][0]) for g, e in chains}
        lin_in = {(g, e): _dot(xb[g], up[e][1]) for g, e in chains}
        hid = {c: (_silu(gate_in[c]) * lin_in[c]).astype(BF16) for c in chains}
        y = {(g, e): _dot(hid[(g, e)], w2_bf_ref[e]) for g, e in chains}
        for g in range(n_grp):
            s_lo = jax.nn.sigmoid(jnp.sum(x[g] * wrl_ref[0], -1, keepdims=True))
            s_hi = jax.nn.sigmoid(jnp.sum(x[g] * wrh_ref[0], -1, keepdims=True))
            denom = s_lo + s_hi
            m = (s_lo / denom) * y[(g, 0)] + (s_hi / denom) * y[(g, 1)]
            out = _layer_norm(DN_ALPHA * x[g] + m, lng_ref[...], lnb_ref[...])
            _store_tokens(ys_ref, out, rows, g * rows)

    @pl.when(pl.program_id(0) >= nv_ref[0])
    def _():
        ys_ref[...] = jnp.zeros(ys_ref.shape, F32)


def _moe(tile_lo, tile_hi, n_valid, xs, w1, w3, w2, layer, wr_rows, ln_g_row, ln_b_row):
    tm = MOE_TILE
    n_tiles = xs.shape[0] // (tm * SLAB)
    rows = lambda i, lo, hi, nv: (jnp.minimum(i, nv[0] - 1), 0)
    w_lo = lambda i, lo, hi, nv: (layer, lo[i], 0, 0)
    w_hi = lambda i, lo, hi, nv: (layer, hi[i], 0, 0)
    r_lo = lambda i, lo, hi, nv: (lo[i], 0, 0)
    r_hi = lambda i, lo, hi, nv: (hi[i], 0, 0)
    const = lambda i, lo, hi, nv: (0, 0)
    up = (1, 1, D_MODEL, D_EXPERT)
    down = (1, 1, D_EXPERT, D_MODEL)
    router_row = (1, 1, D_MODEL)
    grid_spec = pltpu.PrefetchScalarGridSpec(
        num_scalar_prefetch=3,
        grid=(n_tiles,),
        in_specs=[
            pl.BlockSpec((tm * SLAB, LANES), rows),
            pl.BlockSpec(up, w_lo), pl.BlockSpec(up, w_lo), pl.BlockSpec(down, w_lo),
            pl.BlockSpec(up, w_hi), pl.BlockSpec(up, w_hi), pl.BlockSpec(down, w_hi),
            pl.BlockSpec(router_row, r_lo), pl.BlockSpec(router_row, r_hi),
            pl.BlockSpec(ln_g_row.shape, const), pl.BlockSpec(ln_b_row.shape, const),
        ],
        out_specs=[pl.BlockSpec((tm * SLAB, LANES), lambda i, lo, hi, nv: (i, 0))] * 2,
        scratch_shapes=[pltpu.VMEM((2, D_MODEL, D_EXPERT), BF16), pltpu.VMEM((2, D_MODEL, D_EXPERT), BF16),
                        pltpu.VMEM((2, D_EXPERT, D_MODEL), BF16)],
    )
    return pl.pallas_call(
        _moe_kernel,
        grid_spec=grid_spec,
        out_shape=[jax.ShapeDtypeStruct(xs.shape, F32)] * 2,
        compiler_params=_cparams(("arbitrary",)),
        name="moe",
    )(tile_lo, tile_hi, n_valid, xs, w1, w3, w2, w1, w3, w2, wr_rows, wr_rows, ln_g_row, ln_b_row)


def _routing_plan(meta, counts, n_slots):
    cls = meta[0].astype(I32)
    rank = meta[1].astype(I32)
    cnt = counts[:N_PAIR_CLASSES, 0].astype(I32)
    padded = ((cnt + MOE_TILE - 1) // MOE_TILE) * MOE_TILE
    end = jnp.cumsum(padded)
    start = end - padded
    classes = jnp.arange(N_PAIR_CLASSES, dtype=I32)
    pos = jnp.sum(jnp.where(cls[:, None] == classes[None, :], start[None, :], 0), -1) + rank
    n_tiles = n_slots // MOE_TILE
    n_valid = (end[-1] // MOE_TILE).astype(I32).reshape(1)
    tile_start = jnp.minimum(jnp.arange(n_tiles, dtype=I32), n_valid[0] - 1) * MOE_TILE
    tile_cls = jnp.sum((tile_start[:, None] >= end[None, :]).astype(I32), -1)
    onehot = (tile_cls[:, None] == classes[None, :]).astype(I32)
    tile_lo = jnp.sum(onehot * jnp.asarray(_PAIR_LO)[None, :], -1)
    tile_hi = jnp.sum(onehot * jnp.asarray(_PAIR_HI)[None, :], -1)
    return pos, tile_lo, tile_hi, n_valid


def _moe_layer(hs, meta, counts, w1, w3, w2, layer, wr_rows, ln_g_row, ln_b_row, row_major_out,
               zeroed_slots):
    t = meta.shape[1]
    n_slots = t + N_PAIR_CLASSES * MOE_TILE
    pos, tile_lo, tile_hi, n_valid = _routing_plan(meta, counts, n_slots)
    pos = pos.reshape(t // PERMUTE_TOKENS, 1, PERMUTE_TOKENS)
    xs = _permute(pos, hs, n_slots * SLAB, scatter=True, zeroed_dst=zeroed_slots)
    ys, zeroed_next = _moe(tile_lo, tile_hi, n_valid, xs, w1, w3, w2, layer, wr_rows, ln_g_row, ln_b_row)
    return _permute(pos, ys, t * SLAB, scatter=False, row_major_out=row_major_out), zeroed_next


def _row(v, width=None):
    v = v.astype(F32).reshape(1, -1)
    if width is not None and v.shape[1] < width:
        v = jnp.pad(v, ((0, 0), (0, width - v.shape[1])))
    return v


def _deltanet(h2d, batch, seq, w_in, conv_w, a_log, dt_bias, norm_g):
    qkvg = 4 * D_MODEL
    w_big = w_in.astype(BF16)
    w_ab = jnp.pad(w_in[:, qkvg:], ((0, 0), (0, LANES - 2 * A_HEADS))).astype(BF16)
    q, k, v, gate, gb = _dn_in(h2d, batch, seq, w_big, w_ab, conv_w.astype(F32),
                               _row(a_log, LANES), _row(dt_bias, LANES))
    return _dn_scan(q, k, v, gate, gb, _row(norm_g), batch, seq)


def _diff_attention(h2d, batch, seq, w_in, lam, norm_g, rel_bias, lambda_init):
    lamf = lam.astype(F32)
    lam_full = jnp.exp(jnp.sum(lamf[0] * lamf[1])) - jnp.exp(jnp.sum(lamf[2] * lamf[3])) + lambda_init
    scalars = jnp.stack([lam_full, jnp.asarray(1.0 - lambda_init, F32)]).astype(F32)
    qt, k, vt = _at_in(h2d, w_in.astype(BF16))
    return _attn(scalars, qt, k, vt, _attn_bias_table(rel_bias), _row(norm_g), batch, seq)


def _gmlp(h2d, w_in, ln_g, ln_b, w_s, b_s):
    ws_causal = jnp.tril(w_s.astype(F32)).astype(BF16)
    bs_t = jnp.pad(jnp.transpose(b_s.astype(F32)), ((0, 0), (0, LANES - C_GROUPS)))
    return _gm(h2d, w_in.astype(BF16), _row(ln_g), _row(ln_b), ws_causal, bs_t)


def kernel(x, ln_mix_g, ln_mix_b, ln_ffn_g, ln_ffn_b, a_w_in, a_conv, a_a_log, a_dt_bias, a_norm_g, a_w_out, b_w_in, b_lambda, b_norm_g, b_w_out, rel_bias, c_w_in, c_ln_g, c_ln_b, c_w_s, c_b_s, c_w_out, w_router, router_bias, e_w1, e_w3, e_w2):
    batch, seq, d = x.shape
    assert d == D_MODEL and seq % max(ROW_TILE, ATT_BLOCK, DN_CHUNK) == 0
    t = batch * seq
    h = x.astype(F32).reshape(t, d)

    wr_rows = jnp.transpose(w_router.astype(F32)).reshape(N_EXPERTS, 1, D_MODEL)
    wr = jnp.pad(jnp.transpose(w_router.astype(F32)), ((0, LANES - N_EXPERTS), (0, 0)))
    wr_hi = wr.astype(BF16)
    wr_lo = (wr - wr_hi.astype(F32)).astype(BF16)
    wr_split = jnp.concatenate([wr_hi, wr_lo], axis=0)
    rbias_row = jnp.broadcast_to(router_bias.astype(F32)[:, None], (N_EXPERTS, LANES))

    w1_all, w3_all, w2_all = e_w1, e_w3, e_w2

    zeroed_slots = None
    for i in range(DEPTH):
        kind, j = i % N_MIXERS, i // N_MIXERS
        if kind == 0:
            y = _deltanet(h, batch, seq, a_w_in[j], a_conv[j], a_a_log[j], a_dt_bias[j], a_norm_g[j])
            w_out = a_w_out[j]
        elif kind == 1:
            lambda_init = 0.8 - 0.6 * math.exp(-0.3 * i)
            y = _diff_attention(h, batch, seq, b_w_in[j], b_lambda[j], b_norm_g[j], rel_bias, lambda_init)
            w_out = b_w_out[j]
        else:
            y = _gmlp(h, c_w_in[j], c_ln_g[j], c_ln_b[j], c_w_s[j], c_b_s[j])
            w_out = c_w_out[j]
        hs, meta, counts = _out_proj(y, w_out.astype(BF16), h, _row(ln_mix_g[i]), _row(ln_mix_b[i]),
                                     wr_split, rbias_row)
        h, zeroed_slots = _moe_layer(hs, meta, counts, w1_all, w3_all, w2_all, i, wr_rows,
                                     _row(ln_ffn_g[i]), _row(ln_ffn_b[i]),
                                     row_major_out=(i == DEPTH - 1), zeroed_slots=zeroed_slots)
    return h.reshape(batch, seq, d).astype(x.dtype)
```

```python
import functools
import math

import jax
import jax.numpy as jnp
import numpy as np
from jax import lax
from jax.experimental import pallas as pl
from jax.experimental.pallas import tpu as pltpu

F32 = jnp.float32
BF16 = jnp.bfloat16
I32 = jnp.int32

D_MODEL = 1024
DEPTH = 4
N_MIXERS = 3
A_HEADS = 8
A_DK = 128
A_CONV = 4
B_DH = 64
B_HEADS = D_MODEL // (2 * B_DH)
REL_BUCKETS = 32
REL_MAX_DIST = 128
C_CHUNK = 128
C_HALF = 2 * D_MODEL
C_GROUPS = 8
C_GW = C_HALF // C_GROUPS
N_EXPERTS = 16
N_GROUPS = 4
EXPERTS_PER_GROUP = 4
D_EXPERT = 512
DN_ALPHA = (2 * DEPTH) ** 0.25
EPS = 1e-5

LANES = 128
SUBLANES = 8
BF16_ROWS = 16
VMEM_LIMIT_BYTES = 56 * 1024 * 1024
SLAB = D_MODEL // LANES

ROW_TILE = 512
OUT_TILE = 1024
OUT_TILE_PARTS = 2
DN_CHUNK = 128
DN_SEQS_PER_STEP = 2
DN_CHAINS_PER_GROUP = 16
ATT_BLOCK = 512
V_ROWS = 2 * B_DH + BF16_ROWS
MOE_TILE = 512
MOE_TILE_PARTS = 2
PERMUTE_TOKENS = 4096
PERMUTE_UNROLL = 16
_PAIRS = ((0, 1), (0, 2), (0, 3), (1, 2), (1, 3), (2, 3))
N_PAIR_CLASSES = N_GROUPS * len(_PAIRS)
CLASS_ROWS = -(-N_PAIR_CLASSES // SUBLANES) * SUBLANES
NEG_BIG = -1e30
LOG2E = math.log2(math.e)

_PAIR_LO = np.array([g * EXPERTS_PER_GROUP + a for g in range(N_GROUPS) for a, _ in _PAIRS], np.int32)
_PAIR_HI = np.array([g * EXPERTS_PER_GROUP + b for g in range(N_GROUPS) for _, b in _PAIRS], np.int32)


def _cparams(sem):
    return pltpu.CompilerParams(dimension_semantics=sem, vmem_limit_bytes=VMEM_LIMIT_BYTES)


def _layer_norm(x, g, b):
    mu = jnp.mean(x, -1, keepdims=True)
    xc = x - mu
    var = jnp.mean(xc * xc, -1, keepdims=True)
    return xc * lax.rsqrt(var + EPS) * g + b


def _silu(x):
    return x * jax.nn.sigmoid(x)


def _dot(a, b):
    return jnp.dot(a, b, preferred_element_type=F32)


def _dot_nt(a, b):
    return lax.dot_general(a, b, (((1,), (1,)), ((), ())), preferred_element_type=F32)


def _load_tokens(ref, n_tok, first=0):
    if ref.shape[-1] == D_MODEL:
        return ref[first:first + n_tok, :]
    return jnp.concatenate([ref[pl.ds(first * SLAB + s, n_tok, stride=SLAB), :] for s in range(SLAB)],
                           axis=1)


def _token_spec(arr, n_tok, index_map):
    if arr.shape[-1] == D_MODEL:
        return pl.BlockSpec((n_tok, D_MODEL), index_map)
    return pl.BlockSpec((n_tok * SLAB, LANES), index_map)


def _store_tokens(ref, val, n_tok, first=0):
    for s in range(SLAB):
        ref[pl.ds(first * SLAB + s, n_tok, stride=SLAB), :] = val[:, s * LANES:(s + 1) * LANES]


def _split3(x):
    hi = x.astype(BF16)
    r = x - hi.astype(F32)
    mid = r.astype(BF16)
    lo = (r - mid.astype(F32)).astype(BF16)
    return hi, mid, lo


def _dn_in_kernel(x_ref, w_ref, wab_ref, conv_ref, alog_ref, dtb_ref,
                  q_ref, k_ref, v_ref, gate_ref, gb_ref, ext_ref):
    ts = q_ref.shape[0]
    qkv_w = 3 * D_MODEL
    n_slabs = qkv_w // LANES
    s_idx = pl.program_id(1)

    @pl.when(s_idx == 0)
    def _():
        ext_ref[:, 0:SUBLANES, :] = jnp.zeros((n_slabs, SUBLANES, LANES), F32)

    @pl.when(s_idx > 0)
    def _():
        ext_ref[:, 0:SUBLANES, :] = ext_ref[:, ts:ts + SUBLANES, :]

    xb = _load_tokens(x_ref, ts).astype(BF16)
    gate_ref[...] = _dot(xb, w_ref[:, qkv_w:qkv_w + D_MODEL]).astype(BF16)
    cw = conv_ref[...]
    for part, out_ref in enumerate((q_ref, k_ref, v_ref)):
        pre = _dot(xb, w_ref[:, part * D_MODEL:(part + 1) * D_MODEL])
        for h in range(A_HEADS):
            slab = part * A_HEADS + h
            lo = h * A_DK
            ext_ref[slab, SUBLANES:SUBLANES + ts, :] = pre[:, lo:lo + A_DK]
            col = slice(slab * LANES, (slab + 1) * LANES)
            y = ext_ref[slab, pl.ds(SUBLANES - 3, ts), :] * cw[0:1, col]
            for j in range(1, A_CONV):
                y = y + ext_ref[slab, pl.ds(SUBLANES - 3 + j, ts), :] * cw[j:j + 1, col]
            y = _silu(y)
            if part == 0:
                y = y * (lax.rsqrt(jnp.sum(y * y, -1, keepdims=True) + 1e-6) * (A_DK ** -0.5))
            elif part == 1:
                y = y * lax.rsqrt(jnp.sum(y * y, -1, keepdims=True) + 1e-6)
            out_ref[:, lo:lo + A_DK] = y.astype(BF16)

    ab = _dot(xb, wab_ref[...])
    z = ab + dtb_ref[...]
    softplus = jnp.maximum(z, 0.0) + jnp.log1p(jnp.exp(-jnp.abs(z)))
    g = -jnp.exp(alog_ref[...]) * softplus
    lane = lax.broadcasted_iota(I32, ab.shape, 1)
    gb_ref[...] = jnp.where(lane < A_HEADS, g, jax.nn.sigmoid(ab))


def _dn_in(x2d, batch, seq, w_big, w_ab, conv_w, alog_row, dtb_row):
    ts = ROW_TILE
    ns = seq // ts
    rows = lambda b, s: (b * ns + s, 0)
    const = lambda b, s: (0, 0)
    t = batch * seq
    out_bf = jax.ShapeDtypeStruct((t, D_MODEL), BF16)
    return pl.pallas_call(
        _dn_in_kernel,
        grid=(batch, ns),
        in_specs=[
            _token_spec(x2d, ts, rows),
            pl.BlockSpec(w_big.shape, const),
            pl.BlockSpec(w_ab.shape, const),
            pl.BlockSpec(conv_w.shape, const),
            pl.BlockSpec(alog_row.shape, const),
            pl.BlockSpec(dtb_row.shape, const),
        ],
        out_specs=[pl.BlockSpec((ts, D_MODEL), rows)] * 4 + [pl.BlockSpec((ts, LANES), rows)],
        out_shape=[out_bf, out_bf, out_bf, out_bf, jax.ShapeDtypeStruct((t, LANES), F32)],
        scratch_shapes=[pltpu.VMEM((3 * D_MODEL // LANES, ts + 2 * SUBLANES, LANES), F32)],
        compiler_params=_cparams(("arbitrary", "arbitrary")),
        name="dn_in",
    )(x2d, w_big, w_ab, conv_w, alog_row, dtb_row)


def _dn_scan_kernel(q_ref, k_ref, v_ref, gate_ref, gb_ref, ng_ref, o_ref, state_ref):
    n_seq, c = q_ref.shape[0], q_ref.shape[1]
    n_doublings = int(math.log2(c)) - 1

    @pl.when(pl.program_id(1) == 0)
    def _():
        state_ref[...] = jnp.zeros(state_ref.shape, F32)

    row = lax.broadcasted_iota(I32, (c, c), 0)
    col = lax.broadcasted_iota(I32, (c, c), 1)
    causal = row >= col
    strict = row > col
    eye = (row == col).astype(F32)

    tril = causal.astype(BF16)
    gbs = [gb_ref[s] for s in range(n_seq)]
    gc_alls, gc_all_ts = [], []
    for gb_s in gbs:
        parts = _split3(gb_s)
        gc = _dot(tril, parts[0]) + _dot(tril, parts[1]) + _dot(tril, parts[2])
        gc_alls.append(gc)
        gc_all_ts.append(gc.T)
    ng = ng_ref[...]

    def chain_group(chains):
        heads = range(len(chains))
        seqs = [s for s, _ in chains]
        hid = [g for _, g in chains]
        lanes = [slice(g * A_DK, (g + 1) * A_DK) for g in hid]
        qh = [q_ref[seqs[h], :, lanes[h]] for h in heads]
        kh = [k_ref[seqs[h], :, lanes[h]] for h in heads]
        gc_col = [gc_alls[seqs[h]][:, hid[h]:hid[h] + 1] for h in heads]
        beta = [gbs[seqs[h]][:, A_HEADS + hid[h]:A_HEADS + hid[h] + 1] for h in heads]
        g_last = [gc_alls[seqs[h]][c - 1:c, hid[h]:hid[h] + 1] for h in heads]

        kq = [_dot_nt(jnp.concatenate([kh[h], qh[h]], axis=0), kh[h]) for h in heads]
        decay = [jnp.where(causal, jnp.exp(jnp.where(
            causal, gc_col[h] - gc_all_ts[seqs[h]][hid[h]:hid[h] + 1, :], 0.0)), 0.0) for h in heads]
        attn = [(kq[h][c:2 * c] * decay[h]).astype(BF16) for h in heads]

        p = [jnp.where(strict, -(kq[h][0:c] * beta[h] * decay[h]), 0.0) for h in heads]
        t_inv = [eye + p[h] for h in heads]
        for _ in range(n_doublings):
            pb = [p[h].astype(BF16) for h in heads]
            p = [_dot(pb[h], pb[h]) for h in heads]
            t_inv = [t_inv[h] + _dot(t_inv[h].astype(BF16), p[h].astype(BF16)) for h in heads]

        e_gc = [jnp.exp(gc_col[h]) for h in heads]
        khf = [kh[h].astype(F32) for h in heads]
        rhs = [jnp.concatenate([v_ref[seqs[h], :, lanes[h]].astype(F32) * beta[h],
                                khf[h] * (beta[h] * e_gc[h])], axis=1).astype(BF16)
               for h in heads]
        sol = [_dot(t_inv[h].astype(BF16), rhs[h]) for h in heads]

        slot = [seqs[h] * A_HEADS + hid[h] for h in heads]
        state = [state_ref[slot[h]] for h in heads]
        lhs = [jnp.concatenate([sol[h][:, A_DK:2 * A_DK], qh[h].astype(F32) * e_gc[h]],
                               axis=0).astype(BF16) for h in heads]
        ws = [_dot(lhs[h], state[h].astype(BF16)) for h in heads]
        v_new = [(sol[h][:, 0:A_DK] - ws[h][0:c]).astype(BF16) for h in heads]
        k_dec_t = [(khf[h] * jnp.exp(g_last[h] - gc_col[h])).T.astype(BF16) for h in heads]
        o = [ws[h][c:2 * c] + _dot(attn[h], v_new[h]) for h in heads]
        for h in heads:
            state_ref[slot[h]] = state[h] * jnp.exp(g_last[h]) + _dot(k_dec_t[h], v_new[h])
        for h in heads:
            on = o[h] * lax.rsqrt(jnp.mean(o[h] * o[h], -1, keepdims=True) + EPS) * ng
            gate = gate_ref[seqs[h], :, lanes[h]].astype(F32)
            o_ref[seqs[h], :, lanes[h]] = (on * _silu(gate)).astype(BF16)

    chains = [(s, g) for s in range(n_seq) for g in range(A_HEADS)]
    for first in range(0, len(chains), DN_CHAINS_PER_GROUP):
        chain_group(chains[first:first + DN_CHAINS_PER_GROUP])


def _dn_scan(q, k, v, gate, gb, norm_g_row, batch, seq):
    c = DN_CHUNK
    nc = seq // c
    n_seq = DN_SEQS_PER_STEP
    assert batch % n_seq == 0
    rows = lambda b, n: (b, n, 0)
    const = lambda b, n: (0, 0)
    wide = lambda a: a.reshape(batch, seq, a.shape[-1])
    out = pl.pallas_call(
        _dn_scan_kernel,
        grid=(batch // n_seq, nc),
        in_specs=[pl.BlockSpec((n_seq, c, D_MODEL), rows)] * 4
        + [pl.BlockSpec((n_seq, c, LANES), rows), pl.BlockSpec(norm_g_row.shape, const)],
        out_specs=pl.BlockSpec((n_seq, c, D_MODEL), rows),
        out_shape=jax.ShapeDtypeStruct((batch, seq, D_MODEL), BF16),
        scratch_shapes=[pltpu.VMEM((n_seq * A_HEADS, A_DK, A_DK), F32)],
        compiler_params=_cparams(("arbitrary", "arbitrary")),
        name="dn_scan",
    )(wide(q), wide(k), wide(v), wide(gate), wide(gb), norm_g_row)
    return out.reshape(batch * seq, D_MODEL)


def _at_in_kernel(x_ref, w_ref, qt_ref, k_ref, vt_ref):
    tm = k_ref.shape[1]
    xb = _load_tokens(x_ref, tm).astype(BF16)
    q = _dot(xb, w_ref[:, 0:D_MODEL]) * (B_DH ** -0.5 * LOG2E)
    lane = lax.broadcasted_iota(I32, q.shape, 1)
    first_map = (lane % (2 * B_DH)) < B_DH
    qt_ref[0] = jnp.where(first_map, q, 0.0).T.astype(BF16)
    qt_ref[1] = jnp.where(first_map, 0.0, q).T.astype(BF16)
    k = _dot(xb, w_ref[:, D_MODEL:2 * D_MODEL]).astype(BF16)
    vt = _dot(xb, w_ref[:, 2 * D_MODEL:3 * D_MODEL]).T.astype(BF16)
    ones = jnp.ones((V_ROWS - 2 * B_DH, tm), BF16)
    for h in range(B_HEADS):
        k_ref[h] = k[:, h * LANES:(h + 1) * LANES]
        vt_ref[h] = jnp.concatenate([vt[h * LANES:(h + 1) * LANES], ones], axis=0)


def _at_in(x2d, w_in):
    t = x2d.shape[0] // SLAB
    tm = ROW_TILE
    return pl.pallas_call(
        _at_in_kernel,
        grid=(t // tm,),
        in_specs=[pl.BlockSpec((tm * SLAB, LANES), lambda i: (i, 0)), pl.BlockSpec(w_in.shape, lambda i: (0, 0))],
        out_specs=[pl.BlockSpec((2, D_MODEL, tm), lambda i: (0, 0, i)),
                   pl.BlockSpec((B_HEADS, tm, LANES), lambda i: (0, i, 0)),
                   pl.BlockSpec((B_HEADS, V_ROWS, tm), lambda i: (0, 0, i))],
        out_shape=[jax.ShapeDtypeStruct((2, D_MODEL, t), BF16),
                   jax.ShapeDtypeStruct((B_HEADS, t, LANES), BF16),
                   jax.ShapeDtypeStruct((B_HEADS, V_ROWS, t), BF16)],
        compiler_params=_cparams(("parallel",)),
        name="at_in",
    )(x2d, w_in)


def _attn_kernel(scal_ref, q0_ref, q1_ref, q0n_ref, q1n_ref, k_ref, vt_ref, bias_ref, ng_ref, o_ref,
                 m_ref, acc_ref, sa_ref, sb_ref, sc_ref):
    tq = o_ref.shape[0]
    tk = tq
    i = pl.program_id(2)
    lam = scal_ref[0]
    out_scale = scal_ref[1]

    m_ref[...] = jnp.full(m_ref.shape, NEG_BIG, F32)
    acc_ref[...] = jnp.zeros(acc_ref.shape, F32)
    qts = (q0_ref[0], q1_ref[0])

    maps = range(2)

    def scores(j, s_ref, q_maps=qts):
        kb = k_ref[0, pl.ds(pl.multiple_of(j * tk, tk), tk), :]
        for mp in maps:
            s_ref[mp] = _dot(kb, q_maps[mp])

    def absorb(j, s_ref, bias):
        vtb = vt_ref[0, :, pl.ds(pl.multiple_of(j * tk, tk), tk)]
        s = [s_ref[mp] for mp in maps]
        if bias is not None:
            s = [x + bias for x in s]
        m_old = [m_ref[mp] for mp in maps]
        m_new = [jnp.maximum(m_old[mp], jnp.max(s[mp], 0, keepdims=True)) for mp in maps]
        alpha = [jnp.exp2(m_old[mp] - m_new[mp]) for mp in maps]
        p = [jnp.exp2(s[mp] - m_new[mp]) for mp in maps]
        pv = [_dot(vtb, p[mp].astype(BF16)) for mp in maps]
        for mp in maps:
            acc_ref[mp] = alpha[mp] * acc_ref[mp] + pv[mp]
            m_ref[mp] = m_new[mp]

    @pl.when(i == 0)
    def _():
        scores(0, sa_ref)

    @pl.when((i % 2 == 0) & (i >= 2))
    def _():
        scores(1, sb_ref)
        absorb(0, sc_ref, None)

    first = jnp.where(i % 2 == 1, 0, 1)

    def far_pair(jj, carry):
        j = first + 2 * jj
        scores(j + 1, sa_ref)
        absorb(j, sb_ref, None)
        scores(j + 2, sb_ref)
        absorb(j + 1, sa_ref, None)
        return carry

    lax.fori_loop(0, jnp.maximum(i - 1, 0) // 2, far_pair, 0)

    @pl.when(i >= 1)
    def _():
        scores(i, sa_ref)
        absorb(i - 1, sb_ref, bias_ref[0, 1])

    last = i == pl.num_programs(2) - 1
    next_qts = (q0n_ref[0], q1n_ref[0])

    @pl.when(last)
    def _():
        absorb(i, sa_ref, bias_ref[0, 0])

    @pl.when(jnp.logical_not(last) & (i % 2 == 0))
    def _():
        scores(0, sb_ref, next_qts)
        absorb(i, sa_ref, bias_ref[0, 0])

    @pl.when(jnp.logical_not(last) & (i % 2 == 1))
    def _():
        scores(0, sc_ref, next_qts)
        absorb(i, sa_ref, bias_ref[0, 0])

    dv = 2 * B_DH
    o_t = (acc_ref[0, 0:dv] * (1.0 / acc_ref[0, dv:dv + 1])
           - acc_ref[1, 0:dv] * (lam / acc_ref[1, dv:dv + 1]))
    o = o_t.T
    o = o * lax.rsqrt(jnp.mean(o * o, -1, keepdims=True) + EPS) * ng_ref[...] * out_scale
    o_ref[...] = o.astype(BF16)


def _attn(scalars, qt, k, vt, bias_tab, norm_g_row, batch, seq):
    tq = ATT_BLOCK
    nq = seq // tq
    t = batch * seq
    return pl.pallas_call(
        _attn_kernel,
        grid=(batch, B_HEADS, nq),
        in_specs=[
            pl.BlockSpec(memory_space=pltpu.SMEM),
            pl.BlockSpec((1, LANES, tq), lambda b, h, i: (0, h, b * nq + i)),
            pl.BlockSpec((1, LANES, tq), lambda b, h, i: (1, h, b * nq + i)),
            pl.BlockSpec((1, LANES, tq), lambda b, h, i: (0, h, b * nq + jnp.minimum(i + 1, nq - 1))),
            pl.BlockSpec((1, LANES, tq), lambda b, h, i: (1, h, b * nq + jnp.minimum(i + 1, nq - 1))),
            pl.BlockSpec((1, seq, LANES), lambda b, h, i: (h, b, 0)),
            pl.BlockSpec((1, V_ROWS, seq), lambda b, h, i: (h, 0, b)),
            pl.BlockSpec((1, 2, tq, tq), lambda b, h, i: (h, 0, 0, 0)),
            pl.BlockSpec(norm_g_row.shape, lambda b, h, i: (0, 0)),
        ],
        out_specs=pl.BlockSpec((tq, LANES), lambda b, h, i: (b * nq + i, h)),
        out_shape=jax.ShapeDtypeStruct((t, D_MODEL), BF16),
        scratch_shapes=[pltpu.VMEM((2, 1, tq), F32), pltpu.VMEM((2, V_ROWS, tq), F32),
                        pltpu.VMEM((2, tq, tq), F32), pltpu.VMEM((2, tq, tq), F32),
                        pltpu.VMEM((2, tq, tq), F32)],
        compiler_params=_cparams(("parallel", "parallel", "arbitrary")),
        name="attn",
    )(scalars, qt, qt, qt, qt, k, vt, bias_tab, norm_g_row)


def _rel_bucket(rel):
    n = jnp.maximum(rel, 0)
    max_exact = REL_BUCKETS // 2
    large = max_exact + (jnp.log(jnp.maximum(n, 1).astype(F32) / max_exact)
                         / math.log(REL_MAX_DIST / max_exact) * (REL_BUCKETS - max_exact)).astype(I32)
    large = jnp.minimum(large, REL_BUCKETS - 1)
    return jnp.where(n < max_exact, n, large)


def _attn_bias_table(rel_bias):
    tq = ATT_BLOCK
    assert tq + 1 >= REL_MAX_DIST, "blocks two back must fall in the last bucket"
    table = (rel_bias.astype(F32) - rel_bias.astype(F32)[REL_BUCKETS - 1]) * LOG2E
    period = 2 * tq
    idx = jnp.arange(period)
    diag_vec = jnp.where((idx < tq)[:, None], table[_rel_bucket(jnp.minimum(idx, tq - 1))], NEG_BIG)
    near_vec = table[_rel_bucket(jnp.where(idx < tq, idx + tq, idx - tq))]
    vecs = jnp.transpose(jnp.stack([diag_vec, near_vec], 0), (2, 0, 1))
    return pl.pallas_call(
        _toeplitz_kernel,
        grid=(B_HEADS, 2),
        in_specs=[pl.BlockSpec((1, 1, 1, period), lambda h, k: (h, k, 0, 0))],
        out_specs=pl.BlockSpec((1, 1, tq, tq), lambda h, k: (h, k, 0, 0)),
        out_shape=jax.ShapeDtypeStruct((B_HEADS, 2, tq, tq), F32),
        name="bias_table",
    )(vecs.reshape(B_HEADS, 2, 1, period))


def _toeplitz_kernel(v_ref, o_ref):
    tq = o_ref.shape[2]
    x = jnp.broadcast_to(v_ref[0, 0], (tq, v_ref.shape[3]))
    o_ref[0, 0] = pltpu.roll(x, 0, 1, stride=1, stride_axis=0)[:, :tq]


def _gelu(x):
    return 0.5 * x * (1.0 + lax.erf(x * (2.0 ** -0.5)))


def _gm_kernel(x_ref, w_ref, lng_ref, lnb_ref, ws_ref, bs_ref, o_ref):
    tm = o_ref.shape[0]
    xb = _load_tokens(x_ref, tm).astype(BF16)
    v = _gelu(_dot(xb, w_ref[:, C_HALF:2 * C_HALF]))
    vn = _layer_norm(v, lng_ref[...], lnb_ref[...]).astype(BF16)
    bs = bs_ref[...]
    for g in range(C_GROUPS):
        lo = g * C_GW
        u = _gelu(_dot(xb, w_ref[:, lo:lo + C_GW]))
        wg = ws_ref[g]
        for ch in range(tm // C_CHUNK):
            r0 = ch * C_CHUNK
            mixed = _dot(wg, vn[r0:r0 + C_CHUNK, lo:lo + C_GW]) + bs[:, g:g + 1]
            o_ref[r0:r0 + C_CHUNK, lo:lo + C_GW] = (u[r0:r0 + C_CHUNK] * mixed).astype(BF16)


def _gm(x2d, w_in, ln_g_row, ln_b_row, ws_causal, bs_t):
    t = x2d.shape[0] // SLAB
    tm = ROW_TILE
    rows = lambda i: (i, 0)
    const2 = lambda i: (0, 0)
    return pl.pallas_call(
        _gm_kernel,
        grid=(t // tm,),
        in_specs=[
            pl.BlockSpec((tm * SLAB, LANES), rows),
            pl.BlockSpec(w_in.shape, const2),
            pl.BlockSpec(ln_g_row.shape, const2),
            pl.BlockSpec(ln_b_row.shape, const2),
            pl.BlockSpec(ws_causal.shape, lambda i: (0, 0, 0)),
            pl.BlockSpec(bs_t.shape, const2),
        ],
        out_specs=pl.BlockSpec((tm, C_HALF), rows),
        out_shape=jax.ShapeDtypeStruct((t, C_HALF), BF16),
        compiler_params=_cparams(("parallel",)),
        name="gmlp",
    )(x2d, w_in, ln_g_row, ln_b_row, ws_causal, bs_t)


def _out_kernel(y_ref, w_ref, h_ref, lng_ref, lnb_ref, wr_ref, rb_ref, hs_ref, meta_ref, cnt_ref,
                carry_ref):
    tm = y_ref.shape[0]

    @pl.when(pl.program_id(0) == 0)
    def _():
        carry_ref[...] = jnp.zeros(carry_ref.shape, F32)

    n_part = OUT_TILE_PARTS
    rows_per = tm // n_part
    parts = range(n_part)
    lng, lnb = lng_ref[...], lnb_ref[...]
    proj = [_dot(y_ref[k * rows_per:(k + 1) * rows_per, :], w_ref[...]) for k in parts]
    h1 = [_layer_norm(DN_ALPHA * _load_tokens(h_ref, rows_per, k * rows_per) + proj[k], lng, lnb)
          for k in parts]
    for k in parts:
        _store_tokens(hs_ref, h1[k], rows_per, k * rows_per)

    h_hi = [h1[k].astype(BF16) for k in parts]
    h_lo = [(h1[k] - h_hi[k].astype(F32)).astype(BF16) for k in parts]
    both = [_dot_nt(wr_ref[...], h_hi[k]) for k in parts]
    low = [_dot_nt(wr_ref[0:N_EXPERTS, :], h_lo[k]) for k in parts]
    logits = jnp.concatenate(
        [both[k][0:N_EXPERTS] + both[k][LANES:LANES + N_EXPERTS] + low[k] for k in parts], axis=1)
    sel = jax.nn.sigmoid(logits) + jnp.concatenate([rb_ref[...]] * (tm // LANES), axis=1)
    rows = [sel[e:e + 1, :] for e in range(N_EXPERTS)]

    def top2_sum(v):
        sums = [v[a] + v[b] for a, b in _PAIRS]
        return functools.reduce(jnp.maximum, sums)

    grp = [top2_sum(rows[g * EXPERTS_PER_GROUP:(g + 1) * EXPERTS_PER_GROUP]) for g in range(N_GROUPS)]
    best, best_val = jnp.zeros_like(grp[0]), grp[0]
    for g in range(1, N_GROUPS):
        better = grp[g] > best_val
        best = jnp.where(better, float(g), best)
        best_val = jnp.where(better, grp[g], best_val)
    member = []
    for k in range(EXPERTS_PER_GROUP):
        v = rows[k]
        for g in range(1, N_GROUPS):
            v = jnp.where(best == float(g), rows[g * EXPERTS_PER_GROUP + k], v)
        member.append(v)
    chosen = []
    for k in range(EXPERTS_PER_GROUP):
        beaten = jnp.zeros_like(best)
        for j in range(EXPERTS_PER_GROUP):
            if j != k:
                ahead = (member[j] >= member[k]) if j < k else (member[j] > member[k])
                beaten = beaten + ahead.astype(F32)
        chosen.append(beaten < 2.0)
    cls = best * float(len(_PAIRS))
    for idx, (a, b) in enumerate(_PAIRS):
        cls = cls + jnp.where(chosen[a] & chosen[b], float(idx), 0.0)

    n_cls_rows = carry_ref.shape[0]
    cls_row = lax.broadcasted_iota(I32, (n_cls_rows, tm), 0).astype(F32)
    onehot = (cls_row == cls).astype(F32)
    t_i = lax.broadcasted_iota(I32, (tm, tm), 0)
    t_j = lax.broadcasted_iota(I32, (tm, tm), 1)
    carry = carry_ref[...]
    before = _dot(onehot.astype(BF16), (t_i < t_j).astype(BF16)) + jnp.concatenate(
        [carry] * (tm // LANES), axis=1)
    rank = jnp.sum(onehot * before, 0, keepdims=True)
    carry_ref[...] = carry + jnp.sum(onehot, 1, keepdims=True)
    cnt_ref[...] = carry_ref[...]

    row_id = lax.broadcasted_iota(I32, meta_ref.shape, 0)
    meta_ref[...] = jnp.where(row_id == 0, cls, jnp.where(row_id == 1, rank, 0.0))


def _out_proj(y, w_out, hs, ln_g_row, ln_b_row, wr_split, rbias_row):
    t, kd = y.shape
    tm = OUT_TILE
    rows = lambda i: (i, 0)
    const = lambda i: (0, 0)
    return pl.pallas_call(
        _out_kernel,
        grid=(t // tm,),
        in_specs=[
            pl.BlockSpec((tm, kd), rows),
            pl.BlockSpec(w_out.shape, const),
            _token_spec(hs, tm, rows),
            pl.BlockSpec(ln_g_row.shape, const),
            pl.BlockSpec(ln_b_row.shape, const),
            pl.BlockSpec(wr_split.shape, const),
            pl.BlockSpec(rbias_row.shape, const),
        ],
        out_specs=[pl.BlockSpec((tm * SLAB, LANES), rows), pl.BlockSpec((SUBLANES, tm), lambda i: (0, i)),
                   pl.BlockSpec((CLASS_ROWS, LANES), const)],
        out_shape=[jax.ShapeDtypeStruct((t * SLAB, LANES), F32),
                   jax.ShapeDtypeStruct((SUBLANES, t), F32),
                   jax.ShapeDtypeStruct((CLASS_ROWS, LANES), F32)],
        scratch_shapes=[pltpu.VMEM((CLASS_ROWS, LANES), F32)],
        compiler_params=_cparams(("arbitrary",)),
        name="out_proj",
    )(y, w_out, hs, ln_g_row, ln_b_row, wr_split, rbias_row)


def _permute_kernel(pos_ref, src_ref, *refs, scatter, row_major_out):
    if row_major_out:
        o_ref, sem, dst_ref = refs[-3:]
    else:
        dst_ref, sem = refs[-2:]
    n_tok = pos_ref.shape[2]

    def issue(g, carry):
        for u in range(PERMUTE_UNROLL):
            r = g * PERMUTE_UNROLL + u
            here = pl.ds(pl.multiple_of(r * SLAB, SLAB), SLAB)
            there = pl.ds(pl.multiple_of(pos_ref[0, 0, r] * SLAB, SLAB), SLAB)
            src_rows, dst_rows = (here, there) if scatter else (there, here)
            pltpu.make_async_copy(src_ref.at[src_rows, :], dst_ref.at[dst_rows, :], sem).start(
                priority=u % 2)
        return carry

    lax.fori_loop(0, n_tok // PERMUTE_UNROLL, issue, 0)
    whole = pl.ds(0, n_tok * SLAB)
    pltpu.make_async_copy(src_ref.at[whole, :], dst_ref.at[whole, :], sem).wait()
    if row_major_out:
        o_ref[...] = _load_tokens(dst_ref, n_tok)


def _permute(pos, src, n_dst_rows, scatter, row_major_out=False, zeroed_dst=None):
    n_tok = pos.shape[2]
    any_spec = pl.BlockSpec(memory_space=pl.ANY)
    tile_spec = pl.BlockSpec((n_tok * SLAB, LANES), lambda i: (i, 0))
    out_shape = jax.ShapeDtypeStruct((n_dst_rows, LANES), F32)
    scratch = [pltpu.SemaphoreType.DMA(())]
    if row_major_out:
        assert not scatter
        tile_spec = pl.BlockSpec((n_tok, D_MODEL), lambda i: (i, 0))
        out_shape = jax.ShapeDtypeStruct((n_dst_rows // SLAB, D_MODEL), F32)
        scratch.append(pltpu.VMEM((n_tok * SLAB, LANES), F32))
    extra = []
    if scatter:
        extra = [jnp.zeros((n_dst_rows, LANES), F32) if zeroed_dst is None else zeroed_dst]
    return pl.pallas_call(
        functools.partial(_permute_kernel, scatter=scatter, row_major_out=row_major_out),
        grid=(pos.shape[0],),
        in_specs=[pl.BlockSpec((1, 1, n_tok), lambda i: (i, 0, 0), memory_space=pltpu.SMEM),
                  tile_spec if scatter else any_spec] + [any_spec] * len(extra),
        out_specs=any_spec if scatter else tile_spec,
        out_shape=out_shape,
        scratch_shapes=scratch,
        input_output_aliases={2: 0} if scatter else {},
        compiler_params=_cparams(("arbitrary",)),
        name="dispatch" if scatter else "unpermute",
    )(pos, src, *extra)


def _moe_kernel(lo_ref, hi_ref, nv_ref, xs_ref, w1l_ref, w3l_ref, w2l_ref, w1h_ref, w3h_ref, w2h_ref,
                wrl_ref, wrh_ref, lng_ref, lnb_ref, ys_ref, zeros_ref, wl_bf_ref, wh_bf_ref, w2_bf_ref):
    zeros_ref[...] = jnp.zeros(zeros_ref.shape, F32)
    tm = xs_ref.shape[0] // SLAB

    i = pl.program_id(0)
    prev = jnp.maximum(i - 1, 0)
    valid = i < nv_ref[0]

    @pl.when(valid & ((i == 0) | (lo_ref[i] != lo_ref[prev])))
    def _():
        wl_bf_ref[0] = w1l_ref[0, 0].astype(BF16)
        wl_bf_ref[1] = w3l_ref[0, 0].astype(BF16)
        w2_bf_ref[0] = w2l_ref[0, 0].astype(BF16)

    @pl.when(valid & ((i == 0) | (hi_ref[i] != hi_ref[prev])))
    def _():
        wh_bf_ref[0] = w1h_ref[0, 0].astype(BF16)
        wh_bf_ref[1] = w3h_ref[0, 0].astype(BF16)
        w2_bf_ref[1] = w2h_ref[0, 0].astype(BF16)

    @pl.when(valid)
    def _():
        n_grp = MOE_TILE_PARTS
        rows = tm // n_grp
        up = (wl_bf_ref, wh_bf_ref)
        chains = [(g, e) for g in range(n_grp) for e in range(2)]
        x = [_load_tokens(xs_ref, rows, g * rows) for g in range(n_grp)]
        xb = [x[g].astype(BF16) for g in range(n_grp)]
        gate_in = {(g, e): _dot(xb[g], up[e][0]) for g, e in chains}
        lin_in = {(g, e): _dot(xb[g], up[e][1]) for g, e in chains}
        hid = {c: (_silu(gate_in[c]) * lin_in[c]).astype(BF16) for c in chains}
        y = {(g, e): _dot(hid[(g, e)], w2_bf_ref[e]) for g, e in chains}
        for g in range(n_grp):
            s_lo = jax.nn.sigmoid(jnp.sum(x[g] * wrl_ref[0], -1, keepdims=True))
            s_hi = jax.nn.sigmoid(jnp.sum(x[g] * wrh_ref[0], -1, keepdims=True))
            denom = s_lo + s_hi
            m = (s_lo / denom) * y[(g, 0)] + (s_hi / denom) * y[(g, 1)]
            out = _layer_norm(DN_ALPHA * x[g] + m, lng_ref[...], lnb_ref[...])
            _store_tokens(ys_ref, out, rows, g * rows)

    @pl.when(pl.program_id(0) >= nv_ref[0])
    def _():
        ys_ref[...] = jnp.zeros(ys_ref.shape, F32)


def _moe(tile_lo, tile_hi, n_valid, xs, w1, w3, w2, layer, wr_rows, ln_g_row, ln_b_row):
    tm = MOE_TILE
    n_tiles = xs.shape[0] // (tm * SLAB)
    rows = lambda i, lo, hi, nv: (jnp.minimum(i, nv[0] - 1), 0)
    w_lo = lambda i, lo, hi, nv: (layer, lo[i], 0, 0)
    w_hi = lambda i, lo, hi, nv: (layer, hi[i], 0, 0)
    r_lo = lambda i, lo, hi, nv: (lo[i], 0, 0)
    r_hi = lambda i, lo, hi, nv: (hi[i], 0, 0)
    const = lambda i, lo, hi, nv: (0, 0)
    up = (1, 1, D_MODEL, D_EXPERT)
    down = (1, 1, D_EXPERT, D_MODEL)
    router_row = (1, 1, D_MODEL)
    grid_spec = pltpu.PrefetchScalarGridSpec(
        num_scalar_prefetch=3,
        grid=(n_tiles,),
        in_specs=[
            pl.BlockSpec((tm * SLAB, LANES), rows),
            pl.BlockSpec(up, w_lo), pl.BlockSpec(up, w_lo), pl.BlockSpec(down, w_lo),
            pl.BlockSpec(up, w_hi), pl.BlockSpec(up, w_hi), pl.BlockSpec(down, w_hi),
            pl.BlockSpec(router_row, r_lo), pl.BlockSpec(router_row, r_hi),
            pl.BlockSpec(ln_g_row.shape, const), pl.BlockSpec(ln_b_row.shape, const),
        ],
        out_specs=[pl.BlockSpec((tm * SLAB, LANES), lambda i, lo, hi, nv: (i, 0))] * 2,
        scratch_shapes=[pltpu.VMEM((2, D_MODEL, D_EXPERT), BF16), pltpu.VMEM((2, D_MODEL, D_EXPERT), BF16),
                        pltpu.VMEM((2, D_EXPERT, D_MODEL), BF16)],
    )
    return pl.pallas_call(
        _moe_kernel,
        grid_spec=grid_spec,
        out_shape=[jax.ShapeDtypeStruct(xs.shape, F32)] * 2,
        compiler_params=_cparams(("arbitrary",)),
        name="moe",
    )(tile_lo, tile_hi, n_valid, xs, w1, w3, w2, w1, w3, w2, wr_rows, wr_rows, ln_g_row, ln_b_row)


def _routing_plan(meta, counts, n_slots):
    cls = meta[0].astype(I32)
    rank = meta[1].astype(I32)
    cnt = counts[:N_PAIR_CLASSES, 0].astype(I32)
    padded = ((cnt + MOE_TILE - 1) // MOE_TILE) * MOE_TILE
    end = jnp.cumsum(padded)
    start = end - padded
    classes = jnp.arange(N_PAIR_CLASSES, dtype=I32)
    pos = jnp.sum(jnp.where(cls[:, None] == classes[None, :], start[None, :], 0), -1) + rank
    n_tiles = n_slots // MOE_TILE
    n_valid = (end[-1] // MOE_TILE).astype(I32).reshape(1)
    tile_start = jnp.minimum(jnp.arange(n_tiles, dtype=I32), n_valid[0] - 1) * MOE_TILE
    tile_cls = jnp.sum((tile_start[:, None] >= end[None, :]).astype(I32), -1)
    onehot = (tile_cls[:, None] == classes[None, :]).astype(I32)
    tile_lo = jnp.sum(onehot * jnp.asarray(_PAIR_LO)[None, :], -1)
    tile_hi = jnp.sum(onehot * jnp.asarray(_PAIR_HI)[None, :], -1)
    return pos, tile_lo, tile_hi, n_valid


def _moe_layer(hs, meta, counts, w1, w3, w2, layer, wr_rows, ln_g_row, ln_b_row, row_major_out,
               zeroed_slots):
    t = meta.shape[1]
    n_slots = t + N_PAIR_CLASSES * MOE_TILE
    pos, tile_lo, tile_hi, n_valid = _routing_plan(meta, counts, n_slots)
    pos = pos.reshape(t // PERMUTE_TOKENS, 1, PERMUTE_TOKENS)
    xs = _permute(pos, hs, n_slots * SLAB, scatter=True, zeroed_dst=zeroed_slots)
    ys, zeroed_next = _moe(tile_lo, tile_hi, n_valid, xs, w1, w3, w2, layer, wr_rows, ln_g_row, ln_b_row)
    return _permute(pos, ys, t * SLAB, scatter=False, row_major_out=row_major_out), zeroed_next


def _row(v, width=None):
    v = v.astype(F32).reshape(1, -1)
    if width is not None and v.shape[1] < width:
        v = jnp.pad(v, ((0, 0), (0, width - v.shape[1])))
    return v


def _deltanet(h2d, batch, seq, w_in, conv_w, a_log, dt_bias, norm_g):
    qkvg = 4 * D_MODEL
    w_big = w_in.astype(BF16)
    w_ab = jnp.pad(w_in[:, qkvg:], ((0, 0), (0, LANES - 2 * A_HEADS))).astype(BF16)
    q, k, v, gate, gb = _dn_in(h2d, batch, seq, w_big, w_ab, conv_w.astype(F32),
                               _row(a_log, LANES), _row(dt_bias, LANES))
    return _dn_scan(q, k, v, gate, gb, _row(norm_g), batch, seq)


def _diff_attention(h2d, batch, seq, w_in, lam, norm_g, rel_bias, lambda_init):
    lamf = lam.astype(F32)
    lam_full = jnp.exp(jnp.sum(lamf[0] * lamf[1])) - jnp.exp(jnp.sum(lamf[2] * lamf[3])) + lambda_init
    scalars = jnp.stack([lam_full, jnp.asarray(1.0 - lambda_init, F32)]).astype(F32)
    qt, k, vt = _at_in(h2d, w_in.astype(BF16))
    return _attn(scalars, qt, k, vt, _attn_bias_table(rel_bias), _row(norm_g), batch, seq)


def _gmlp(h2d, w_in, ln_g, ln_b, w_s, b_s):
    ws_causal = jnp.tril(w_s.astype(F32)).astype(BF16)
    bs_t = jnp.pad(jnp.transpose(b_s.astype(F32)), ((0, 0), (0, LANES - C_GROUPS)))
    return _gm(h2d, w_in.astype(BF16), _row(ln_g), _row(ln_b), ws_causal, bs_t)


def kernel(x, ln_mix_g, ln_mix_b, ln_ffn_g, ln_ffn_b, a_w_in, a_conv, a_a_log, a_dt_bias, a_norm_g, a_w_out, b_w_in, b_lambda, b_norm_g, b_w_out, rel_bias, c_w_in, c_ln_g, c_ln_b, c_w_s, c_b_s, c_w_out, w_router, router_bias, e_w1, e_w3, e_w2):
    batch, seq, d = x.shape
    assert d == D_MODEL and seq % max(ROW_TILE, ATT_BLOCK, DN_CHUNK) == 0
    t = batch * seq
    h = x.astype(F32).reshape(t, d)

    wr_rows = jnp.transpose(w_router.astype(F32)).reshape(N_EXPERTS, 1, D_MODEL)
    wr = jnp.pad(jnp.transpose(w_router.astype(F32)), ((0, LANES - N_EXPERTS), (0, 0)))
    wr_hi = wr.astype(BF16)
    wr_lo = (wr - wr_hi.astype(F32)).astype(BF16)
    wr_split = jnp.concatenate([wr_hi, wr_lo], axis=0)
    rbias_row = jnp.broadcast_to(router_bias.astype(F32)[:, None], (N_EXPERTS, LANES))

    w1_all, w3_all, w2_all = e_w1, e_w3, e_w2

    zeroed_slots = None
    for i in range(DEPTH):
        kind, j = i % N_MIXERS, i // N_MIXERS
        if kind == 0:
            y = _deltanet(h, batch, seq, a_w_in[j], a_conv[j], a_a_log[j], a_dt_bias[j], a_norm_g[j])
            w_out = a_w_out[j]
        elif kind == 1:
            lambda_init = 0.8 - 0.6 * math.exp(-0.3 * i)
            y = _diff_attention(h, batch, seq, b_w_in[j], b_lambda[j], b_norm_g[j], rel_bias, lambda_init)
            w_out = b_w_out[j]
        else:
            y = _gmlp(h, c_w_in[j], c_ln_g[j], c_ln_b[j], c_w_s[j], c_b_s[j])
            w_out = c_w_out[j]
        hs, meta, counts = _out_proj(y, w_out.astype(BF16), h, _row(ln_mix_g[i]), _row(ln_mix_b[i]),
                                     wr_split, rbias_row)
        h, zeroed_slots = _moe_layer(hs, meta, counts, w1_all, w3_all, w2_all, i, wr_rows,
                                     _row(ln_ffn_g[i]), _row(ln_ffn_b[i]),
                                     row_major_out=(i == DEPTH - 1), zeroed_slots=zeroed_slots)
    return h.reshape(batch, seq, d).astype(x.dtype)
```
